```python
import jax, jax.numpy as jnp
from jax import lax
import numpy as np

D_MODEL = 1024
BATCH = 2
SEQ = 16384
DEPTH = 1

CHUNK = 64
LEFT_CHUNKS = 8
BAND = LEFT_CHUNKS + 1
CONV_WIDTH = D_MODEL // 2
CONV_GROUP = 64
CONV_GROUPS = CONV_WIDTH // CONV_GROUP
CONV_K = 3
HEAD_DIM = 64
N_HEADS = (D_MODEL - CONV_WIDTH) // HEAD_DIM
ATTN_WIDTH = N_HEADS * HEAD_DIM
MIX_WIDTH = CONV_WIDTH + ATTN_WIDTH
IN_COLS = 3 * CONV_WIDTH + 3 * ATTN_WIDTH
REL_CLIP = 128
REL_TABLE = (CHUNK - 1) + REL_CLIP + 1
D_FF = 4 * D_MODEL
EPS = 1e-6
NEG_INF = -1e30

kernel_name = "hybrid_conv_chunkattn_block"


def rms_norm(x, g):
    xf = x.astype(jnp.float32)
    y = xf * lax.rsqrt(jnp.mean(xf * xf, axis=-1, keepdims=True) + EPS)
    return (y * g.astype(jnp.float32)).astype(x.dtype)


def group_rms_norm(y, g, group):
    shp = y.shape
    yf = y.astype(jnp.float32).reshape(shp[:-1] + (shp[-1] // group, group))
    yf = yf * lax.rsqrt(jnp.mean(yf * yf, axis=-1, keepdims=True) + EPS)
    return (yf.reshape(shp) * g.astype(jnp.float32)).astype(y.dtype)


def short_gated_conv(b, c, h, w_conv, b_conv):
    u = c * h
    y = lax.conv_general_dilated(
        u, w_conv[:, None, :].astype(u.dtype),
        window_strides=(1,), padding=[(CONV_K - 1, 0)],
        dimension_numbers=("NWC", "WIO", "NWC"),
        feature_group_count=CONV_WIDTH)
    return b * (y + b_conv)


def chunk_band_attention(q, k, v, q_gain, k_gain, rel_bias):
    bsz, s, _ = q.shape
    n = s // CHUNK
    q = q.reshape(bsz, n, CHUNK, N_HEADS, HEAD_DIM)
    k = k.reshape(bsz, n, CHUNK, N_HEADS, HEAD_DIM)
    v = v.reshape(bsz, n, CHUNK, N_HEADS, HEAD_DIM)
    q = rms_norm(q, q_gain)
    k = rms_norm(k, k_gain)
    pad = ((0, 0), (LEFT_CHUNKS, 0), (0, 0), (0, 0), (0, 0))
    kp = jnp.pad(k, pad)
    vp = jnp.pad(v, pad)
    k_band = jnp.concatenate([kp[:, o:o + n] for o in range(BAND)], axis=2)
    v_band = jnp.concatenate([vp[:, o:o + n] for o in range(BAND)], axis=2)
    scores = jnp.einsum("bnqhd,bnkhd->bnhqk", q, k_band).astype(jnp.float32)
    scores = scores * (HEAD_DIM ** -0.5)
    q_idx = jnp.arange(CHUNK)[:, None] + LEFT_CHUNKS * CHUNK
    k_idx = jnp.arange(BAND * CHUNK)[None, :]
    dist = jnp.clip(q_idx - k_idx, -(CHUNK - 1), REL_CLIP) + (CHUNK - 1)
    bias = rel_bias[:, dist].astype(jnp.float32)
    scores = scores + bias[None, None]
    key_chunk = (jnp.arange(n)[:, None] - LEFT_CHUNKS
                 + (jnp.arange(BAND * CHUNK) // CHUNK)[None, :])
    valid = key_chunk >= 0
    scores = jnp.where(valid[None, :, None, None, :], scores, NEG_INF)
    p = jax.nn.softmax(scores, axis=-1).astype(v.dtype)
    o = jnp.einsum("bnhqk,bnkhd->bnqhd", p, v_band)
    return o.reshape(bsz, s, ATTN_WIDTH)


def setup_inputs(seed: int = 0) -> dict:
    key = jax.random.key(seed)
    ks = jax.random.split(key, 16)
    f32 = jnp.float32
    nrm = lambda k, shp, sc: jax.random.normal(k, shp, f32) * sc
    gain = lambda k, shp: 1.0 + 0.01 * jax.random.normal(k, shp, f32)
    return {
        "x": jax.random.normal(ks[0], (BATCH, SEQ, D_MODEL), f32),
        "norm_mix_g": gain(ks[1], (DEPTH, D_MODEL)),
        "w_in": nrm(ks[2], (DEPTH, D_MODEL, IN_COLS), D_MODEL ** -0.5),
        "conv_w": nrm(ks[3], (DEPTH, CONV_K, CONV_WIDTH), CONV_K ** -0.5),
        "conv_b": nrm(ks[4], (DEPTH, CONV_WIDTH), 0.01),
        "q_norm_g": gain(ks[5], (DEPTH, HEAD_DIM)),
        "k_norm_g": gain(ks[6], (DEPTH, HEAD_DIM)),
        "rel_bias": nrm(ks[7], (DEPTH, N_HEADS, REL_TABLE), 0.1),
        "conv_out_g": gain(ks[8], (DEPTH, CONV_WIDTH)),
        "attn_out_g": gain(ks[9], (DEPTH, ATTN_WIDTH)),
        "w_out": nrm(ks[10], (DEPTH, MIX_WIDTH, D_MODEL), MIX_WIDTH ** -0.5),
        "norm_mlp_g": gain(ks[11], (DEPTH, D_MODEL)),
        "w_mlp_in": nrm(ks[12], (DEPTH, D_MODEL, D_FF), D_MODEL ** -0.5),
        "w_mlp_out": nrm(ks[13], (DEPTH, D_FF, D_MODEL), D_FF ** -0.5),
    }


def reference(x, norm_mix_g, w_in, conv_w, conv_b, q_norm_g, k_norm_g, rel_bias,
              conv_out_g, attn_out_g, w_out, norm_mlp_g, w_mlp_in, w_mlp_out):
    c0 = CONV_WIDTH
    a0 = 3 * CONV_WIDTH
    for l in range(DEPTH):
        xn = rms_norm(x, norm_mix_g[l])
        z = jnp.einsum("bsd,dc->bsc", xn, w_in[l])
        gb = z[..., 0:c0]
        gc = z[..., c0:2 * c0]
        hh = z[..., 2 * c0:3 * c0]
        q = z[..., a0:a0 + ATTN_WIDTH]
        k = z[..., a0 + ATTN_WIDTH:a0 + 2 * ATTN_WIDTH]
        v = z[..., a0 + 2 * ATTN_WIDTH:a0 + 3 * ATTN_WIDTH]
        y_conv = short_gated_conv(gb, gc, hh, conv_w[l], conv_b[l])
        y_attn = chunk_band_attention(q, k, v, q_norm_g[l], k_norm_g[l], rel_bias[l])
        y_mix = jnp.concatenate(
            [group_rms_norm(y_conv, conv_out_g[l], CONV_GROUP),
             group_rms_norm(y_attn, attn_out_g[l], HEAD_DIM)], axis=-1)
        x = x + jnp.einsum("bsc,cd->bsd", y_mix, w_out[l])
        xn = rms_norm(x, norm_mlp_g[l])
        hid = jnp.square(jax.nn.relu(jnp.einsum("bsd,df->bsf", xn, w_mlp_in[l])))
        x = x + jnp.einsum("bsf,fd->bsd", hid, w_mlp_out[l])
    return x
```

```python
import functools

import jax
import jax.numpy as jnp
from jax import lax
from jax.experimental import pallas as pl
from jax.experimental.pallas import tpu as pltpu

D_MODEL = 1024
CHUNK = 64
LEFT_CHUNKS = 8
BAND = LEFT_CHUNKS + 1
BAND_KEYS = BAND * CHUNK
CONV_WIDTH = D_MODEL // 2
GROUP = 64
CONV_K = 3
HEAD_DIM = 64
N_HEADS = (D_MODEL - CONV_WIDTH) // HEAD_DIM
ATTN_WIDTH = N_HEADS * HEAD_DIM
IN_COLS = 3 * CONV_WIDTH + 3 * ATTN_WIDTH
REL_CLIP = 128
D_FF = 4 * D_MODEL
EPS = 1e-6
NEG_INF = -1e30

LANES = 128
SUBLANES = 8
HEADS_PER_LANE_TILE = LANES // HEAD_DIM
SEQ_TILE = LEFT_CHUNKS * CHUNK
MLP_TILE = 512
FF_CHUNK = 1024
VMEM_LIMIT = 48 * 1024 * 1024

F32 = jnp.float32
BF16 = jnp.bfloat16


def _resident(shape):
    return pl.BlockSpec(shape, lambda *_: (0,) * len(shape), pipeline_mode=pl.Buffered(1))


def _group_norm(y, gmat, gain):
    ms = jnp.dot((y * y).astype(BF16), gmat, preferred_element_type=F32)
    return y * lax.rsqrt(ms + EPS) * gain


def _rms_norm_bf16(x, gain):
    ms = jnp.mean(x * x, axis=-1, keepdims=True)
    return (x * lax.rsqrt(ms + EPS) * gain).astype(BF16)


def _mix_proj_kernel(x_ref, g_ref, w_ref, cw_ref, cb_ref, qg_ref, kg_ref, cog_ref, gm_ref,
                     yc_ref, q_ref, k_ref, v_ref, carry_ref):
    i = pl.program_id(1)

    @pl.when(i == 0)
    def _():
        carry_ref[...] = jnp.zeros_like(carry_ref)

    xn = _rms_norm_bf16(x_ref[...], g_ref[...])
    gmat = gm_ref[...]

    def proj(j):
        return jnp.dot(xn, w_ref[:, j * CONV_WIDTH:(j + 1) * CONV_WIDTH],
                       preferred_element_type=F32)

    u = proj(1) * proj(2)
    rows = lax.broadcasted_iota(jnp.int32, u.shape, 0)
    prev2 = carry_ref[SUBLANES - 2:SUBLANES - 1, :]
    prev1 = carry_ref[SUBLANES - 1:SUBLANES, :]
    u1 = jnp.where(rows == 0, prev1, pltpu.roll(u, 1, 0))
    u2 = jnp.where(rows == 0, prev2, jnp.where(rows == 1, prev1, pltpu.roll(u, 2, 0)))
    carry_ref[...] = u[SEQ_TILE - SUBLANES:, :]
    y = cw_ref[0:1, :] * u2 + cw_ref[1:2, :] * u1 + cw_ref[2:3, :] * u
    yc = proj(0) * (y + cb_ref[...])
    yc_ref[...] = _group_norm(yc, gmat, cog_ref[...]).astype(BF16)

    q_ref[...] = (_group_norm(proj(3), gmat, qg_ref[...]) * (HEAD_DIM ** -0.5)).astype(BF16)
    k_ref[...] = _group_norm(proj(4), gmat, kg_ref[...]).astype(BF16)
    v_ref[...] = proj(5).astype(BF16)


def _mix_proj(x, g, w_in, conv_w, conv_b, qg, kg, cog, gmat):
    bsz, seq, _ = x.shape
    tile = pl.BlockSpec((None, SEQ_TILE, D_MODEL), lambda b, i: (b, i, 0))
    out_tile = pl.BlockSpec((None, SEQ_TILE, CONV_WIDTH), lambda b, i: (b, i, 0))
    out_sds = jax.ShapeDtypeStruct((bsz, seq, CONV_WIDTH), BF16)
    return pl.pallas_call(
        _mix_proj_kernel,
        grid=(bsz, seq // SEQ_TILE),
        in_specs=[tile, _resident((1, D_MODEL)), _resident((D_MODEL, IN_COLS)),
                  _resident((CONV_K, CONV_WIDTH)), _resident((1, CONV_WIDTH)),
                  _resident((1, ATTN_WIDTH)), _resident((1, ATTN_WIDTH)),
                  _resident((1, CONV_WIDTH)), _resident((CONV_WIDTH, CONV_WIDTH))],
        out_specs=[out_tile] * 4,
        out_shape=[out_sds] * 4,
        scratch_shapes=[pltpu.VMEM((SUBLANES, CONV_WIDTH), F32)],
        compiler_params=pltpu.CompilerParams(
            dimension_semantics=("arbitrary", "arbitrary"), vmem_limit_bytes=VMEM_LIMIT),
        name="mix_proj",
    )(x, g, w_in, conv_w, conv_b, qg, kg, cog, gmat)


def _attn_out_kernel(x_ref, yc_ref, q_ref, kp_ref, kc_ref, vp_ref, vc_ref, bias_ref,
                     aog_ref, gm_ref, wo_ref, h_ref, kb_ref, vb_ref, ya_ref):
    i = pl.program_id(1)
    kb_ref[0:SEQ_TILE, :] = kp_ref[...]
    kb_ref[SEQ_TILE:, :] = kc_ref[...]
    vb_ref[0:SEQ_TILE, :] = vp_ref[...]
    vb_ref[SEQ_TILE:, :] = vc_ref[...]

    lane = lax.broadcasted_iota(jnp.int32, (CHUNK, LANES), 1)
    first_head = lane < HEAD_DIM
    col = lax.broadcasted_iota(jnp.int32, (CHUNK, BAND_KEYS), 1)

    def chunk_body(c, carry):
        r0 = pl.multiple_of(c * CHUNK, CHUNK)
        thresh = jnp.where(i == 0, SEQ_TILE - c * CHUNK, 0)
        valid = col >= thresh
        for p in range(N_HEADS // HEADS_PER_LANE_TILE):
            cols = slice(p * LANES, (p + 1) * LANES)
            qp = q_ref[pl.ds(r0, CHUNK), cols]
            kband = kb_ref[pl.ds(r0, BAND_KEYS), cols]
            vband = vb_ref[pl.ds(r0, BAND_KEYS), cols]
            outs = []
            for e in range(HEADS_PER_LANE_TILE):
                h = p * HEADS_PER_LANE_TILE + e
                keep = first_head if e == 0 else jnp.logical_not(first_head)
                qh = jnp.where(keep, qp, jnp.zeros_like(qp))
                s = lax.dot_general(qh, kband, (((1,), (1,)), ((), ())),
                                    preferred_element_type=F32)
                s = s + bias_ref[h]
                s = jnp.where(valid, s, NEG_INF)
                m = jnp.max(s, axis=-1, keepdims=True)
                pr = jnp.exp(s - m)
                denom = jnp.sum(pr, axis=-1, keepdims=True)
                o = jnp.dot(pr.astype(BF16), vband, preferred_element_type=F32)
                outs.append(o / denom)
            ya_ref[pl.ds(r0, CHUNK), cols] = jnp.where(first_head, outs[0], outs[1])
        return carry

    lax.fori_loop(0, SEQ_TILE // CHUNK, chunk_body, 0)

    yn = _group_norm(ya_ref[...], gm_ref[...], aog_ref[...]).astype(BF16)
    mixed = jnp.dot(yc_ref[...], wo_ref[0:CONV_WIDTH, :], preferred_element_type=F32)
    mixed = mixed + jnp.dot(yn, wo_ref[CONV_WIDTH:, :], preferred_element_type=F32)
    h_ref[...] = x_ref[...] + mixed


def _attn_out(x, yc, q, k, v, bias, aog, gmat, w_out):
    bsz, seq, _ = x.shape
    x_tile = pl.BlockSpec((None, SEQ_TILE, D_MODEL), lambda b, i: (b, i, 0))
    cur = pl.BlockSpec((None, SEQ_TILE, ATTN_WIDTH), lambda b, i: (b, i, 0))
    prev = pl.BlockSpec((None, SEQ_TILE, ATTN_WIDTH), lambda b, i: (b, jnp.maximum(i - 1, 0), 0))
    return pl.pallas_call(
        _attn_out_kernel,
        grid=(bsz, seq // SEQ_TILE),
        in_specs=[x_tile, cur, cur, prev, cur, prev, cur,
                  _resident((N_HEADS, CHUNK, BAND_KEYS)), _resident((1, ATTN_WIDTH)),
                  _resident((ATTN_WIDTH, ATTN_WIDTH)), _resident((D_MODEL, D_MODEL))],
        out_specs=x_tile,
        out_shape=jax.ShapeDtypeStruct(x.shape, F32),
        scratch_shapes=[pltpu.VMEM((2 * SEQ_TILE, ATTN_WIDTH), BF16),
                        pltpu.VMEM((2 * SEQ_TILE, ATTN_WIDTH), BF16),
                        pltpu.VMEM((SEQ_TILE, ATTN_WIDTH), F32)],
        compiler_params=pltpu.CompilerParams(
            dimension_semantics=("arbitrary", "arbitrary"), vmem_limit_bytes=VMEM_LIMIT),
        name="attn_out",
    )(x, yc, q, k, k, v, v, bias, aog, gmat, w_out)


def _mlp_kernel(h_ref, g_ref, w1_ref, w2_ref, o_ref):
    h = h_ref[...]
    xn = _rms_norm_bf16(h, g_ref[...])
    acc = h
    for c in range(D_FF // FF_CHUNK):
        ff = slice(c * FF_CHUNK, (c + 1) * FF_CHUNK)
        hid = jnp.maximum(jnp.dot(xn, w1_ref[:, ff], preferred_element_type=F32), 0.0)
        acc = acc + jnp.dot((hid * hid).astype(BF16), w2_ref[ff, :], preferred_element_type=F32)
    o_ref[...] = acc


def _mlp(h, g, w1, w2):
    tokens = h.shape[0]
    tile = pl.BlockSpec((MLP_TILE, D_MODEL), lambda i: (i, 0))
    return pl.pallas_call(
        _mlp_kernel,
        grid=(tokens // MLP_TILE,),
        in_specs=[tile, _resident((1, D_MODEL)), _resident((D_MODEL, D_FF)),
                  _resident((D_FF, D_MODEL))],
        out_specs=tile,
        out_shape=jax.ShapeDtypeStruct(h.shape, F32),
        compiler_params=pltpu.CompilerParams(
            dimension_semantics=("arbitrary",), vmem_limit_bytes=VMEM_LIMIT),
        name="mlp",
    )(h, g, w1, w2)


def _band_bias(rel_bias):
    q_idx = jnp.arange(CHUNK)[:, None] + LEFT_CHUNKS * CHUNK
    k_idx = jnp.arange(BAND_KEYS)[None, :]
    dist = jnp.clip(q_idx - k_idx, -(CHUNK - 1), REL_CLIP) + (CHUNK - 1)
    return rel_bias[:, dist].astype(F32)


def kernel(x, norm_mix_g, w_in, conv_w, conv_b, q_norm_g, k_norm_g, rel_bias, conv_out_g,
           attn_out_g, w_out, norm_mlp_g, w_mlp_in, w_mlp_out):
    bsz, seq, d = x.shape
    depth = w_in.shape[0]
    grp = jnp.arange(CONV_WIDTH) // GROUP
    gmat = jnp.where(grp[:, None] == grp[None, :], 1.0 / GROUP, 0.0).astype(BF16)
    for l in range(depth):
        yc, q, k, v = _mix_proj(
            x, norm_mix_g[l][None], w_in[l].astype(BF16), conv_w[l], conv_b[l][None],
            jnp.tile(q_norm_g[l], N_HEADS)[None], jnp.tile(k_norm_g[l], N_HEADS)[None],
            conv_out_g[l][None], gmat)
        h = _attn_out(x, yc, q, k, v, _band_bias(rel_bias[l]), attn_out_g[l][None], gmat,
                      w_out[l].astype(BF16))
        x = _mlp(h.reshape(bsz * seq, d), norm_mlp_g[l][None], w_mlp_in[l].astype(BF16),
                 w_mlp_out[l].astype(BF16)).reshape(bsz, seq, d)
    return x
```

```python
import jax
import jax.numpy as jnp
from jax import lax
from jax.experimental import pallas as pl
from jax.experimental.pallas import tpu as pltpu

D_MODEL = 1024
CHUNK = 64
LEFT_CHUNKS = 8
BAND = LEFT_CHUNKS + 1
CONV_WIDTH = D_MODEL // 2
GROUP = 64
CONV_K = 3
HEAD_DIM = 64
N_HEADS = (D_MODEL - CONV_WIDTH) // HEAD_DIM
ATTN_WIDTH = N_HEADS * HEAD_DIM
REL_CLIP = 128
REL_TABLE = (CHUNK - 1) + REL_CLIP + 1
D_FF = 4 * D_MODEL
EPS = 1e-6
NEG_INF = -1e30

LANES = 128
SUBLANES = 8
SEQ_TILE = LEFT_CHUNKS * CHUNK
Q_BLOCK = 2 * CHUNK
Q_BLOCKS = SEQ_TILE // Q_BLOCK
KEY_SPAN = (BAND + 1) * CHUNK
HEAD_PAIRS = N_HEADS * HEAD_DIM // LANES
PAIR_COLS = 2 * Q_BLOCK
BIAS_FLAT_ROWS = KEY_SPAN - 2 * Q_BLOCK
BIAS_SLAB = 32
MLP_TILE = 512
FF_CHUNK = 1024
VMEM_LIMIT = 48 * 1024 * 1024

F32 = jnp.float32
BF16 = jnp.bfloat16
NT_DIMS = (((1,), (1,)), ((), ()))
TN_DIMS = (((0,), (0,)), ((), ()))


def _resident(shape):
    return pl.BlockSpec(shape, lambda *_: (0,) * len(shape), pipeline_mode=pl.Buffered(1))


def _rms_norm_bf16(x, gain):
    ms = jnp.mean(x * x, axis=-1, keepdims=True)
    return (x * lax.rsqrt(ms + EPS) * gain).astype(BF16)


def _group_rsqrt(y, gmat, channels_first):
    sq = (y * y).astype(BF16)
    if channels_first:
        ms = jnp.dot(gmat, sq, preferred_element_type=F32)
    else:
        ms = jnp.dot(sq, gmat, preferred_element_type=F32)
    return lax.rsqrt(ms + EPS)


def _bias_table_kernel(rel_ref, out_ref):
    def col_fields(shape):
        c = lax.broadcasted_iota(jnp.int32, shape, 1)
        return c < Q_BLOCK, (c // CHUNK) % 2, c % Q_BLOCK

    def in_band(r, j):
        return (r >= j * CHUNK) & (r < j * CHUNK + BAND * CHUNK)

    first, j, _ = col_fields((BIAS_FLAT_ROWS, PAIR_COLS))
    r = lax.broadcasted_iota(jnp.int32, (BIAS_FLAT_ROWS, PAIR_COLS), 0)
    for p in range(HEAD_PAIRS):
        far = jnp.where(first, rel_ref[2 * p, REL_TABLE - 1], rel_ref[2 * p + 1, REL_TABLE - 1])
        out_ref[p, 0:BIAS_FLAT_ROWS, :] = jnp.where(in_band(r, j), far, NEG_INF)

    first, j, u = col_fields((BIAS_SLAB, PAIR_COLS))
    for r0 in range(BIAS_FLAT_ROWS, KEY_SPAN, BIAS_SLAB):
        r = lax.broadcasted_iota(jnp.int32, (BIAS_SLAB, PAIR_COLS), 0) + r0
        dist = LEFT_CHUNKS * CHUNK + u - r
        idx = jnp.clip(dist, -(CHUNK - 1), REL_CLIP) + (CHUNK - 1)
        lo = min(max(LEFT_CHUNKS * CHUNK - (r0 + BIAS_SLAB - 1), -(CHUNK - 1)), REL_CLIP) + CHUNK - 1
        hi = min(max(LEFT_CHUNKS * CHUNK + Q_BLOCK - 1 - r0, -(CHUNK - 1)), REL_CLIP) + CHUNK - 1

        def pick(t, accs):
            hit = idx == t
            return tuple(
                jnp.where(hit, jnp.where(first, rel_ref[2 * p, t], rel_ref[2 * p + 1, t]), acc)
                for p, acc in enumerate(accs))

        zero = jnp.zeros((BIAS_SLAB, PAIR_COLS), F32)
        accs = lax.fori_loop(lo, hi + 1, pick, (zero,) * HEAD_PAIRS)
        for p in range(HEAD_PAIRS):
            out_ref[p, r0:r0 + BIAS_SLAB, :] = jnp.where(in_band(r, j), accs[p], NEG_INF)


def _bias_table(rel_bias):
    return pl.pallas_call(
        _bias_table_kernel,
        in_specs=[pl.BlockSpec(memory_space=pltpu.SMEM)],
        out_specs=pl.BlockSpec(memory_space=pltpu.VMEM),
        out_shape=jax.ShapeDtypeStruct((HEAD_PAIRS, KEY_SPAN, PAIR_COLS), F32),
        name="rel_bias_table",
    )(rel_bias)


def _mix_proj_kernel(x_ref, g_ref, wc_ref, wk_ref, wqt_ref, wvt_ref, cw_ref, cb_ref, qg_ref,
                     kg_ref, cog_ref, gm_ref, yc_ref, qt_ref, k_ref, vt_ref, carry_ref):
    i = pl.program_id(1)

    @pl.when(i == 0)
    def _():
        carry_ref[...] = jnp.zeros_like(carry_ref)

    xn = _rms_norm_bf16(x_ref[...], g_ref[...])
    gmat = gm_ref[...]

    def proj(j):
        return jnp.dot(xn, wc_ref[:, j * CONV_WIDTH:(j + 1) * CONV_WIDTH],
                       preferred_element_type=F32)

    u = proj(1) * proj(2)
    rows = lax.broadcasted_iota(jnp.int32, u.shape, 0)
    prev2 = carry_ref[SUBLANES - 2:SUBLANES - 1, :]
    prev1 = carry_ref[SUBLANES - 1:SUBLANES, :]
    u1 = jnp.where(rows == 0, prev1, pltpu.roll(u, 1, 0))
    u2 = jnp.where(rows == 0, prev2, jnp.where(rows == 1, prev1, pltpu.roll(u, 2, 0)))
    carry_ref[...] = u[SEQ_TILE - SUBLANES:, :]
    y = cw_ref[0:1, :] * u2 + cw_ref[1:2, :] * u1 + cw_ref[2:3, :] * u
    yc = proj(0) * (y + cb_ref[...])
    yc_ref[...] = (yc * _group_rsqrt(yc, gmat, False) * cog_ref[...]).astype(BF16)

    k = jnp.dot(xn, wk_ref[...], preferred_element_type=F32)
    k_ref[...] = (k * _group_rsqrt(k, gmat, False) * kg_ref[...]).astype(BF16)
    qt = lax.dot_general(wqt_ref[...], xn, NT_DIMS, preferred_element_type=F32)
    qt_ref[...] = (qt * _group_rsqrt(qt, gmat, True) * qg_ref[...] * (HEAD_DIM ** -0.5)).astype(BF16)
    vt_ref[...] = lax.dot_general(wvt_ref[...], xn, NT_DIMS, preferred_element_type=F32).astype(BF16)


def _mix_proj(x, g, w_conv, w_k, w_qt, w_vt, conv_w, conv_b, qg_col, kg, cog, gmat):
    bsz, seq, _ = x.shape
    tile = pl.BlockSpec((None, SEQ_TILE, D_MODEL), lambda b, i: (b, i, 0))
    tok_major = pl.BlockSpec((None, SEQ_TILE, CONV_WIDTH), lambda b, i: (b, i, 0))
    ch_major = pl.BlockSpec((None, ATTN_WIDTH, SEQ_TILE), lambda b, i: (b, 0, i))
    tok_sds = jax.ShapeDtypeStruct((bsz, seq, CONV_WIDTH), BF16)
    ch_sds = jax.ShapeDtypeStruct((bsz, ATTN_WIDTH, seq), BF16)
    return pl.pallas_call(
        _mix_proj_kernel,
        grid=(bsz, seq // SEQ_TILE),
        in_specs=[tile, _resident((1, D_MODEL)), _resident((D_MODEL, 3 * CONV_WIDTH)),
                  _resident((D_MODEL, ATTN_WIDTH)), _resident((ATTN_WIDTH, D_MODEL)),
                  _resident((ATTN_WIDTH, D_MODEL)),
                  _resident((CONV_K, CONV_WIDTH)), _resident((1, CONV_WIDTH)),
                  _resident((ATTN_WIDTH, 1)), _resident((1, ATTN_WIDTH)),
                  _resident((1, CONV_WIDTH)), _resident((CONV_WIDTH, CONV_WIDTH))],
        out_specs=[tok_major, ch_major, tok_major, ch_major],
        out_shape=[tok_sds, ch_sds, tok_sds, ch_sds],
        scratch_shapes=[pltpu.VMEM((SUBLANES, CONV_WIDTH), F32)],
        compiler_params=pltpu.CompilerParams(
            dimension_semantics=("arbitrary", "arbitrary"), vmem_limit_bytes=VMEM_LIMIT),
        name="mix_proj",
    )(x, g, w_conv, w_k, w_qt, w_vt, conv_w, conv_b, qg_col, kg, cog, gmat)


def _attn_out_kernel(x_ref, yc_ref, qt_ref, kp_ref, kc_ref, vtp_ref, vtc_ref, bias_ref,
                     aog_ref, gm_ref, wo_ref, h_ref, kw_ref, vtw_ref, yt_ref):
    i = pl.program_id(1)
    kw_ref[0:SEQ_TILE, :] = kp_ref[...]
    kw_ref[SEQ_TILE:, :] = kc_ref[...]
    vtw_ref[:, 0:SEQ_TILE] = vtp_ref[...]
    vtw_ref[:, SEQ_TILE:] = vtc_ref[...]

    key_row = lax.broadcasted_iota(jnp.int32, (KEY_SPAN, PAIR_COLS), 0)
    first_head = lax.broadcasted_iota(jnp.int32, (LANES, Q_BLOCK), 0) < HEAD_DIM

    for qb in range(Q_BLOCKS):
        k0 = qb * Q_BLOCK
        valid = key_row >= jnp.where(i == 0, SEQ_TILE - k0, 0)
        for p in range(HEAD_PAIRS):
            ch = slice(p * LANES, (p + 1) * LANES)
            qt = qt_ref[ch, k0:k0 + Q_BLOCK]
            zero = jnp.zeros_like(qt)
            rhs = jnp.concatenate([jnp.where(first_head, qt, zero),
                                   jnp.where(first_head, zero, qt)], axis=1)
            s = jnp.dot(kw_ref[k0:k0 + KEY_SPAN, ch], rhs, preferred_element_type=F32)
            s = jnp.where(valid, s + bias_ref[p], NEG_INF)
            m = jnp.max(s, axis=0, keepdims=True)
            pr = jnp.exp(s - m)
            denom = jnp.sum(pr, axis=0, keepdims=True)
            ot = jnp.dot(vtw_ref[ch, k0:k0 + KEY_SPAN], pr.astype(BF16),
                         preferred_element_type=F32) / denom
            yt_ref[p * LANES:p * LANES + HEAD_DIM, k0:k0 + Q_BLOCK] = ot[0:HEAD_DIM, 0:Q_BLOCK]
            yt_ref[p * LANES + HEAD_DIM:(p + 1) * LANES, k0:k0 + Q_BLOCK] = ot[HEAD_DIM:, Q_BLOCK:]

    yt = yt_ref[...]
    ynt = (yt * _group_rsqrt(yt, gm_ref[...], True) * aog_ref[...]).astype(BF16)
    mixed = jnp.dot(yc_ref[...], wo_ref[0:CONV_WIDTH, :], preferred_element_type=F32)
    mixed = mixed + lax.dot_general(ynt, wo_ref[CONV_WIDTH:, :], TN_DIMS,
                                    preferred_element_type=F32)
    h_ref[...] = x_ref[...] + mixed


def _attn_out(x, yc, qt, k, vt, bias, aog_col, gmat, w_out):
    bsz, seq, _ = x.shape
    x_tile = pl.BlockSpec((None, SEQ_TILE, D_MODEL), lambda b, i: (b, i, 0))
    tok_cur = pl.BlockSpec((None, SEQ_TILE, ATTN_WIDTH), lambda b, i: (b, i, 0))
    tok_prev = pl.BlockSpec((None, SEQ_TILE, ATTN_WIDTH),
                            lambda b, i: (b, jnp.maximum(i - 1, 0), 0))
    ch_cur = pl.BlockSpec((None, ATTN_WIDTH, SEQ_TILE), lambda b, i: (b, 0, i))
    ch_prev = pl.BlockSpec((None, ATTN_WIDTH, SEQ_TILE),
                           lambda b, i: (b, 0, jnp.maximum(i - 1, 0)))
    return pl.pallas_call(
        _attn_out_kernel,
        grid=(bsz, seq // SEQ_TILE),
        in_specs=[x_tile, tok_cur, ch_cur, tok_prev, tok_cur, ch_prev, ch_cur,
                  _resident((HEAD_PAIRS, KEY_SPAN, PAIR_COLS)), _resident((ATTN_WIDTH, 1)),
                  _resident((ATTN_WIDTH, ATTN_WIDTH)), _resident((D_MODEL, D_MODEL))],
        out_specs=x_tile,
        out_shape=jax.ShapeDtypeStruct(x.shape, F32),
        scratch_shapes=[pltpu.VMEM((2 * SEQ_TILE, ATTN_WIDTH), BF16),
                        pltpu.VMEM((ATTN_WIDTH, 2 * SEQ_TILE), BF16),
                        pltpu.VMEM((ATTN_WIDTH, SEQ_TILE), F32)],
        compiler_params=pltpu.CompilerParams(
            dimension_semantics=("arbitrary", "arbitrary"), vmem_limit_bytes=VMEM_LIMIT),
        name="attn_out",
    )(x, yc, qt, k, k, vt, vt, bias, aog_col, gmat, w_out)


def _mlp_kernel(h_ref, g_ref, w1_ref, w2_ref, o_ref):
    h = h_ref[...]
    xn = _rms_norm_bf16(h, g_ref[...])
    acc = h
    for c in range(D_FF // FF_CHUNK):
        ff = slice(c * FF_CHUNK, (c + 1) * FF_CHUNK)
        hid = jnp.maximum(jnp.dot(xn, w1_ref[:, ff], preferred_element_type=F32), 0.0)
        acc = acc + jnp.dot((hid * hid).astype(BF16), w2_ref[ff, :], preferred_element_type=F32)
    o_ref[...] = acc


def _mlp(h, g, w1, w2):
    tokens = h.shape[0]
    tile = pl.BlockSpec((MLP_TILE, D_MODEL), lambda i: (i, 0))
    return pl.pallas_call(
        _mlp_kernel,
        grid=(tokens // MLP_TILE,),
        in_specs=[tile, _resident((1, D_MODEL)), _resident((D_MODEL, D_FF)),
                  _resident((D_FF, D_MODEL))],
        out_specs=tile,
        out_shape=jax.ShapeDtypeStruct(h.shape, F32),
        compiler_params=pltpu.CompilerParams(
            dimension_semantics=("arbitrary",), vmem_limit_bytes=VMEM_LIMIT),
        name="mlp",
    )(h, g, w1, w2)


def kernel(x, norm_mix_g, w_in, conv_w, conv_b, q_norm_g, k_norm_g, rel_bias, conv_out_g,
           attn_out_g, w_out, norm_mlp_g, w_mlp_in, w_mlp_out):
    bsz, seq, d = x.shape
    depth = w_in.shape[0]
    grp = jnp.arange(CONV_WIDTH) // GROUP
    gmat = jnp.where(grp[:, None] == grp[None, :], 1.0 / GROUP, 0.0).astype(BF16)
    q0 = 3 * CONV_WIDTH
    for l in range(depth):
        w = w_in[l].astype(BF16)
        yc, qt, k, vt = _mix_proj(
            x, norm_mix_g[l][None], w[:, :q0], w[:, q0 + ATTN_WIDTH:q0 + 2 * ATTN_WIDTH],
            w[:, q0:q0 + ATTN_WIDTH].T, w[:, q0 + 2 * ATTN_WIDTH:].T,
            conv_w[l], conv_b[l][None], jnp.tile(q_norm_g[l], N_HEADS)[:, None],
            jnp.tile(k_norm_g[l], N_HEADS)[None], conv_out_g[l][None], gmat)
        h = _attn_out(x, yc, qt, k, vt, _bias_table(rel_bias[l]), attn_out_g[l][:, None], gmat,
                      w_out[l].astype(BF16))
        x = _mlp(h.reshape(bsz * seq, d), norm_mlp_g[l][None], w_mlp_in[l].astype(BF16),
                 w_mlp_out[l].astype(BF16)).reshape(bsz, seq, d)
    return x
```

```python
import jax
import jax.numpy as jnp
from jax import lax
from jax.experimental import pallas as pl
from jax.experimental.pallas import tpu as pltpu

D_MODEL = 1024
CHUNK = 64
LEFT_CHUNKS = 8
BAND = LEFT_CHUNKS + 1
CONV_WIDTH = D_MODEL // 2
GROUP = 64
CONV_K = 3
HEAD_DIM = 64
N_HEADS = (D_MODEL - CONV_WIDTH) // HEAD_DIM
ATTN_WIDTH = N_HEADS * HEAD_DIM
REL_CLIP = 128
REL_TABLE = (CHUNK - 1) + REL_CLIP + 1
D_FF = 4 * D_MODEL
EPS = 1e-6
NEG_INF = -1e30

LANES = 128
SUBLANES = 8
SEQ_TILE = LEFT_CHUNKS * CHUNK
Q_BLOCK = 2 * CHUNK
Q_BLOCKS = SEQ_TILE // Q_BLOCK
KEY_SPAN = (BAND + 1) * CHUNK
HEAD_PAIRS = N_HEADS * HEAD_DIM // LANES
PAIR_COLS = 2 * Q_BLOCK
BIAS_FLAT_ROWS = KEY_SPAN - 2 * Q_BLOCK
BIAS_SLAB = 32
MLP_TILE = 512
FF_CHUNK = 1024
VMEM_LIMIT = 48 * 1024 * 1024

F32 = jnp.float32
BF16 = jnp.bfloat16
NT_DIMS = (((1,), (1,)), ((), ()))
TN_DIMS = (((0,), (0,)), ((), ()))


def _resident(shape):
    return pl.BlockSpec(shape, lambda *_: (0,) * len(shape), pipeline_mode=pl.Buffered(1))


def _rms_norm_bf16(x, gain):
    ms = jnp.mean(x * x, axis=-1, keepdims=True)
    return (x * lax.rsqrt(ms + EPS) * gain).astype(BF16)


def _group_rsqrt(y, gmat, channels_first):
    sq = (y * y).astype(BF16)
    if channels_first:
        ms = jnp.dot(gmat, sq, preferred_element_type=F32)
    else:
        ms = jnp.dot(sq, gmat, preferred_element_type=F32)
    return lax.rsqrt(ms + EPS)


def _bias_table_kernel(rel_ref, out_ref):
    def col_fields(shape):
        c = lax.broadcasted_iota(jnp.int32, shape, 1)
        return c < Q_BLOCK, (c // CHUNK) % 2, c % Q_BLOCK

    def in_band(r, j):
        return (r >= j * CHUNK) & (r < j * CHUNK + BAND * CHUNK)

    first, j, _ = col_fields((BIAS_FLAT_ROWS, PAIR_COLS))
    r = lax.broadcasted_iota(jnp.int32, (BIAS_FLAT_ROWS, PAIR_COLS), 0)
    for p in range(HEAD_PAIRS):
        far = jnp.where(first, rel_ref[2 * p, REL_TABLE - 1], rel_ref[2 * p + 1, REL_TABLE - 1])
        out_ref[p, 0:BIAS_FLAT_ROWS, :] = jnp.where(in_band(r, j), far, NEG_INF)

    first, j, u = col_fields((BIAS_SLAB, PAIR_COLS))
    for r0 in range(BIAS_FLAT_ROWS, KEY_SPAN, BIAS_SLAB):
        r = lax.broadcasted_iota(jnp.int32, (BIAS_SLAB, PAIR_COLS), 0) + r0
        dist = LEFT_CHUNKS * CHUNK + u - r
        idx = jnp.clip(dist, -(CHUNK - 1), REL_CLIP) + (CHUNK - 1)
        lo = min(max(LEFT_CHUNKS * CHUNK - (r0 + BIAS_SLAB - 1), -(CHUNK - 1)), REL_CLIP) + CHUNK - 1
        hi = min(max(LEFT_CHUNKS * CHUNK + Q_BLOCK - 1 - r0, -(CHUNK - 1)), REL_CLIP) + CHUNK - 1

        def pick(t, accs):
            hit = idx == t
            return tuple(
                jnp.where(hit, jnp.where(first, rel_ref[2 * p, t], rel_ref[2 * p + 1, t]), acc)
                for p, acc in enumerate(accs))

        zero = jnp.zeros((BIAS_SLAB, PAIR_COLS), F32)
        accs = lax.fori_loop(lo, hi + 1, pick, (zero,) * HEAD_PAIRS)
        for p in range(HEAD_PAIRS):
            out_ref[p, r0:r0 + BIAS_SLAB, :] = jnp.where(in_band(r, j), accs[p], NEG_INF)


def _bias_table(rel_bias):
    return pl.pallas_call(
        _bias_table_kernel,
        in_specs=[pl.BlockSpec(memory_space=pltpu.SMEM)],
        out_specs=pl.BlockSpec(memory_space=pltpu.VMEM),
        out_shape=jax.ShapeDtypeStruct((HEAD_PAIRS, KEY_SPAN, PAIR_COLS), F32),
        name="rel_bias_table",
    )(rel_bias)


def _mix_proj_kernel(x_ref, g_ref, wc_ref, wk_ref, wqt_ref, wvt_ref, cw_ref, cb_ref, qg_ref,
                     kg_ref, cog_ref, gm_ref, yc_ref, qt_ref, k_ref, vt_ref, carry_ref):
    i = pl.program_id(1)

    @pl.when(i == 0)
    def _():
        carry_ref[...] = jnp.zeros_like(carry_ref)

    xn = _rms_norm_bf16(x_ref[...], g_ref[...])
    gmat = gm_ref[...]

    def proj(j):
        return jnp.dot(xn, wc_ref[:, j * CONV_WIDTH:(j + 1) * CONV_WIDTH],
                       preferred_element_type=F32)

    u = proj(1) * proj(2)
    rows = lax.broadcasted_iota(jnp.int32, u.shape, 0)
    prev2 = carry_ref[SUBLANES - 2:SUBLANES - 1, :]
    prev1 = carry_ref[SUBLANES - 1:SUBLANES, :]
    u1 = jnp.where(rows == 0, prev1, pltpu.roll(u, 1, 0))
    u2 = jnp.where(rows == 0, prev2, jnp.where(rows == 1, prev1, pltpu.roll(u, 2, 0)))
    carry_ref[...] = u[SEQ_TILE - SUBLANES:, :]
    y = cw_ref[0:1, :] * u2 + cw_ref[1:2, :] * u1 + cw_ref[2:3, :] * u
    yc = proj(0) * (y + cb_ref[...])
    yc_ref[...] = (yc * _group_rsqrt(yc, gmat, False) * cog_ref[...]).astype(BF16)

    k = jnp.dot(xn, wk_ref[...], preferred_element_type=F32)
    k_ref[...] = (k * _group_rsqrt(k, gmat, False) * kg_ref[...]).astype(BF16)
    qt = lax.dot_general(wqt_ref[...], xn, NT_DIMS, preferred_element_type=F32)
    qt_ref[...] = (qt * _group_rsqrt(qt, gmat, True) * qg_ref[...] * (HEAD_DIM ** -0.5)).astype(BF16)
    vt_ref[...] = lax.dot_general(wvt_ref[...], xn, NT_DIMS, preferred_element_type=F32).astype(BF16)


def _mix_proj(x, g, w_conv, w_k, w_qt, w_vt, conv_w, conv_b, qg_col, kg, cog, gmat):
    bsz, seq, _ = x.shape
    tile = pl.BlockSpec((None, SEQ_TILE, D_MODEL), lambda b, i: (b, i, 0))
    tok_major = pl.BlockSpec((None, SEQ_TILE, CONV_WIDTH), lambda b, i: (b, i, 0))
    ch_major = pl.BlockSpec((None, ATTN_WIDTH, SEQ_TILE), lambda b, i: (b, 0, i))
    tok_sds = jax.ShapeDtypeStruct((bsz, seq, CONV_WIDTH), BF16)
    ch_sds = jax.ShapeDtypeStruct((bsz, ATTN_WIDTH, seq), BF16)
    return pl.pallas_call(
        _mix_proj_kernel,
        grid=(bsz, seq // SEQ_TILE),
        in_specs=[tile, _resident((1, D_MODEL)), _resident((D_MODEL, 3 * CONV_WIDTH)),
                  _resident((D_MODEL, ATTN_WIDTH)), _resident((ATTN_WIDTH, D_MODEL)),
                  _resident((ATTN_WIDTH, D_MODEL)),
                  _resident((CONV_K, CONV_WIDTH)), _resident((1, CONV_WIDTH)),
                  _resident((ATTN_WIDTH, 1)), _resident((1, ATTN_WIDTH)),
                  _resident((1, CONV_WIDTH)), _resident((CONV_WIDTH, CONV_WIDTH))],
        out_specs=[tok_major, ch_major, tok_major, ch_major],
        out_shape=[tok_sds, ch_sds, tok_sds, ch_sds],
        scratch_shapes=[pltpu.VMEM((SUBLANES, CONV_WIDTH), F32)],
        compiler_params=pltpu.CompilerParams(
            dimension_semantics=("arbitrary", "arbitrary"), vmem_limit_bytes=VMEM_LIMIT),
        name="mix_proj",
    )(x, g, w_conv, w_k, w_qt, w_vt, conv_w, conv_b, qg_col, kg, cog, gmat)


def _attn_out_kernel(x_ref, yc_ref, qt_ref, kp_ref, kc_ref, vtp_ref, vtc_ref, bias_ref,
                     aog_ref, gm_ref, wo_ref, h_ref, kw_ref, vtw_ref, yt_ref, s_ref):
    i = pl.program_id(1)
    kw_ref[0:SEQ_TILE, :] = kp_ref[...]
    kw_ref[SEQ_TILE:, :] = kc_ref[...]
    vtw_ref[:, 0:SEQ_TILE] = vtp_ref[...]
    vtw_ref[:, SEQ_TILE:] = vtc_ref[...]

    key_row = lax.broadcasted_iota(jnp.int32, (KEY_SPAN, PAIR_COLS), 0)
    first_head = lax.broadcasted_iota(jnp.int32, (LANES, Q_BLOCK), 0) < HEAD_DIM

    blocks = [(qb * Q_BLOCK, p) for qb in range(Q_BLOCKS) for p in range(HEAD_PAIRS)]

    def scores(n):
        k0, p = blocks[n]
        ch = slice(p * LANES, (p + 1) * LANES)
        qt = qt_ref[ch, k0:k0 + Q_BLOCK]
        zero = jnp.zeros_like(qt)
        rhs = jnp.concatenate([jnp.where(first_head, qt, zero),
                               jnp.where(first_head, zero, qt)], axis=1)
        s = jnp.dot(kw_ref[k0:k0 + KEY_SPAN, ch], rhs, preferred_element_type=F32)
        valid = key_row >= jnp.where(i == 0, SEQ_TILE - k0, 0)
        s_ref[n % 2] = jnp.where(valid, s + bias_ref[p], NEG_INF)

    def attend(n):
        k0, p = blocks[n]
        ch = slice(p * LANES, (p + 1) * LANES)
        s = s_ref[n % 2]
        m = jnp.max(s, axis=0, keepdims=True)
        pr = jnp.exp(s - m)
        denom = jnp.sum(pr, axis=0, keepdims=True)
        ot = jnp.dot(vtw_ref[ch, k0:k0 + KEY_SPAN], pr.astype(BF16),
                     preferred_element_type=F32) / denom
        yt_ref[p * LANES:p * LANES + HEAD_DIM, k0:k0 + Q_BLOCK] = ot[0:HEAD_DIM, 0:Q_BLOCK]
        yt_ref[p * LANES + HEAD_DIM:(p + 1) * LANES, k0:k0 + Q_BLOCK] = ot[HEAD_DIM:, Q_BLOCK:]

    scores(0)
    for n in range(len(blocks)):
        if n + 1 < len(blocks):
            scores(n + 1)
        attend(n)

    yt = yt_ref[...]
    ynt = (yt * _group_rsqrt(yt, gm_ref[...], True) * aog_ref[...]).astype(BF16)
    mixed = jnp.dot(yc_ref[...], wo_ref[0:CONV_WIDTH, :], preferred_element_type=F32)
    mixed = mixed + lax.dot_general(ynt, wo_ref[CONV_WIDTH:, :], TN_DIMS,
                                    preferred_element_type=F32)
    h_ref[...] = x_ref[...] + mixed


def _attn_out(x, yc, qt, k, vt, bias, aog_col, gmat, w_out):
    bsz, seq, _ = x.shape
    x_tile = pl.BlockSpec((None, SEQ_TILE, D_MODEL), lambda b, i: (b, i, 0))
    tok_cur = pl.BlockSpec((None, SEQ_TILE, ATTN_WIDTH), lambda b, i: (b, i, 0))
    tok_prev = pl.BlockSpec((None, SEQ_TILE, ATTN_WIDTH),
                            lambda b, i: (b, jnp.maximum(i - 1, 0), 0))
    ch_cur = pl.BlockSpec((None, ATTN_WIDTH, SEQ_TILE), lambda b, i: (b, 0, i))
    ch_prev = pl.BlockSpec((None, ATTN_WIDTH, SEQ_TILE),
                           lambda b, i: (b, 0, jnp.maximum(i - 1, 0)))
    return pl.pallas_call(
        _attn_out_kernel,
        grid=(bsz, seq // SEQ_TILE),
        in_specs=[x_tile, tok_cur, ch_cur, tok_prev, tok_cur, ch_prev, ch_cur,
                  _resident((HEAD_PAIRS, KEY_SPAN, PAIR_COLS)), _resident((ATTN_WIDTH, 1)),
                  _resident((ATTN_WIDTH, ATTN_WIDTH)), _resident((D_MODEL, D_MODEL))],
        out_specs=x_tile,
        out_shape=jax.ShapeDtypeStruct(x.shape, F32),
        scratch_shapes=[pltpu.VMEM((2 * SEQ_TILE, ATTN_WIDTH), BF16),
                        pltpu.VMEM((ATTN_WIDTH, 2 * SEQ_TILE), BF16),
                        pltpu.VMEM((ATTN_WIDTH, SEQ_TILE), F32),
                        pltpu.VMEM((2, KEY_SPAN, PAIR_COLS), F32)],
        compiler_params=pltpu.CompilerParams(
            dimension_semantics=("arbitrary", "arbitrary"), vmem_limit_bytes=VMEM_LIMIT),
        name="attn_out",
    )(x, yc, qt, k, k, vt, vt, bias, aog_col, gmat, w_out)


def _mlp_kernel(h_ref, g_ref, w1_ref, w2_ref, o_ref):
    h = h_ref[...]
    xn = _rms_norm_bf16(h, g_ref[...])
    acc = h
    for c in range(D_FF // FF_CHUNK):
        ff = slice(c * FF_CHUNK, (c + 1) * FF_CHUNK)
        hid = jnp.maximum(jnp.dot(xn, w1_ref[:, ff], preferred_element_type=F32), 0.0)
        acc = acc + jnp.dot((hid * hid).astype(BF16), w2_ref[ff, :], preferred_element_type=F32)
    o_ref[...] = acc


def _mlp(h, g, w1, w2):
    tokens = h.shape[0]
    tile = pl.BlockSpec((MLP_TILE, D_MODEL), lambda i: (i, 0))
    return pl.pallas_call(
        _mlp_kernel,
        grid=(tokens // MLP_TILE,),
        in_specs=[tile, _resident((1, D_MODEL)), _resident((D_MODEL, D_FF)),
                  _resident((D_FF, D_MODEL))],
        out_specs=tile,
        out_shape=jax.ShapeDtypeStruct(h.shape, F32),
        compiler_params=pltpu.CompilerParams(
            dimension_semantics=("arbitrary",), vmem_limit_bytes=VMEM_LIMIT),
        name="mlp",
    )(h, g, w1, w2)


def kernel(x, norm_mix_g, w_in, conv_w, conv_b, q_norm_g, k_norm_g, rel_bias, conv_out_g,
           attn_out_g, w_out, norm_mlp_g, w_mlp_in, w_mlp_out):
    bsz, seq, d = x.shape
    depth = w_in.shape[0]
    grp = jnp.arange(CONV_WIDTH) // GROUP
    gmat = jnp.where(grp[:, None] == grp[None, :], 1.0 / GROUP, 0.0).astype(BF16)
    q0 = 3 * CONV_WIDTH
    for l in range(depth):
        w = w_in[l].astype(BF16)
        yc, qt, k, vt = _mix_proj(
            x, norm_mix_g[l][None], w[:, :q0], w[:, q0 + ATTN_WIDTH:q0 + 2 * ATTN_WIDTH],
            w[:, q0:q0 + ATTN_WIDTH].T, w[:, q0 + 2 * ATTN_WIDTH:].T,
            conv_w[l], conv_b[l][None], jnp.tile(q_norm_g[l], N_HEADS)[:, None],
            jnp.tile(k_norm_g[l], N_HEADS)[None], conv_out_g[l][None], gmat)
        h = _attn_out(x, yc, qt, k, vt, _bias_table(rel_bias[l]), attn_out_g[l][:, None], gmat,
                      w_out[l].astype(BF16))
        x = _mlp(h.reshape(bsz * seq, d), norm_mlp_g[l][None], w_mlp_in[l].astype(BF16),
                 w_mlp_out[l].astype(BF16)).reshape(bsz, seq, d)
    return x
```

```python
import math

import jax
import jax.numpy as jnp
from jax import lax
from jax.experimental import pallas as pl
from jax.experimental.pallas import tpu as pltpu

D_MODEL = 1024
CHUNK = 64
LEFT_CHUNKS = 8
BAND = LEFT_CHUNKS + 1
CONV_WIDTH = D_MODEL // 2
GROUP = 64
CONV_K = 3
HEAD_DIM = 64
N_HEADS = (D_MODEL - CONV_WIDTH) // HEAD_DIM
ATTN_WIDTH = N_HEADS * HEAD_DIM
REL_CLIP = 128
REL_TABLE = (CHUNK - 1) + REL_CLIP + 1
D_FF = 4 * D_MODEL
EPS = 1e-6
NEG_INF = -1e30
LOG2_E = math.log2(math.e)

LANES = 128
SUBLANES = 8
BF16_ROWS = 16
SEQ_TILE = LEFT_CHUNKS * CHUNK
Q_BLOCK = 2 * CHUNK
Q_BLOCKS = SEQ_TILE // Q_BLOCK
KEY_SPAN = (BAND + 1) * CHUNK
HEAD_PAIRS = N_HEADS * HEAD_DIM // LANES
PAIR_COLS = 2 * Q_BLOCK
BIAS_FLAT_ROWS = KEY_SPAN - 2 * Q_BLOCK
BIAS_SLAB = 32
OUT_COL_CHUNK = D_MODEL // Q_BLOCKS
MLP_TILE = 512
FF_CHUNK = 1024
VMEM_LIMIT = 48 * 1024 * 1024

F32 = jnp.float32
BF16 = jnp.bfloat16
NT_DIMS = (((1,), (1,)), ((), ()))
TN_DIMS = (((0,), (0,)), ((), ()))


def _resident(shape):
    return pl.BlockSpec(shape, lambda *_: (0,) * len(shape), pipeline_mode=pl.Buffered(1))


def _rms_norm_bf16(x, gain):
    ms = jnp.mean(x * x, axis=-1, keepdims=True)
    return (x * lax.rsqrt(ms + EPS) * gain).astype(BF16)


def _group_norm_tokens(y, gmat):
    ms = jnp.dot((y * y).astype(BF16), gmat, preferred_element_type=F32)
    return y * lax.rsqrt(ms + EPS)


def _group_norm_channels(y):
    c, t = y.shape
    y3 = y.reshape(c // GROUP, GROUP, t)
    ms = jnp.mean(y3 * y3, axis=1, keepdims=True)
    return (y3 * lax.rsqrt(ms + EPS)).reshape(c, t)


def _bias_table_kernel(rel_ref, out_ref):
    def col_fields(shape):
        c = lax.broadcasted_iota(jnp.int32, shape, 1)
        return c < Q_BLOCK, (c // CHUNK) % 2, c % Q_BLOCK

    def in_band(r, j):
        return (r >= j * CHUNK) & (r < j * CHUNK + BAND * CHUNK)

    first, j, _ = col_fields((BIAS_FLAT_ROWS, PAIR_COLS))
    r = lax.broadcasted_iota(jnp.int32, (BIAS_FLAT_ROWS, PAIR_COLS), 0)
    for p in range(HEAD_PAIRS):
        far = jnp.where(first, rel_ref[2 * p, REL_TABLE - 1], rel_ref[2 * p + 1, REL_TABLE - 1])
        out_ref[p, 0:BIAS_FLAT_ROWS, :] = jnp.where(in_band(r, j), far * LOG2_E, NEG_INF)

    first, j, u = col_fields((BIAS_SLAB, PAIR_COLS))
    for r0 in range(BIAS_FLAT_ROWS, KEY_SPAN, BIAS_SLAB):
        r = lax.broadcasted_iota(jnp.int32, (BIAS_SLAB, PAIR_COLS), 0) + r0
        dist = LEFT_CHUNKS * CHUNK + u - r
        idx = jnp.clip(dist, -(CHUNK - 1), REL_CLIP) + (CHUNK - 1)
        lo = min(max(LEFT_CHUNKS * CHUNK - (r0 + BIAS_SLAB - 1), -(CHUNK - 1)), REL_CLIP) + CHUNK - 1
        hi = min(max(LEFT_CHUNKS * CHUNK + Q_BLOCK - 1 - r0, -(CHUNK - 1)), REL_CLIP) + CHUNK - 1

        def pick(t, accs):
            hit = idx == t
            return tuple(
                jnp.where(hit, jnp.where(first, rel_ref[2 * p, t], rel_ref[2 * p + 1, t]), acc)
                for p, acc in enumerate(accs))

        zero = jnp.zeros((BIAS_SLAB, PAIR_COLS), F32)
        accs = lax.fori_loop(lo, hi + 1, pick, (zero,) * HEAD_PAIRS)
        for p in range(HEAD_PAIRS):
            out_ref[p, r0:r0 + BIAS_SLAB, :] = jnp.where(in_band(r, j), accs[p] * LOG2_E, NEG_INF)


def _bias_table(rel_bias):
    return pl.pallas_call(
        _bias_table_kernel,
        in_specs=[pl.BlockSpec(memory_space=pltpu.SMEM)],
        out_specs=pl.BlockSpec(memory_space=pltpu.VMEM),
        out_shape=jax.ShapeDtypeStruct((HEAD_PAIRS, KEY_SPAN, PAIR_COLS), F32),
        name="rel_bias_table",
    )(rel_bias)


def _mix_proj_kernel(x_ref, g_ref, wc_ref, wk_ref, wqt_ref, wvt_ref, cw_ref, cb_ref, qg_ref,
                     kg_ref, cog_ref, gm_ref, yc_ref, qt_ref, k_ref, vt_ref, carry_ref):
    i = pl.program_id(1)

    @pl.when(i == 0)
    def _():
        carry_ref[...] = jnp.zeros_like(carry_ref)

    xn = _rms_norm_bf16(x_ref[...], g_ref[...])
    gmat = gm_ref[...]

    def proj(j):
        return jnp.dot(xn, wc_ref[:, j * CONV_WIDTH:(j + 1) * CONV_WIDTH],
                       preferred_element_type=F32)

    u = proj(1) * proj(2)
    rows = lax.broadcasted_iota(jnp.int32, u.shape, 0)
    prev2 = carry_ref[SUBLANES - 2:SUBLANES - 1, :]
    prev1 = carry_ref[SUBLANES - 1:SUBLANES, :]
    u1 = jnp.where(rows == 0, prev1, pltpu.roll(u, 1, 0))
    u2 = jnp.where(rows == 0, prev2, jnp.where(rows == 1, prev1, pltpu.roll(u, 2, 0)))
    carry_ref[...] = u[SEQ_TILE - SUBLANES:, :]
    y = cw_ref[0:1, :] * u2 + cw_ref[1:2, :] * u1 + cw_ref[2:3, :] * u
    yc = proj(0) * (y + cb_ref[...])
    yc_ref[...] = (_group_norm_tokens(yc, gmat) * cog_ref[...]).astype(BF16)

    k = jnp.dot(xn, wk_ref[...], preferred_element_type=F32)
    k_ref[...] = (_group_norm_tokens(k, gmat) * kg_ref[...]).astype(BF16)
    qt = lax.dot_general(wqt_ref[...], xn, NT_DIMS, preferred_element_type=F32)
    qt_ref[...] = (_group_norm_channels(qt) * qg_ref[...] * (HEAD_DIM ** -0.5 * LOG2_E)).astype(BF16)
    vt_ref[...] = lax.dot_general(wvt_ref[...], xn, NT_DIMS, preferred_element_type=F32).astype(BF16)


def _mix_proj(x, g, w_conv, w_k, w_qt, w_vt, conv_w, conv_b, qg_col, kg, cog, gmat):
    bsz, seq, _ = x.shape
    tile = pl.BlockSpec((None, SEQ_TILE, D_MODEL), lambda b, i: (b, i, 0))
    tok_major = pl.BlockSpec((None, SEQ_TILE, CONV_WIDTH), lambda b, i: (b, i, 0))
    ch_major = pl.BlockSpec((None, ATTN_WIDTH, SEQ_TILE), lambda b, i: (b, 0, i))
    tok_sds = jax.ShapeDtypeStruct((bsz, seq, CONV_WIDTH), BF16)
    ch_sds = jax.ShapeDtypeStruct((bsz, ATTN_WIDTH, seq), BF16)
    return pl.pallas_call(
        _mix_proj_kernel,
        grid=(bsz, seq // SEQ_TILE),
        in_specs=[tile, _resident((1, D_MODEL)), _resident((D_MODEL, 3 * CONV_WIDTH)),
                  _resident((D_MODEL, ATTN_WIDTH)), _resident((ATTN_WIDTH, D_MODEL)),
                  _resident((ATTN_WIDTH, D_MODEL)),
                  _resident((CONV_K, CONV_WIDTH)), _resident((1, CONV_WIDTH)),
                  _resident((ATTN_WIDTH, 1)), _resident((1, ATTN_WIDTH)),
                  _resident((1, CONV_WIDTH)), _resident((CONV_WIDTH, CONV_WIDTH))],
        out_specs=[tok_major, ch_major, tok_major, ch_major],
        out_shape=[tok_sds, ch_sds, tok_sds, ch_sds],
        scratch_shapes=[pltpu.VMEM((SUBLANES, CONV_WIDTH), F32)],
        compiler_params=pltpu.CompilerParams(
            dimension_semantics=("arbitrary", "arbitrary"), vmem_limit_bytes=VMEM_LIMIT),
        name="mix_proj",
    )(x, g, w_conv, w_k, w_qt, w_vt, conv_w, conv_b, qg_col, kg, cog, gmat)


def _attn_out_kernel(x_ref, yc_ref, qt_ref, kp_ref, kc_ref, vtp_ref, vtc_ref, bias_ref,
                     aog_ref, wo_ref, h_ref, kw_ref, vtw_ref, yt_ref, s_ref):
    i = pl.program_id(1)
    kw_ref[0:SEQ_TILE, :] = kp_ref[...]
    kw_ref[SEQ_TILE:, :] = kc_ref[...]
    vtw_ref[:, 0:SEQ_TILE] = vtp_ref[...]
    vtw_ref[:, SEQ_TILE:] = vtc_ref[...]

    first_head = lax.broadcasted_iota(jnp.int32, (LANES, Q_BLOCK), 0) < HEAD_DIM
    blocks = [(qb * Q_BLOCK, p) for qb in range(Q_BLOCKS) for p in range(HEAD_PAIRS)]

    def conv_half_out_proj(c):
        cols = slice(c * OUT_COL_CHUNK, (c + 1) * OUT_COL_CHUNK)
        h_ref[:, cols] = x_ref[:, cols] + jnp.dot(yc_ref[...], wo_ref[0:CONV_WIDTH, cols],
                                                  preferred_element_type=F32)

    def attention(first_tile):
        def span_lo(k0):
            return SEQ_TILE - k0 if first_tile else 0

        def scores(n):
            k0, p = blocks[n]
            lo = span_lo(k0)
            ch = slice(p * LANES, (p + 1) * LANES)
            qt = qt_ref[ch, k0:k0 + Q_BLOCK]
            zero = jnp.zeros_like(qt)
            rhs = jnp.concatenate([jnp.where(first_head, qt, zero),
                                   jnp.where(first_head, zero, qt)], axis=1)
            s = jnp.dot(kw_ref[k0 + lo:k0 + KEY_SPAN, ch], rhs, preferred_element_type=F32)
            s_ref[n % 2, lo:, :] = s + bias_ref[p, lo:, :]

        def attend(n):
            k0, p = blocks[n]
            lo = span_lo(k0)
            ch = slice(p * LANES, (p + 1) * LANES)
            s = s_ref[n % 2, lo:, :]
            pr = jnp.exp2(s - jnp.max(s, axis=0, keepdims=True)).astype(BF16)
            vt1 = jnp.concatenate([vtw_ref[ch, k0 + lo:k0 + KEY_SPAN],
                                   jnp.ones((BF16_ROWS, KEY_SPAN - lo), BF16)], axis=0)
            ot = jnp.dot(vt1, pr, preferred_element_type=F32)
            ot = ot[0:LANES] / ot[LANES:LANES + 1]
            yt_ref[p * LANES:p * LANES + HEAD_DIM, k0:k0 + Q_BLOCK] = ot[0:HEAD_DIM, 0:Q_BLOCK]
            yt_ref[p * LANES + HEAD_DIM:(p + 1) * LANES, k0:k0 + Q_BLOCK] = ot[HEAD_DIM:, Q_BLOCK:]

        scores(0)
        for n in range(len(blocks)):
            if n + 1 < len(blocks):
                scores(n + 1)
            attend(n)
            if n % HEAD_PAIRS == 1:
                conv_half_out_proj(n // HEAD_PAIRS)

    pl.when(i == 0)(lambda: attention(True))
    pl.when(i > 0)(lambda: attention(False))

    ynt = (_group_norm_channels(yt_ref[...]) * aog_ref[...]).astype(BF16)
    h_ref[...] += lax.dot_general(ynt, wo_ref[CONV_WIDTH:, :], TN_DIMS,
                                  preferred_element_type=F32)


def _attn_out(x, yc, qt, k, vt, bias, aog_col, w_out):
    bsz, seq, _ = x.shape
    x_tile = pl.BlockSpec((None, SEQ_TILE, D_MODEL), lambda b, i: (b, i, 0))
    tok_cur = pl.BlockSpec((None, SEQ_TILE, ATTN_WIDTH), lambda b, i: (b, i, 0))
    tok_prev = pl.BlockSpec((None, SEQ_TILE, ATTN_WIDTH),
                            lambda b, i: (b, jnp.maximum(i - 1, 0), 0))
    ch_cur = pl.BlockSpec((None, ATTN_WIDTH, SEQ_TILE), lambda b, i: (b, 0, i))
    ch_prev = pl.BlockSpec((None, ATTN_WIDTH, SEQ_TILE),
                           lambda b, i: (b, 0, jnp.maximum(i - 1, 0)))
    return pl.pallas_call(
        _attn_out_kernel,
        grid=(bsz, seq // SEQ_TILE),
        in_specs=[x_tile, tok_cur, ch_cur, tok_prev, tok_cur, ch_prev, ch_cur,
                  _resident((HEAD_PAIRS, KEY_SPAN, PAIR_COLS)), _resident((ATTN_WIDTH, 1)),
                  _resident((D_MODEL, D_MODEL))],
        out_specs=x_tile,
        out_shape=jax.ShapeDtypeStruct(x.shape, F32),
        scratch_shapes=[pltpu.VMEM((2 * SEQ_TILE, ATTN_WIDTH), BF16),
                        pltpu.VMEM((ATTN_WIDTH, 2 * SEQ_TILE), BF16),
                        pltpu.VMEM((ATTN_WIDTH, SEQ_TILE), F32),
                        pltpu.VMEM((2, KEY_SPAN, PAIR_COLS), F32)],
        compiler_params=pltpu.CompilerParams(
            dimension_semantics=("arbitrary", "arbitrary"), vmem_limit_bytes=VMEM_LIMIT),
        name="attn_out",
    )(x, yc, qt, k, k, vt, vt, bias, aog_col, w_out)


def _mlp_kernel(h_ref, g_ref, w1_ref, w2_ref, o_ref):
    h = h_ref[...]
    xn = _rms_norm_bf16(h, g_ref[...])
    acc = h
    for c in range(D_FF // FF_CHUNK):
        ff = slice(c * FF_CHUNK, (c + 1) * FF_CHUNK)
        hid = jnp.maximum(jnp.dot(xn, w1_ref[:, ff], preferred_element_type=F32), 0.0)
        acc = acc + jnp.dot((hid * hid).astype(BF16), w2_ref[ff, :], preferred_element_type=F32)
    o_ref[...] = acc


def _mlp(h, g, w1, w2):
    tokens = h.shape[0]
    tile = pl.BlockSpec((MLP_TILE, D_MODEL), lambda i: (i, 0))
    return pl.pallas_call(
        _mlp_kernel,
        grid=(tokens // MLP_TILE,),
        in_specs=[tile, _resident((1, D_MODEL)), _resident((D_MODEL, D_FF)),
                  _resident((D_FF, D_MODEL))],
        out_specs=tile,
        out_shape=jax.ShapeDtypeStruct(h.shape, F32),
        compiler_params=pltpu.CompilerParams(
            dimension_semantics=("arbitrary",), vmem_limit_bytes=VMEM_LIMIT),
        name="mlp",
    )(h, g, w1, w2)


def kernel(x, norm_mix_g, w_in, conv_w, conv_b, q_norm_g, k_norm_g, rel_bias, conv_out_g,
           attn_out_g, w_out, norm_mlp_g, w_mlp_in, w_mlp_out):
    bsz, seq, d = x.shape
    depth = w_in.shape[0]
    grp = jnp.arange(CONV_WIDTH) // GROUP
    gmat = jnp.where(grp[:, None] == grp[None, :], 1.0 / GROUP, 0.0).astype(BF16)
    q0 = 3 * CONV_WIDTH
    for l in range(depth):
        w = w_in[l].astype(BF16)
        yc, qt, k, vt = _mix_proj(
            x, norm_mix_g[l][None], w[:, :q0], w[:, q0 + ATTN_WIDTH:q0 + 2 * ATTN_WIDTH],
            w[:, q0:q0 + ATTN_WIDTH].T, w[:, q0 + 2 * ATTN_WIDTH:].T,
            conv_w[l], conv_b[l][None], jnp.tile(q_norm_g[l], N_HEADS)[:, None],
            jnp.tile(k_norm_g[l], N_HEADS)[None], conv_out_g[l][None], gmat)
        h = _attn_out(x, yc, qt, k, vt, _bias_table(rel_bias[l]), attn_out_g[l][:, None],
                      w_out[l].astype(BF16))
        x = _mlp(h.reshape(bsz * seq, d), norm_mlp_g[l][None], w_mlp_in[l].astype(BF16),
                 w_mlp_out[l].astype(BF16)).reshape(bsz, seq, d)
    return x
```

```python
import functools
import math

import jax
import jax.numpy as jnp
from jax import lax
from jax.experimental import pallas as pl
from jax.experimental.pallas import tpu as pltpu

D_MODEL = 1024
CHUNK = 64
LEFT_CHUNKS = 8
BAND = LEFT_CHUNKS + 1
CONV_WIDTH = D_MODEL // 2
GROUP = 64
CONV_K = 3
HEAD_DIM = 64
N_HEADS = (D_MODEL - CONV_WIDTH) // HEAD_DIM
ATTN_WIDTH = N_HEADS * HEAD_DIM
REL_CLIP = 128
REL_TABLE = (CHUNK - 1) + REL_CLIP + 1
D_FF = 4 * D_MODEL
EPS = 1e-6
NEG_INF = -1e30
LOG2_E = math.log2(math.e)

LANES = 128
SUBLANES = 8
BF16_ROWS = 16
SEQ_TILE = LEFT_CHUNKS * CHUNK
Q_BLOCK = 2 * CHUNK
Q_BLOCKS = SEQ_TILE // Q_BLOCK
KEY_SPAN = (BAND + 1) * CHUNK
HEAD_PAIRS = N_HEADS * HEAD_DIM // LANES
PAIR_COLS = 2 * Q_BLOCK
BIAS_FLAT_ROWS = KEY_SPAN - 2 * Q_BLOCK
BIAS_SLAB = 32
OUT_COL_CHUNK = D_MODEL // Q_BLOCKS
FF_CHUNK = 1024
MLP_COLS = 256
VMEM_LIMIT = 48 * 1024 * 1024
VMEM_LIMIT_FUSED = 56 * 1024 * 1024

F32 = jnp.float32
BF16 = jnp.bfloat16
NT_DIMS = (((1,), (1,)), ((), ()))
TN_DIMS = (((0,), (0,)), ((), ()))


def _resident(shape):
    return pl.BlockSpec(shape, lambda *_: (0,) * len(shape), pipeline_mode=pl.Buffered(1))


def _rms_norm_bf16(x, gain):
    ms = jnp.mean(x * x, axis=-1, keepdims=True)
    return (x * lax.rsqrt(ms + EPS) * gain).astype(BF16)


def _group_norm_tokens(y, gmat):
    ms = jnp.dot((y * y).astype(BF16), gmat, preferred_element_type=F32)
    return y * lax.rsqrt(ms + EPS)


def _group_norm_channels(y):
    c, t = y.shape
    y3 = y.reshape(c // GROUP, GROUP, t)
    ms = jnp.mean(y3 * y3, axis=1, keepdims=True)
    return (y3 * lax.rsqrt(ms + EPS)).reshape(c, t)


def _bias_table_kernel(rel_ref, out_ref):
    def col_fields(shape):
        c = lax.broadcasted_iota(jnp.int32, shape, 1)
        return c < Q_BLOCK, (c // CHUNK) % 2, c % Q_BLOCK

    def in_band(r, j):
        return (r >= j * CHUNK) & (r < j * CHUNK + BAND * CHUNK)

    first, j, _ = col_fields((BIAS_FLAT_ROWS, PAIR_COLS))
    r = lax.broadcasted_iota(jnp.int32, (BIAS_FLAT_ROWS, PAIR_COLS), 0)
    for p in range(HEAD_PAIRS):
        far = jnp.where(first, rel_ref[2 * p, REL_TABLE - 1], rel_ref[2 * p + 1, REL_TABLE - 1])
        out_ref[p, 0:BIAS_FLAT_ROWS, :] = jnp.where(in_band(r, j), far * LOG2_E, NEG_INF)

    first, j, u = col_fields((BIAS_SLAB, PAIR_COLS))
    for r0 in range(BIAS_FLAT_ROWS, KEY_SPAN, BIAS_SLAB):
        r = lax.broadcasted_iota(jnp.int32, (BIAS_SLAB, PAIR_COLS), 0) + r0
        dist = LEFT_CHUNKS * CHUNK + u - r
        idx = jnp.clip(dist, -(CHUNK - 1), REL_CLIP) + (CHUNK - 1)
        lo = min(max(LEFT_CHUNKS * CHUNK - (r0 + BIAS_SLAB - 1), -(CHUNK - 1)), REL_CLIP) + CHUNK - 1
        hi = min(max(LEFT_CHUNKS * CHUNK + Q_BLOCK - 1 - r0, -(CHUNK - 1)), REL_CLIP) + CHUNK - 1

        def pick(t, accs):
            hit = idx == t
            return tuple(
                jnp.where(hit, jnp.where(first, rel_ref[2 * p, t], rel_ref[2 * p + 1, t]), acc)
                for p, acc in enumerate(accs))

        zero = jnp.zeros((BIAS_SLAB, PAIR_COLS), F32)
        accs = lax.fori_loop(lo, hi + 1, pick, (zero,) * HEAD_PAIRS)
        for p in range(HEAD_PAIRS):
            out_ref[p, r0:r0 + BIAS_SLAB, :] = jnp.where(in_band(r, j), accs[p] * LOG2_E, NEG_INF)


def _bias_table(rel_bias):
    return pl.pallas_call(
        _bias_table_kernel,
        in_specs=[pl.BlockSpec(memory_space=pltpu.SMEM)],
        out_specs=pl.BlockSpec(memory_space=pltpu.VMEM),
        out_shape=jax.ShapeDtypeStruct((HEAD_PAIRS, KEY_SPAN, PAIR_COLS), F32),
        name="rel_bias_table",
    )(rel_bias)


def _mix_proj_kernel(x_ref, g_ref, wc_ref, wk_ref, wqt_ref, wvt_ref, cw_ref, cb_ref, qg_ref,
                     kg_ref, cog_ref, gm_ref, yc_ref, qt_ref, k_ref, vt_ref, carry_ref):
    i = pl.program_id(1)

    @pl.when(i == 0)
    def _():
        carry_ref[...] = jnp.zeros_like(carry_ref)

    xn = _rms_norm_bf16(x_ref[...], g_ref[...])
    gmat = gm_ref[...]

    def proj(j):
        return jnp.dot(xn, wc_ref[:, j * CONV_WIDTH:(j + 1) * CONV_WIDTH],
                       preferred_element_type=F32)

    u = proj(1) * proj(2)
    rows = lax.broadcasted_iota(jnp.int32, u.shape, 0)
    prev2 = carry_ref[SUBLANES - 2:SUBLANES - 1, :]
    prev1 = carry_ref[SUBLANES - 1:SUBLANES, :]
    u1 = jnp.where(rows == 0, prev1, pltpu.roll(u, 1, 0))
    u2 = jnp.where(rows == 0, prev2, jnp.where(rows == 1, prev1, pltpu.roll(u, 2, 0)))
    carry_ref[...] = u[SEQ_TILE - SUBLANES:, :]
    y = cw_ref[0:1, :] * u2 + cw_ref[1:2, :] * u1 + cw_ref[2:3, :] * u
    yc = proj(0) * (y + cb_ref[...])
    yc_ref[...] = (_group_norm_tokens(yc, gmat) * cog_ref[...]).astype(BF16)

    k = jnp.dot(xn, wk_ref[...], preferred_element_type=F32)
    k_ref[...] = (_group_norm_tokens(k, gmat) * kg_ref[...]).astype(BF16)
    qt = lax.dot_general(wqt_ref[...], xn, NT_DIMS, preferred_element_type=F32)
    qt_ref[...] = (_group_norm_channels(qt) * qg_ref[...] * (HEAD_DIM ** -0.5 * LOG2_E)).astype(BF16)
    vt_ref[...] = lax.dot_general(wvt_ref[...], xn, NT_DIMS, preferred_element_type=F32).astype(BF16)


def _mix_proj(x, g, w_conv, w_k, w_qt, w_vt, conv_w, conv_b, qg_col, kg, cog, gmat):
    bsz, seq, _ = x.shape
    tile = pl.BlockSpec((None, SEQ_TILE, D_MODEL), lambda b, i: (b, i, 0))
    tok_major = pl.BlockSpec((None, SEQ_TILE, CONV_WIDTH), lambda b, i: (b, i, 0))
    ch_major = pl.BlockSpec((None, ATTN_WIDTH, SEQ_TILE), lambda b, i: (b, 0, i))
    tok_sds = jax.ShapeDtypeStruct((bsz, seq, CONV_WIDTH), BF16)
    ch_sds = jax.ShapeDtypeStruct((bsz, ATTN_WIDTH, seq), BF16)
    return pl.pallas_call(
        _mix_proj_kernel,
        grid=(bsz, seq // SEQ_TILE),
        in_specs=[tile, _resident((1, D_MODEL)), _resident((D_MODEL, 3 * CONV_WIDTH)),
                  _resident((D_MODEL, ATTN_WIDTH)), _resident((ATTN_WIDTH, D_MODEL)),
                  _resident((ATTN_WIDTH, D_MODEL)),
                  _resident((CONV_K, CONV_WIDTH)), _resident((1, CONV_WIDTH)),
                  _resident((ATTN_WIDTH, 1)), _resident((1, ATTN_WIDTH)),
                  _resident((1, CONV_WIDTH)), _resident((CONV_WIDTH, CONV_WIDTH))],
        out_specs=[tok_major, ch_major, tok_major, ch_major],
        out_shape=[tok_sds, ch_sds, tok_sds, ch_sds],
        scratch_shapes=[pltpu.VMEM((SUBLANES, CONV_WIDTH), F32)],
        compiler_params=pltpu.CompilerParams(
            dimension_semantics=("arbitrary", "arbitrary"), vmem_limit_bytes=VMEM_LIMIT),
        name="mix_proj",
    )(x, g, w_conv, w_k, w_qt, w_vt, conv_w, conv_b, qg_col, kg, cog, gmat)


def _attn_mlp_kernel(tiles_per_seq, x_ref, yc_ref, qt_ref, k_ref, vt_ref, bias_ref, aog_ref,
                     wo_ref, g2_ref, w1_ref, w2_ref, o_ref,
                     kw_ref, vtw_ref, pen_ref, yt_ref, s_ref, h_ref, xn_ref, hid_ref):
    t = pl.program_id(0)
    n_tiles = pl.num_programs(0) - 1
    first_tile = lax.rem(jnp.minimum(t, n_tiles - 1), tiles_per_seq) == 0

    @pl.when(t == 0)
    def _():
        kw_ref[...] = jnp.zeros_like(kw_ref)
        vtw_ref[...] = jnp.zeros_like(vtw_ref)
        h_ref[...] = jnp.zeros_like(h_ref)

    h_prev = h_ref[...]
    xn_ref[...] = _rms_norm_bf16(h_prev, g2_ref[...])
    o_ref[...] = h_prev

    def mlp_up(c, j):
        cols = slice(c * FF_CHUNK + j * MLP_COLS, c * FF_CHUNK + (j + 1) * MLP_COLS)
        hid = jnp.maximum(jnp.dot(xn_ref[...], w1_ref[:, cols], preferred_element_type=F32), 0.0)
        hid_ref[c % 2, :, j * MLP_COLS:(j + 1) * MLP_COLS] = (hid * hid).astype(BF16)

    def mlp_down(c, j):
        cols = slice(j * MLP_COLS, (j + 1) * MLP_COLS)
        o_ref[:, cols] += jnp.dot(hid_ref[c % 2], w2_ref[c * FF_CHUNK:(c + 1) * FF_CHUNK, cols],
                                  preferred_element_type=F32)

    mlp_pieces = []
    for c in range(D_FF // FF_CHUNK):
        mlp_pieces += [(mlp_up, c, j) for j in range(FF_CHUNK // MLP_COLS)]
        mlp_pieces += [(mlp_down, c, j) for j in range(D_MODEL // MLP_COLS)]

    kw_ref[0:SEQ_TILE, :] = kw_ref[SEQ_TILE:, :]
    kw_ref[SEQ_TILE:, :] = k_ref[...]
    vtw_ref[:, 0:SEQ_TILE] = vtw_ref[:, SEQ_TILE:]
    vtw_ref[:, SEQ_TILE:] = vt_ref[...]
    pen_lane = lax.broadcasted_iota(jnp.int32, (SEQ_TILE, LANES), 1) == 0
    pen_ref[0:SEQ_TILE, :] = jnp.where(pen_lane & first_tile, NEG_INF, 0.0).astype(BF16)
    pen_ref[SEQ_TILE:, :] = jnp.zeros((SEQ_TILE, LANES), BF16)

    first_head = lax.broadcasted_iota(jnp.int32, (LANES, Q_BLOCK), 0) < HEAD_DIM
    pen_rows = (lax.broadcasted_iota(jnp.int32, (LANES, PAIR_COLS), 0) == 0).astype(BF16)
    blocks = [(qb * Q_BLOCK, p) for qb in range(Q_BLOCKS) for p in range(HEAD_PAIRS)]

    def scores(n):
        k0, p = blocks[n]
        ch = slice(p * LANES, (p + 1) * LANES)
        qt = qt_ref[ch, k0:k0 + Q_BLOCK]
        zero = jnp.zeros_like(qt)
        rhs = jnp.concatenate([jnp.where(first_head, qt, zero),
                               jnp.where(first_head, zero, qt)], axis=1)
        rhs = jnp.concatenate([rhs, pen_rows], axis=0)
        lhs = jnp.concatenate([kw_ref[k0:k0 + KEY_SPAN, ch], pen_ref[k0:k0 + KEY_SPAN, :]],
                              axis=1)
        s_ref[n % 2] = jnp.dot(lhs, rhs, preferred_element_type=F32) + bias_ref[p]

    def attend(n):
        k0, p = blocks[n]
        ch = slice(p * LANES, (p + 1) * LANES)
        s = s_ref[n % 2]
        pr = jnp.exp2(s - jnp.max(s, axis=0, keepdims=True)).astype(BF16)
        vt1 = jnp.concatenate([vtw_ref[ch, k0:k0 + KEY_SPAN],
                               jnp.ones((BF16_ROWS, KEY_SPAN), BF16)], axis=0)
        ot = jnp.dot(vt1, pr, preferred_element_type=F32)
        ot = ot[0:LANES] / ot[LANES:LANES + 1]
        yt_ref[p * LANES:p * LANES + HEAD_DIM, k0:k0 + Q_BLOCK] = ot[0:HEAD_DIM, 0:Q_BLOCK]
        yt_ref[p * LANES + HEAD_DIM:(p + 1) * LANES, k0:k0 + Q_BLOCK] = ot[HEAD_DIM:, Q_BLOCK:]

    def conv_half_out_proj(c):
        cols = slice(c * OUT_COL_CHUNK, (c + 1) * OUT_COL_CHUNK)
        h_ref[:, cols] = x_ref[:, cols] + jnp.dot(yc_ref[...], wo_ref[0:CONV_WIDTH, cols],
                                                  preferred_element_type=F32)

    per_block = len(mlp_pieces) // len(blocks)
    scores(0)
    for n in range(len(blocks)):
        if n + 1 < len(blocks):
            scores(n + 1)
        attend(n)
        for fn, c, j in mlp_pieces[n * per_block:(n + 1) * per_block]:
            fn(c, j)
        if n % HEAD_PAIRS == 1:
            conv_half_out_proj(n // HEAD_PAIRS)

    ynt = (_group_norm_channels(yt_ref[...]) * aog_ref[...]).astype(BF16)
    h_ref[...] += lax.dot_general(ynt, wo_ref[CONV_WIDTH:, :], TN_DIMS,
                                  preferred_element_type=F32)


def _attn_mlp(x, yc, qt, k, vt, bias, aog_col, w_out, g2, w1, w2):
    bsz, seq, _ = x.shape
    tiles_per_seq = seq // SEQ_TILE
    n_tiles = bsz * tiles_per_seq

    def tile_of(t):
        t = jnp.minimum(t, n_tiles - 1)
        return t // tiles_per_seq, t % tiles_per_seq

    def tok_map(t):
        b, i = tile_of(t)
        return b, i, 0

    def ch_map(t):
        b, i = tile_of(t)
        return b, 0, i

    def out_map(t):
        b, i = tile_of(jnp.maximum(t - 1, 0))
        return b, i, 0

    x_tile = pl.BlockSpec((None, SEQ_TILE, D_MODEL), tok_map)
    tok_tile = pl.BlockSpec((None, SEQ_TILE, ATTN_WIDTH), tok_map)
    ch_tile = pl.BlockSpec((None, ATTN_WIDTH, SEQ_TILE), ch_map)
    return pl.pallas_call(
        functools.partial(_attn_mlp_kernel, tiles_per_seq),
        grid=(n_tiles + 1,),
        in_specs=[x_tile, tok_tile, ch_tile, tok_tile, ch_tile,
                  _resident((HEAD_PAIRS, KEY_SPAN, PAIR_COLS)), _resident((ATTN_WIDTH, 1)),
                  _resident((D_MODEL, D_MODEL)), _resident((1, D_MODEL)),
                  _resident((D_MODEL, D_FF)), _resident((D_FF, D_MODEL))],
        out_specs=pl.BlockSpec((None, SEQ_TILE, D_MODEL), out_map),
        out_shape=jax.ShapeDtypeStruct(x.shape, F32),
        scratch_shapes=[pltpu.VMEM((2 * SEQ_TILE, ATTN_WIDTH), BF16),
                        pltpu.VMEM((ATTN_WIDTH, 2 * SEQ_TILE), BF16),
                        pltpu.VMEM((2 * SEQ_TILE, LANES), BF16),
                        pltpu.VMEM((ATTN_WIDTH, SEQ_TILE), F32),
                        pltpu.VMEM((2, KEY_SPAN, PAIR_COLS), F32),
                        pltpu.VMEM((SEQ_TILE, D_MODEL), F32),
                        pltpu.VMEM((SEQ_TILE, D_MODEL), BF16),
                        pltpu.VMEM((2, SEQ_TILE, FF_CHUNK), BF16)],
        compiler_params=pltpu.CompilerParams(
            dimension_semantics=("arbitrary",), vmem_limit_bytes=VMEM_LIMIT_FUSED),
        name="attn_mlp",
    )(x, yc, qt, k, vt, bias, aog_col, w_out, g2, w1, w2)


def kernel(x, norm_mix_g, w_in, conv_w, conv_b, q_norm_g, k_norm_g, rel_bias, conv_out_g,
           attn_out_g, w_out, norm_mlp_g, w_mlp_in, w_mlp_out):
    depth = w_in.shape[0]
    grp = jnp.arange(CONV_WIDTH) // GROUP
    gmat = jnp.where(grp[:, None] == grp[None, :], 1.0 / GROUP, 0.0).astype(BF16)
    q0 = 3 * CONV_WIDTH
    for l in range(depth):
        w = w_in[l].astype(BF16)
        yc, qt, k, vt = _mix_proj(
            x, norm_mix_g[l][None], w[:, :q0], w[:, q0 + ATTN_WIDTH:q0 + 2 * ATTN_WIDTH],
            w[:, q0:q0 + ATTN_WIDTH].T, w[:, q0 + 2 * ATTN_WIDTH:].T,
            conv_w[l], conv_b[l][None], jnp.tile(q_norm_g[l], N_HEADS)[:, None],
            jnp.tile(k_norm_g[l], N_HEADS)[None], conv_out_g[l][None], gmat)
        x = _attn_mlp(x, yc, qt, k, vt, _bias_table(rel_bias[l]), attn_out_g[l][:, None],
                      w_out[l].astype(BF16), norm_mlp_g[l][None], w_mlp_in[l].astype(BF16),
                      w_mlp_out[l].astype(BF16))
    return x
```

```python
import functools
import math

import jax
import jax.numpy as jnp
from jax import lax
from jax.experimental import pallas as pl
from jax.experimental.pallas import tpu as pltpu

D_MODEL = 1024
CHUNK = 64
LEFT_CHUNKS = 8
BAND = LEFT_CHUNKS + 1
CONV_WIDTH = D_MODEL // 2
GROUP = 64
CONV_K = 3
HEAD_DIM = 64
N_HEADS = (D_MODEL - CONV_WIDTH) // HEAD_DIM
ATTN_WIDTH = N_HEADS * HEAD_DIM
REL_CLIP = 128
REL_TABLE = (CHUNK - 1) + REL_CLIP + 1
D_FF = 4 * D_MODEL
EPS = 1e-6
NEG_INF = -1e30
LOG2_E = math.log2(math.e)

LANES = 128
SUBLANES = 8
BF16_ROWS = 16
SEQ_TILE = LEFT_CHUNKS * CHUNK
Q_BLOCK = 2 * CHUNK
Q_BLOCKS = SEQ_TILE // Q_BLOCK
KEY_SPAN = (BAND + 1) * CHUNK
HEAD_PAIRS = N_HEADS * HEAD_DIM // LANES
PAIR_COLS = 2 * Q_BLOCK
BIAS_FLAT_ROWS = KEY_SPAN - 2 * Q_BLOCK
BIAS_SLAB = 32
OUT_COL_CHUNK = D_MODEL // Q_BLOCKS
FF_CHUNK = 1024
MLP_COLS = 256
VMEM_LIMIT = 48 * 1024 * 1024
VMEM_LIMIT_FUSED = 56 * 1024 * 1024

F32 = jnp.float32
BF16 = jnp.bfloat16
NT_DIMS = (((1,), (1,)), ((), ()))
TN_DIMS = (((0,), (0,)), ((), ()))


def _resident(shape):
    return pl.BlockSpec(shape, lambda *_: (0,) * len(shape), pipeline_mode=pl.Buffered(1))


def _rms_norm_bf16(x, gain):
    ms = jnp.mean(x * x, axis=-1, keepdims=True)
    return (x * lax.rsqrt(ms + EPS) * gain).astype(BF16)


def _group_norm_tokens(y, gmat):
    ms = jnp.dot((y * y).astype(BF16), gmat, preferred_element_type=F32)
    return y * lax.rsqrt(ms + EPS)


def _group_norm_channels(y):
    c, t = y.shape
    y3 = y.reshape(c // GROUP, GROUP, t)
    ms = jnp.mean(y3 * y3, axis=1, keepdims=True)
    return (y3 * lax.rsqrt(ms + EPS)).reshape(c, t)


def _bias_table_kernel(rel_ref, out_ref):
    def col_fields(shape):
        c = lax.broadcasted_iota(jnp.int32, shape, 1)
        return c < Q_BLOCK, (c // CHUNK) % 2, c % Q_BLOCK

    def in_band(r, j):
        return (r >= j * CHUNK) & (r < j * CHUNK + BAND * CHUNK)

    first, j, _ = col_fields((BIAS_FLAT_ROWS, PAIR_COLS))
    r = lax.broadcasted_iota(jnp.int32, (BIAS_FLAT_ROWS, PAIR_COLS), 0)
    for p in range(HEAD_PAIRS):
        far = jnp.where(first, rel_ref[2 * p, REL_TABLE - 1], rel_ref[2 * p + 1, REL_TABLE - 1])
        out_ref[p, 0:BIAS_FLAT_ROWS, :] = jnp.where(in_band(r, j), far * LOG2_E, NEG_INF)

    first, j, u = col_fields((BIAS_SLAB, PAIR_COLS))
    for r0 in range(BIAS_FLAT_ROWS, KEY_SPAN, BIAS_SLAB):
        r = lax.broadcasted_iota(jnp.int32, (BIAS_SLAB, PAIR_COLS), 0) + r0
        dist = LEFT_CHUNKS * CHUNK + u - r
        idx = jnp.clip(dist, -(CHUNK - 1), REL_CLIP) + (CHUNK - 1)
        lo = min(max(LEFT_CHUNKS * CHUNK - (r0 + BIAS_SLAB - 1), -(CHUNK - 1)), REL_CLIP) + CHUNK - 1
        hi = min(max(LEFT_CHUNKS * CHUNK + Q_BLOCK - 1 - r0, -(CHUNK - 1)), REL_CLIP) + CHUNK - 1

        def pick(t, accs):
            hit = idx == t
            return tuple(
                jnp.where(hit, jnp.where(first, rel_ref[2 * p, t], rel_ref[2 * p + 1, t]), acc)
                for p, acc in enumerate(accs))

        zero = jnp.zeros((BIAS_SLAB, PAIR_COLS), F32)
        accs = lax.fori_loop(lo, hi + 1, pick, (zero,) * HEAD_PAIRS)
        for p in range(HEAD_PAIRS):
            out_ref[p, r0:r0 + BIAS_SLAB, :] = jnp.where(in_band(r, j), accs[p] * LOG2_E, NEG_INF)


def _bias_table(rel_bias):
    return pl.pallas_call(
        _bias_table_kernel,
        in_specs=[pl.BlockSpec(memory_space=pltpu.SMEM)],
        out_specs=pl.BlockSpec(memory_space=pltpu.VMEM),
        out_shape=jax.ShapeDtypeStruct((HEAD_PAIRS, KEY_SPAN, PAIR_COLS), F32),
        name="rel_bias_table",
    )(rel_bias)


def _mix_proj_kernel(x_ref, g_ref, wc_ref, wk_ref, wqt_ref, wvt_ref, cw_ref, cb_ref, qg_ref,
                     kg_ref, cog_ref, gm_ref, yc_ref, qt_ref, k_ref, vt_ref, carry_ref):
    i = pl.program_id(1)

    @pl.when(i == 0)
    def _():
        carry_ref[...] = jnp.zeros_like(carry_ref)

    xn = _rms_norm_bf16(x_ref[...], g_ref[...])
    gmat = gm_ref[...]

    def proj(j):
        return jnp.dot(xn, wc_ref[:, j * CONV_WIDTH:(j + 1) * CONV_WIDTH],
                       preferred_element_type=F32)

    u = proj(1) * proj(2)
    rows = lax.broadcasted_iota(jnp.int32, u.shape, 0)
    prev2 = carry_ref[SUBLANES - 2:SUBLANES - 1, :]
    prev1 = carry_ref[SUBLANES - 1:SUBLANES, :]
    u1 = jnp.where(rows == 0, prev1, pltpu.roll(u, 1, 0))
    u2 = jnp.where(rows == 0, prev2, jnp.where(rows == 1, prev1, pltpu.roll(u, 2, 0)))
    carry_ref[...] = u[SEQ_TILE - SUBLANES:, :]
    y = cw_ref[0:1, :] * u2 + cw_ref[1:2, :] * u1 + cw_ref[2:3, :] * u
    yc = proj(0) * (y + cb_ref[...])
    yc_ref[...] = (_group_norm_tokens(yc, gmat) * cog_ref[...]).astype(BF16)

    k = jnp.dot(xn, wk_ref[...], preferred_element_type=F32)
    k_ref[...] = (_group_norm_tokens(k, gmat) * kg_ref[...]).astype(BF16)
    qt = lax.dot_general(wqt_ref[...], xn, NT_DIMS, preferred_element_type=F32)
    qt_ref[...] = (_group_norm_channels(qt) * qg_ref[...] * (HEAD_DIM ** -0.5 * LOG2_E)).astype(BF16)
    vt_ref[...] = lax.dot_general(wvt_ref[...], xn, NT_DIMS, preferred_element_type=F32).astype(BF16)


def _mix_proj(x, g, w_conv, w_k, w_qt, w_vt, conv_w, conv_b, qg_col, kg, cog, gmat):
    bsz, seq, _ = x.shape
    tile = pl.BlockSpec((None, SEQ_TILE, D_MODEL), lambda b, i: (b, i, 0))
    tok_major = pl.BlockSpec((None, SEQ_TILE, CONV_WIDTH), lambda b, i: (b, i, 0))
    ch_major = pl.BlockSpec((None, ATTN_WIDTH, SEQ_TILE), lambda b, i: (b, 0, i))
    tok_sds = jax.ShapeDtypeStruct((bsz, seq, CONV_WIDTH), BF16)
    ch_sds = jax.ShapeDtypeStruct((bsz, ATTN_WIDTH, seq), BF16)
    return pl.pallas_call(
        _mix_proj_kernel,
        grid=(bsz, seq // SEQ_TILE),
        in_specs=[tile, _resident((1, D_MODEL)), _resident((D_MODEL, 3 * CONV_WIDTH)),
                  _resident((D_MODEL, ATTN_WIDTH)), _resident((ATTN_WIDTH, D_MODEL)),
                  _resident((ATTN_WIDTH, D_MODEL)),
                  _resident((CONV_K, CONV_WIDTH)), _resident((1, CONV_WIDTH)),
                  _resident((ATTN_WIDTH, 1)), _resident((1, ATTN_WIDTH)),
                  _resident((1, CONV_WIDTH)), _resident((CONV_WIDTH, CONV_WIDTH))],
        out_specs=[tok_major, ch_major, tok_major, ch_major],
        out_shape=[tok_sds, ch_sds, tok_sds, ch_sds],
        scratch_shapes=[pltpu.VMEM((SUBLANES, CONV_WIDTH), F32)],
        compiler_params=pltpu.CompilerParams(
            dimension_semantics=("arbitrary", "arbitrary"), vmem_limit_bytes=VMEM_LIMIT),
        name="mix_proj",
    )(x, g, w_conv, w_k, w_qt, w_vt, conv_w, conv_b, qg_col, kg, cog, gmat)


def _attn_mlp_kernel(tiles_per_seq, x_ref, yc_ref, qt_ref, k_ref, vt_ref, bias_ref, aog_ref,
                     wo_ref, g2_ref, w1_ref, w2_ref, o_ref,
                     kw_ref, vtw_ref, pen_ref, yt_ref, s_ref, h_ref, hc_ref, xn_ref, hid_ref):
    t = pl.program_id(0)
    n_tiles = pl.num_programs(0) - 1
    first_tile = lax.rem(jnp.minimum(t, n_tiles - 1), tiles_per_seq) == 0

    @pl.when(t == 0)
    def _():
        kw_ref[...] = jnp.zeros_like(kw_ref)
        vtw_ref[...] = jnp.zeros_like(vtw_ref)
        h_ref[...] = jnp.zeros_like(h_ref)

    def mlp_prologue():
        h_prev = h_ref[...]
        xn_ref[...] = _rms_norm_bf16(h_prev, g2_ref[...])
        o_ref[...] = h_prev

    def conv_half_out_proj(c):
        cols = slice(c * OUT_COL_CHUNK, (c + 1) * OUT_COL_CHUNK)
        hc_ref[:, cols] = x_ref[:, cols] + jnp.dot(yc_ref[...], wo_ref[0:CONV_WIDTH, cols],
                                                   preferred_element_type=F32)

    conv_half_out_proj(0)
    conv_half_out_proj(1)

    def mlp_up(c, j):
        cols = slice(c * FF_CHUNK + j * MLP_COLS, c * FF_CHUNK + (j + 1) * MLP_COLS)
        hid = jnp.maximum(jnp.dot(xn_ref[...], w1_ref[:, cols], preferred_element_type=F32), 0.0)
        hid_ref[c % 2, :, j * MLP_COLS:(j + 1) * MLP_COLS] = (hid * hid).astype(BF16)

    def mlp_down(c, j):
        cols = slice(j * MLP_COLS, (j + 1) * MLP_COLS)
        o_ref[:, cols] += jnp.dot(hid_ref[c % 2], w2_ref[c * FF_CHUNK:(c + 1) * FF_CHUNK, cols],
                                  preferred_element_type=F32)

    mlp_pieces = []
    for c in range(D_FF // FF_CHUNK):
        mlp_pieces += [(mlp_up, c, j) for j in range(FF_CHUNK // MLP_COLS)]
        mlp_pieces += [(mlp_down, c, j) for j in range(D_MODEL // MLP_COLS)]

    kw_ref[0:SEQ_TILE, :] = kw_ref[SEQ_TILE:, :]
    kw_ref[SEQ_TILE:, :] = k_ref[...]
    vtw_ref[:, 0:SEQ_TILE] = vtw_ref[:, SEQ_TILE:]
    vtw_ref[:, SEQ_TILE:] = vt_ref[...]
    pen_lane = lax.broadcasted_iota(jnp.int32, (SEQ_TILE, LANES), 1) == 0
    pen_ref[0:SEQ_TILE, :] = jnp.where(pen_lane & first_tile, NEG_INF, 0.0).astype(BF16)
    pen_ref[SEQ_TILE:, :] = jnp.zeros((SEQ_TILE, LANES), BF16)

    first_head = lax.broadcasted_iota(jnp.int32, (LANES, Q_BLOCK), 0) < HEAD_DIM
    pen_rows = (lax.broadcasted_iota(jnp.int32, (LANES, PAIR_COLS), 0) == 0).astype(BF16)
    blocks = [(qb * Q_BLOCK, p) for qb in range(Q_BLOCKS) for p in range(HEAD_PAIRS)]

    def scores(n):
        k0, p = blocks[n]
        ch = slice(p * LANES, (p + 1) * LANES)
        qt = qt_ref[ch, k0:k0 + Q_BLOCK]
        zero = jnp.zeros_like(qt)
        rhs = jnp.concatenate([jnp.where(first_head, qt, zero),
                               jnp.where(first_head, zero, qt)], axis=1)
        rhs = jnp.concatenate([rhs, pen_rows], axis=0)
        lhs = jnp.concatenate([kw_ref[k0:k0 + KEY_SPAN, ch], pen_ref[k0:k0 + KEY_SPAN, :]],
                              axis=1)
        s_ref[n % 2] = jnp.dot(lhs, rhs, preferred_element_type=F32) + bias_ref[p]

    def attend(n):
        k0, p = blocks[n]
        ch = slice(p * LANES, (p + 1) * LANES)
        s = s_ref[n % 2]
        pr = jnp.exp2(s - jnp.max(s, axis=0, keepdims=True))
        denom = jnp.sum(pr, axis=0, keepdims=True)
        ot = jnp.dot(vtw_ref[ch, k0:k0 + KEY_SPAN], pr.astype(BF16),
                     preferred_element_type=F32) / denom
        yt_ref[p * LANES:p * LANES + HEAD_DIM, k0:k0 + Q_BLOCK] = ot[0:HEAD_DIM, 0:Q_BLOCK]
        yt_ref[p * LANES + HEAD_DIM:(p + 1) * LANES, k0:k0 + Q_BLOCK] = ot[HEAD_DIM:, Q_BLOCK:]

    per_block = 2
    scores(0)
    conv_half_out_proj(2)
    conv_half_out_proj(3)
    mlp_prologue()
    for n in range(len(blocks)):
        if n + 1 < len(blocks):
            scores(n + 1)
        attend(n)
        if n + 1 < len(blocks):
            for fn, c, j in mlp_pieces[n * per_block:(n + 1) * per_block]:
                fn(c, j)
    ynt = (_group_norm_channels(yt_ref[...]) * aog_ref[...]).astype(BF16)
    for fn, c, j in mlp_pieces[(len(blocks) - 1) * per_block:]:
        fn(c, j)
    h_ref[...] = hc_ref[...] + lax.dot_general(ynt, wo_ref[CONV_WIDTH:, :], TN_DIMS,
                                               preferred_element_type=F32)


def _attn_mlp(x, yc, qt, k, vt, bias, aog_col, w_out, g2, w1, w2):
    bsz, seq, _ = x.shape
    tiles_per_seq = seq // SEQ_TILE
    n_tiles = bsz * tiles_per_seq

    def tile_of(t):
        t = jnp.minimum(t, n_tiles - 1)
        return t // tiles_per_seq, t % tiles_per_seq

    def tok_map(t):
        b, i = tile_of(t)
        return b, i, 0

    def ch_map(t):
        b, i = tile_of(t)
        return b, 0, i

    def out_map(t):
        b, i = tile_of(jnp.maximum(t - 1, 0))
        return b, i, 0

    x_tile = pl.BlockSpec((None, SEQ_TILE, D_MODEL), tok_map)
    tok_tile = pl.BlockSpec((None, SEQ_TILE, ATTN_WIDTH), tok_map)
    ch_tile = pl.BlockSpec((None, ATTN_WIDTH, SEQ_TILE), ch_map)
    return pl.pallas_call(
        functools.partial(_attn_mlp_kernel, tiles_per_seq),
        grid=(n_tiles + 1,),
        in_specs=[x_tile, tok_tile, ch_tile, tok_tile, ch_tile,
                  _resident((HEAD_PAIRS, KEY_SPAN, PAIR_COLS)), _resident((ATTN_WIDTH, 1)),
                  _resident((D_MODEL, D_MODEL)), _resident((1, D_MODEL)),
                  _resident((D_MODEL, D_FF)), _resident((D_FF, D_MODEL))],
        out_specs=pl.BlockSpec((None, SEQ_TILE, D_MODEL), out_map),
        out_shape=jax.ShapeDtypeStruct(x.shape, F32),
        scratch_shapes=[pltpu.VMEM((2 * SEQ_TILE, ATTN_WIDTH), BF16),
                        pltpu.VMEM((ATTN_WIDTH, 2 * SEQ_TILE), BF16),
                        pltpu.VMEM((2 * SEQ_TILE, LANES), BF16),
                        pltpu.VMEM((ATTN_WIDTH, SEQ_TILE), F32),
                        pltpu.VMEM((2, KEY_SPAN, PAIR_COLS), F32),
                        pltpu.VMEM((SEQ_TILE, D_MODEL), F32),
                        pltpu.VMEM((SEQ_TILE, D_MODEL), F32),
                        pltpu.VMEM((SEQ_TILE, D_MODEL), BF16),
                        pltpu.VMEM((2, SEQ_TILE, FF_CHUNK), BF16)],
        compiler_params=pltpu.CompilerParams(
            dimension_semantics=("arbitrary",), vmem_limit_bytes=VMEM_LIMIT_FUSED),
        name="attn_mlp",
    )(x, yc, qt, k, vt, bias, aog_col, w_out, g2, w1, w2)


def kernel(x, norm_mix_g, w_in, conv_w, conv_b, q_norm_g, k_norm_g, rel_bias, conv_out_g,
           attn_out_g, w_out, norm_mlp_g, w_mlp_in, w_mlp_out):
    depth = w_in.shape[0]
    grp = jnp.arange(CONV_WIDTH) // GROUP
    gmat = jnp.where(grp[:, None] == grp[None, :], 1.0 / GROUP, 0.0).astype(BF16)
    q0 = 3 * CONV_WIDTH
    for l in range(depth):
        w = w_in[l].astype(BF16)
        yc, qt, k, vt = _mix_proj(
            x, norm_mix_g[l][None], w[:, :q0], w[:, q0 + ATTN_WIDTH:q0 + 2 * ATTN_WIDTH],
            w[:, q0:q0 + ATTN_WIDTH].T, w[:, q0 + 2 * ATTN_WIDTH:].T,
            conv_w[l], conv_b[l][None], jnp.tile(q_norm_g[l], N_HEADS)[:, None],
            jnp.tile(k_norm_g[l], N_HEADS)[None], conv_out_g[l][None], gmat)
        x = _attn_mlp(x, yc, qt, k, vt, _bias_table(rel_bias[l]), attn_out_g[l][:, None],
                      w_out[l].astype(BF16), norm_mlp_g[l][None], w_mlp_in[l].astype(BF16),
                      w_mlp_out[l].astype(BF16))
    return x
```

```python
import functools
import math

import jax
import jax.numpy as jnp
from jax import lax
from jax.experimental import pallas as pl
from jax.experimental.pallas import tpu as pltpu

D_MODEL = 1024
CHUNK = 64
LEFT_CHUNKS = 8
BAND = LEFT_CHUNKS + 1
CONV_WIDTH = D_MODEL // 2
GROUP = 64
CONV_K = 3
HEAD_DIM = 64
N_HEADS = (D_MODEL - CONV_WIDTH) // HEAD_DIM
ATTN_WIDTH = N_HEADS * HEAD_DIM
REL_CLIP = 128
REL_TABLE = (CHUNK - 1) + REL_CLIP + 1
D_FF = 4 * D_MODEL
EPS = 1e-6
NEG_INF = -1e30
LOG2_E = math.log2(math.e)

LANES = 128
SUBLANES = 8
BF16_ROWS = 16
SEQ_TILE = LEFT_CHUNKS * CHUNK
Q_BLOCK = 2 * CHUNK
Q_BLOCKS = SEQ_TILE // Q_BLOCK
KEY_SPAN = (BAND + 1) * CHUNK
HEAD_PAIRS = N_HEADS * HEAD_DIM // LANES
PAIR_COLS = 2 * Q_BLOCK
BIAS_FLAT_ROWS = KEY_SPAN - 2 * Q_BLOCK
BIAS_SLAB = 32
OUT_COL_CHUNK = D_MODEL // Q_BLOCKS
FF_CHUNK = 1024
MLP_COLS = 256
VMEM_LIMIT = 48 * 1024 * 1024
VMEM_LIMIT_FUSED = 56 * 1024 * 1024

F32 = jnp.float32
BF16 = jnp.bfloat16
NT_DIMS = (((1,), (1,)), ((), ()))
TN_DIMS = (((0,), (0,)), ((), ()))


def _resident(shape):
    return pl.BlockSpec(shape, lambda *_: (0,) * len(shape), pipeline_mode=pl.Buffered(1))


def _rms_norm_bf16(x, gain):
    ms = jnp.mean(x * x, axis=-1, keepdims=True)
    return (x * lax.rsqrt(ms + EPS) * gain).astype(BF16)


def _group_norm_tokens(y, gmat):
    ms = jnp.dot((y * y).astype(BF16), gmat, preferred_element_type=F32)
    return y * lax.rsqrt(ms + EPS)


def _group_norm_channels(y):
    c, t = y.shape
    y3 = y.reshape(c // GROUP, GROUP, t)
    ms = jnp.mean(y3 * y3, axis=1, keepdims=True)
    return (y3 * lax.rsqrt(ms + EPS)).reshape(c, t)


def _bias_table_kernel(rel_ref, out_ref):
    def col_fields(shape):
        c = lax.broadcasted_iota(jnp.int32, shape, 1)
        return c < Q_BLOCK, (c // CHUNK) % 2, c % Q_BLOCK

    def in_band(r, j):
        return (r >= j * CHUNK) & (r < j * CHUNK + BAND * CHUNK)

    first, j, _ = col_fields((BIAS_FLAT_ROWS, PAIR_COLS))
    r = lax.broadcasted_iota(jnp.int32, (BIAS_FLAT_ROWS, PAIR_COLS), 0)
    for p in range(HEAD_PAIRS):
        far = jnp.where(first, rel_ref[2 * p, REL_TABLE - 1], rel_ref[2 * p + 1, REL_TABLE - 1])
        out_ref[p, 0:BIAS_FLAT_ROWS, :] = jnp.where(in_band(r, j), far * LOG2_E, NEG_INF)

    first, j, u = col_fields((BIAS_SLAB, PAIR_COLS))
    for r0 in range(BIAS_FLAT_ROWS, KEY_SPAN, BIAS_SLAB):
        r = lax.broadcasted_iota(jnp.int32, (BIAS_SLAB, PAIR_COLS), 0) + r0
        dist = LEFT_CHUNKS * CHUNK + u - r
        idx = jnp.clip(dist, -(CHUNK - 1), REL_CLIP) + (CHUNK - 1)
        lo = min(max(LEFT_CHUNKS * CHUNK - (r0 + BIAS_SLAB - 1), -(CHUNK - 1)), REL_CLIP) + CHUNK - 1
        hi = min(max(LEFT_CHUNKS * CHUNK + Q_BLOCK - 1 - r0, -(CHUNK - 1)), REL_CLIP) + CHUNK - 1

        def pick(t, accs):
            hit = idx == t
            return tuple(
                jnp.where(hit, jnp.where(first, rel_ref[2 * p, t], rel_ref[2 * p + 1, t]), acc)
                for p, acc in enumerate(accs))

        zero = jnp.zeros((BIAS_SLAB, PAIR_COLS), F32)
        accs = lax.fori_loop(lo, hi + 1, pick, (zero,) * HEAD_PAIRS)
        for p in range(HEAD_PAIRS):
            out_ref[p, r0:r0 + BIAS_SLAB, :] = jnp.where(in_band(r, j), accs[p] * LOG2_E, NEG_INF)


def _bias_table(rel_bias):
    return pl.pallas_call(
        _bias_table_kernel,
        in_specs=[pl.BlockSpec(memory_space=pltpu.SMEM)],
        out_specs=pl.BlockSpec(memory_space=pltpu.VMEM),
        out_shape=jax.ShapeDtypeStruct((HEAD_PAIRS, KEY_SPAN, PAIR_COLS), F32),
        name="rel_bias_table",
    )(rel_bias)


COL_CONV_W, COL_CONV_B, COL_CONV_G, COL_Q_G, COL_K_G, N_COL_PARAMS = 0, 3, 4, 5, 6, 8


def _mix_proj_kernel(x_ref, g_ref, wt_ref, colp_ref, yc_ref, qt_ref, k_ref, vt_ref, carry_ref):
    i = pl.program_id(1)

    @pl.when(i == 0)
    def _():
        carry_ref[...] = jnp.zeros_like(carry_ref)

    xn = _rms_norm_bf16(x_ref[...], g_ref[...])

    def proj_t(j):
        return lax.dot_general(wt_ref[j * CONV_WIDTH:(j + 1) * CONV_WIDTH, :], xn, NT_DIMS,
                               preferred_element_type=F32)

    def col(j):
        return colp_ref[:, j:j + 1]

    u = proj_t(1) * proj_t(2)
    ext = jnp.concatenate([carry_ref[...], u], axis=1)
    u1 = pltpu.roll(ext, 1, 1)[:, LANES:]
    u2 = pltpu.roll(ext, 2, 1)[:, LANES:]
    carry_ref[...] = u[:, SEQ_TILE - LANES:]
    y = col(COL_CONV_W) * u2 + col(COL_CONV_W + 1) * u1 + col(COL_CONV_W + 2) * u
    yc = proj_t(0) * (y + col(COL_CONV_B))
    yc_ref[...] = (_group_norm_channels(yc) * col(COL_CONV_G)).T.astype(BF16)

    k_ref[...] = (_group_norm_channels(proj_t(4)) * col(COL_K_G)).T.astype(BF16)
    qt_ref[...] = (_group_norm_channels(proj_t(3)) * col(COL_Q_G)
                   * (HEAD_DIM ** -0.5 * LOG2_E)).astype(BF16)
    vt_ref[...] = proj_t(5).astype(BF16)


def _mix_proj(x, g, w_in_t, col_params):
    bsz, seq, _ = x.shape
    tile = pl.BlockSpec((None, SEQ_TILE, D_MODEL), lambda b, i: (b, i, 0))
    tok_major = pl.BlockSpec((None, SEQ_TILE, CONV_WIDTH), lambda b, i: (b, i, 0))
    ch_major = pl.BlockSpec((None, ATTN_WIDTH, SEQ_TILE), lambda b, i: (b, 0, i))
    tok_sds = jax.ShapeDtypeStruct((bsz, seq, CONV_WIDTH), BF16)
    ch_sds = jax.ShapeDtypeStruct((bsz, ATTN_WIDTH, seq), BF16)
    return pl.pallas_call(
        _mix_proj_kernel,
        grid=(bsz, seq // SEQ_TILE),
        in_specs=[tile, _resident((1, D_MODEL)), _resident(w_in_t.shape),
                  _resident((CONV_WIDTH, N_COL_PARAMS))],
        out_specs=[tok_major, ch_major, tok_major, ch_major],
        out_shape=[tok_sds, ch_sds, tok_sds, ch_sds],
        scratch_shapes=[pltpu.VMEM((CONV_WIDTH, LANES), F32)],
        compiler_params=pltpu.CompilerParams(
            dimension_semantics=("arbitrary", "arbitrary"), vmem_limit_bytes=VMEM_LIMIT),
        name="mix_proj",
    )(x, g, w_in_t, col_params)


def _attn_mlp_kernel(tiles_per_seq, x_ref, yc_ref, qt_ref, k_ref, vt_ref, bias_ref, aog_ref,
                     wo_ref, g2_ref, w1_ref, w2_ref, o_ref,
                     kw_ref, vtw_ref, pen_ref, yt_ref, s_ref, h_ref, hc_ref, xn_ref, hid_ref):
    t = pl.program_id(0)
    n_tiles = pl.num_programs(0) - 1
    first_tile = lax.rem(jnp.minimum(t, n_tiles - 1), tiles_per_seq) == 0

    @pl.when(t == 0)
    def _():
        kw_ref[...] = jnp.zeros_like(kw_ref)
        vtw_ref[...] = jnp.zeros_like(vtw_ref)
        h_ref[...] = jnp.zeros_like(h_ref)

    def mlp_prologue():
        h_prev = h_ref[...]
        xn_ref[...] = _rms_norm_bf16(h_prev, g2_ref[...])
        o_ref[...] = h_prev

    def conv_half_out_proj(c):
        cols = slice(c * OUT_COL_CHUNK, (c + 1) * OUT_COL_CHUNK)
        hc_ref[:, cols] = x_ref[:, cols] + jnp.dot(yc_ref[...], wo_ref[0:CONV_WIDTH, cols],
                                                   preferred_element_type=F32)

    conv_half_out_proj(0)
    conv_half_out_proj(1)

    def mlp_up(c, j):
        cols = slice(c * FF_CHUNK + j * MLP_COLS, c * FF_CHUNK + (j + 1) * MLP_COLS)
        hid = jnp.maximum(jnp.dot(xn_ref[...], w1_ref[:, cols], preferred_element_type=F32), 0.0)
        hid_ref[c % 2, :, j * MLP_COLS:(j + 1) * MLP_COLS] = (hid * hid).astype(BF16)

    def mlp_down(c, j):
        cols = slice(j * MLP_COLS, (j + 1) * MLP_COLS)
        o_ref[:, cols] += jnp.dot(hid_ref[c % 2], w2_ref[c * FF_CHUNK:(c + 1) * FF_CHUNK, cols],
                                  preferred_element_type=F32)

    mlp_pieces = []
    for c in range(D_FF // FF_CHUNK):
        mlp_pieces += [(mlp_up, c, j) for j in range(FF_CHUNK // MLP_COLS)]
        mlp_pieces += [(mlp_down, c, j) for j in range(D_MODEL // MLP_COLS)]

    kw_ref[0:SEQ_TILE, :] = kw_ref[SEQ_TILE:, :]
    kw_ref[SEQ_TILE:, :] = k_ref[...]
    vtw_ref[:, 0:SEQ_TILE] = vtw_ref[:, SEQ_TILE:]
    vtw_ref[:, SEQ_TILE:] = vt_ref[...]
    pen_lane = lax.broadcasted_iota(jnp.int32, (SEQ_TILE, LANES), 1) == 0
    pen_ref[0:SEQ_TILE, :] = jnp.where(pen_lane & first_tile, NEG_INF, 0.0).astype(BF16)
    pen_ref[SEQ_TILE:, :] = jnp.zeros((SEQ_TILE, LANES), BF16)

    first_head = lax.broadcasted_iota(jnp.int32, (LANES, Q_BLOCK), 0) < HEAD_DIM
    pen_rows = (lax.broadcasted_iota(jnp.int32, (LANES, PAIR_COLS), 0) == 0).astype(BF16)
    blocks = [(qb * Q_BLOCK, p) for qb in range(Q_BLOCKS) for p in range(HEAD_PAIRS)]

    def scores(n):
        k0, p = blocks[n]
        ch = slice(p * LANES, (p + 1) * LANES)
        qt = qt_ref[ch, k0:k0 + Q_BLOCK]
        zero = jnp.zeros_like(qt)
        rhs = jnp.concatenate([jnp.where(first_head, qt, zero),
                               jnp.where(first_head, zero, qt)], axis=1)
        rhs = jnp.concatenate([rhs, pen_rows], axis=0)
        lhs = jnp.concatenate([kw_ref[k0:k0 + KEY_SPAN, ch], pen_ref[k0:k0 + KEY_SPAN, :]],
                              axis=1)
        s_ref[n % 2] = jnp.dot(lhs, rhs, preferred_element_type=F32) + bias_ref[p]

    def attend(n):
        k0, p = blocks[n]
        ch = slice(p * LANES, (p + 1) * LANES)
        s = s_ref[n % 2]
        pr = jnp.exp2(s - jnp.max(s, axis=0, keepdims=True))
        denom = jnp.sum(pr, axis=0, keepdims=True)
        ot = jnp.dot(vtw_ref[ch, k0:k0 + KEY_SPAN], pr.astype(BF16),
                     preferred_element_type=F32) / denom
        yt_ref[p * LANES:p * LANES + HEAD_DIM, k0:k0 + Q_BLOCK] = ot[0:HEAD_DIM, 0:Q_BLOCK]
        yt_ref[p * LANES + HEAD_DIM:(p + 1) * LANES, k0:k0 + Q_BLOCK] = ot[HEAD_DIM:, Q_BLOCK:]

    per_block = 2
    scores(0)
    conv_half_out_proj(2)
    conv_half_out_proj(3)
    mlp_prologue()
    for n in range(len(blocks)):
        if n + 1 < len(blocks):
            scores(n + 1)
        attend(n)
        if n + 1 < len(blocks):
            for fn, c, j in mlp_pieces[n * per_block:(n + 1) * per_block]:
                fn(c, j)
    ynt = (_group_norm_channels(yt_ref[...]) * aog_ref[...]).astype(BF16)
    for fn, c, j in mlp_pieces[(len(blocks) - 1) * per_block:]:
        fn(c, j)
    h_ref[...] = hc_ref[...] + lax.dot_general(ynt, wo_ref[CONV_WIDTH:, :], TN_DIMS,
                                               preferred_element_type=F32)


def _attn_mlp(x, yc, qt, k, vt, bias, aog_col, w_out, g2, w1, w2):
    bsz, seq, _ = x.shape
    tiles_per_seq = seq // SEQ_TILE
    n_tiles = bsz * tiles_per_seq

    def tile_of(t):
        t = jnp.minimum(t, n_tiles - 1)
        return t // tiles_per_seq, t % tiles_per_seq

    def tok_map(t):
        b, i = tile_of(t)
        return b, i, 0

    def ch_map(t):
        b, i = tile_of(t)
        return b, 0, i

    def out_map(t):
        b, i = tile_of(jnp.maximum(t - 1, 0))
        return b, i, 0

    x_tile = pl.BlockSpec((None, SEQ_TILE, D_MODEL), tok_map)
    tok_tile = pl.BlockSpec((None, SEQ_TILE, ATTN_WIDTH), tok_map)
    ch_tile = pl.BlockSpec((None, ATTN_WIDTH, SEQ_TILE), ch_map)
    return pl.pallas_call(
        functools.partial(_attn_mlp_kernel, tiles_per_seq),
        grid=(n_tiles + 1,),
        in_specs=[x_tile, tok_tile, ch_tile, tok_tile, ch_tile,
                  _resident((HEAD_PAIRS, KEY_SPAN, PAIR_COLS)), _resident((ATTN_WIDTH, 1)),
                  _resident((D_MODEL, D_MODEL)), _resident((1, D_MODEL)),
                  _resident((D_MODEL, D_FF)), _resident((D_FF, D_MODEL))],
        out_specs=pl.BlockSpec((None, SEQ_TILE, D_MODEL), out_map),
        out_shape=jax.ShapeDtypeStruct(x.shape, F32),
        scratch_shapes=[pltpu.VMEM((2 * SEQ_TILE, ATTN_WIDTH), BF16),
                        pltpu.VMEM((ATTN_WIDTH, 2 * SEQ_TILE), BF16),
                        pltpu.VMEM((2 * SEQ_TILE, LANES), BF16),
                        pltpu.VMEM((ATTN_WIDTH, SEQ_TILE), F32),
                        pltpu.VMEM((2, KEY_SPAN, PAIR_COLS), F32),
                        pltpu.VMEM((SEQ_TILE, D_MODEL), F32),
                        pltpu.VMEM((SEQ_TILE, D_MODEL), F32),
                        pltpu.VMEM((SEQ_TILE, D_MODEL), BF16),
                        pltpu.VMEM((2, SEQ_TILE, FF_CHUNK), BF16)],
        compiler_params=pltpu.CompilerParams(
            dimension_semantics=("arbitrary",), vmem_limit_bytes=VMEM_LIMIT_FUSED),
        name="attn_mlp",
    )(x, yc, qt, k, vt, bias, aog_col, w_out, g2, w1, w2)


def kernel(x, norm_mix_g, w_in, conv_w, conv_b, q_norm_g, k_norm_g, rel_bias, conv_out_g,
           attn_out_g, w_out, norm_mlp_g, w_mlp_in, w_mlp_out):
    depth = w_in.shape[0]
    for l in range(depth):
        col_params = jnp.stack(
            [conv_w[l, 0], conv_w[l, 1], conv_w[l, 2], conv_b[l], conv_out_g[l],
             jnp.tile(q_norm_g[l], N_HEADS), jnp.tile(k_norm_g[l], N_HEADS),
             jnp.zeros((CONV_WIDTH,), F32)], axis=1)
        yc, qt, k, vt = _mix_proj(x, norm_mix_g[l][None], w_in[l].T.astype(BF16), col_params)
        x = _attn_mlp(x, yc, qt, k, vt, _bias_table(rel_bias[l]), attn_out_g[l][:, None],
                      w_out[l].astype(BF16), norm_mlp_g[l][None], w_mlp_in[l].astype(BF16),
                      w_mlp_out[l].astype(BF16))
    return x
```

```python
import functools
import math

import jax
import jax.numpy as jnp
from jax import lax
from jax.experimental import pallas as pl
from jax.experimental.pallas import tpu as pltpu

D_MODEL = 1024
CHUNK = 64
LEFT_CHUNKS = 8
BAND = LEFT_CHUNKS + 1
CONV_WIDTH = D_MODEL // 2
GROUP = 64
HEAD_DIM = 64
N_HEADS = (D_MODEL - CONV_WIDTH) // HEAD_DIM
ATTN_WIDTH = N_HEADS * HEAD_DIM
REL_CLIP = 128
REL_TABLE = (CHUNK - 1) + REL_CLIP + 1
D_FF = 4 * D_MODEL
EPS = 1e-6
NEG_INF = -1e30
LOG2_E = math.log2(math.e)

LANES = 128
SEQ_TILE = LEFT_CHUNKS * CHUNK
Q_BLOCK = 2 * CHUNK
Q_BLOCKS = SEQ_TILE // Q_BLOCK
KEY_SPAN = (BAND + 1) * CHUNK
HEAD_PAIRS = N_HEADS * HEAD_DIM // LANES
PAIR_COLS = 2 * Q_BLOCK
BIAS_FLAT_ROWS = KEY_SPAN - 2 * Q_BLOCK
BIAS_SLAB = 32
OUT_COL_CHUNK = D_MODEL // Q_BLOCKS
FF_CHUNK = 1024
MLP_COLS = 256
VMEM_LIMIT = 60000 * 1024

COL_CONV_W, COL_CONV_B, COL_CONV_G, COL_Q_G, COL_K_G, N_COL_PARAMS = 0, 3, 4, 5, 6, 8

F32 = jnp.float32
BF16 = jnp.bfloat16
NT_DIMS = (((1,), (1,)), ((), ()))
TN_DIMS = (((0,), (0,)), ((), ()))


def _resident(shape):
    return pl.BlockSpec(shape, lambda *_: (0,) * len(shape), pipeline_mode=pl.Buffered(1))


def _rms_norm_bf16(x, gain):
    ms = jnp.mean(x * x, axis=-1, keepdims=True)
    return (x * lax.rsqrt(ms + EPS) * gain).astype(BF16)


def _group_norm_channels(y):
    c, t = y.shape
    y3 = y.reshape(c // GROUP, GROUP, t)
    ms = jnp.mean(y3 * y3, axis=1, keepdims=True)
    return (y3 * lax.rsqrt(ms + EPS)).reshape(c, t)


def _bias_table_kernel(rel_ref, out_ref):
    def col_fields(shape):
        c = lax.broadcasted_iota(jnp.int32, shape, 1)
        return c < Q_BLOCK, (c // CHUNK) % 2, c % Q_BLOCK

    def in_band(r, j):
        return (r >= j * CHUNK) & (r < j * CHUNK + BAND * CHUNK)

    first, j, _ = col_fields((BIAS_FLAT_ROWS, PAIR_COLS))
    r = lax.broadcasted_iota(jnp.int32, (BIAS_FLAT_ROWS, PAIR_COLS), 0)
    for p in range(HEAD_PAIRS):
        far = jnp.where(first, rel_ref[2 * p, REL_TABLE - 1], rel_ref[2 * p + 1, REL_TABLE - 1])
        out_ref[p, 0:BIAS_FLAT_ROWS, :] = jnp.where(in_band(r, j), far * LOG2_E, NEG_INF)

    first, j, u = col_fields((BIAS_SLAB, PAIR_COLS))
    for r0 in range(BIAS_FLAT_ROWS, KEY_SPAN, BIAS_SLAB):
        r = lax.broadcasted_iota(jnp.int32, (BIAS_SLAB, PAIR_COLS), 0) + r0
        dist = LEFT_CHUNKS * CHUNK + u - r
        idx = jnp.clip(dist, -(CHUNK - 1), REL_CLIP) + (CHUNK - 1)
        lo = min(max(LEFT_CHUNKS * CHUNK - (r0 + BIAS_SLAB - 1), -(CHUNK - 1)), REL_CLIP) + CHUNK - 1
        hi = min(max(LEFT_CHUNKS * CHUNK + Q_BLOCK - 1 - r0, -(CHUNK - 1)), REL_CLIP) + CHUNK - 1

        def pick(t, accs):
            hit = idx == t
            return tuple(
                jnp.where(hit, jnp.where(first, rel_ref[2 * p, t], rel_ref[2 * p + 1, t]), acc)
                for p, acc in enumerate(accs))

        zero = jnp.zeros((BIAS_SLAB, PAIR_COLS), F32)
        accs = lax.fori_loop(lo, hi + 1, pick, (zero,) * HEAD_PAIRS)
        for p in range(HEAD_PAIRS):
            out_ref[p, r0:r0 + BIAS_SLAB, :] = jnp.where(in_band(r, j), accs[p] * LOG2_E, NEG_INF)


def _bias_table(rel_bias):
    return pl.pallas_call(
        _bias_table_kernel,
        in_specs=[pl.BlockSpec(memory_space=pltpu.SMEM)],
        out_specs=pl.BlockSpec(memory_space=pltpu.VMEM),
        out_shape=jax.ShapeDtypeStruct((HEAD_PAIRS, KEY_SPAN, PAIR_COLS), F32),
        name="rel_bias_table",
    )(rel_bias)


def _layer_kernel(tiles_per_seq, x_ref, g1_ref, wt_ref, colp_ref, bias_ref, aog_ref, wo_ref,
                  g2_ref, w1_ref, w2_ref, o_ref,
                  carry_ref, xn_ref, yc_ref, qt_ref, kw_ref, vtw_ref, pen_ref, yt_ref, s_ref,
                  hc_ref, ynt_ref, xn2_ref, hid_ref):
    t = pl.program_id(0)
    n_tiles = pl.num_programs(0) - 1
    first_tile = lax.rem(jnp.minimum(t, n_tiles - 1), tiles_per_seq) == 0

    @pl.when(t == 0)
    def _():
        kw_ref[...] = jnp.zeros_like(kw_ref)
        vtw_ref[...] = jnp.zeros_like(vtw_ref)
        hc_ref[...] = jnp.zeros_like(hc_ref)
        ynt_ref[...] = jnp.zeros_like(ynt_ref)

    @pl.when(first_tile)
    def _():
        carry_ref[...] = jnp.zeros_like(carry_ref)

    o_ref[...] = hc_ref[...] + lax.dot_general(ynt_ref[...], wo_ref[CONV_WIDTH:, :], TN_DIMS,
                                               preferred_element_type=F32)
    xn_ref[...] = _rms_norm_bf16(x_ref[...], g1_ref[...])

    def mlp_prologue():
        xn2_ref[...] = _rms_norm_bf16(o_ref[...], g2_ref[...])

    def mlp_up(c, j):
        cols = slice(c * FF_CHUNK + j * MLP_COLS, c * FF_CHUNK + (j + 1) * MLP_COLS)
        hid = jnp.maximum(jnp.dot(xn2_ref[...], w1_ref[:, cols], preferred_element_type=F32), 0.0)
        hid_ref[c % 2, :, j * MLP_COLS:(j + 1) * MLP_COLS] = (hid * hid).astype(BF16)

    def mlp_down(c, j):
        cols = slice(j * MLP_COLS, (j + 1) * MLP_COLS)
        o_ref[:, cols] += jnp.dot(hid_ref[c % 2], w2_ref[c * FF_CHUNK:(c + 1) * FF_CHUNK, cols],
                                  preferred_element_type=F32)

    mlp_pieces = []
    for c in range(D_FF // FF_CHUNK):
        mlp_pieces += [(mlp_up, c, j) for j in range(FF_CHUNK // MLP_COLS)]
        mlp_pieces += [(mlp_down, c, j) for j in range(D_MODEL // MLP_COLS)]

    def proj_t(j):
        return lax.dot_general(wt_ref[j * CONV_WIDTH:(j + 1) * CONV_WIDTH, :], xn_ref[...],
                               NT_DIMS, preferred_element_type=F32)

    def col(j):
        return colp_ref[:, j:j + 1]

    kw_ref[0:SEQ_TILE, :] = kw_ref[SEQ_TILE:, :]
    vtw_ref[:, 0:SEQ_TILE] = vtw_ref[:, SEQ_TILE:]
    pen_lane = lax.broadcasted_iota(jnp.int32, (SEQ_TILE, LANES), 1) == 0
    pen_ref[0:SEQ_TILE, :] = jnp.where(pen_lane & first_tile, NEG_INF, 0.0).astype(BF16)
    pen_ref[SEQ_TILE:, :] = jnp.zeros((SEQ_TILE, LANES), BF16)

    gate_c = proj_t(1)
    gate_h = proj_t(2)
    gate_b = proj_t(0)
    mlp_prologue()
    k_t = proj_t(4)
    u = gate_c * gate_h
    ext = jnp.concatenate([carry_ref[...], u], axis=1)
    u1 = pltpu.roll(ext, 1, 1)[:, LANES:]
    u2 = pltpu.roll(ext, 2, 1)[:, LANES:]
    carry_ref[...] = u[:, SEQ_TILE - LANES:]
    y = col(COL_CONV_W) * u2 + col(COL_CONV_W + 1) * u1 + col(COL_CONV_W + 2) * u
    yc = gate_b * (y + col(COL_CONV_B))
    yc_ref[...] = (_group_norm_channels(yc) * col(COL_CONV_G)).T.astype(BF16)
    q_t = proj_t(3)
    kw_ref[SEQ_TILE:, :] = (_group_norm_channels(k_t) * col(COL_K_G)).T.astype(BF16)
    v_t = proj_t(5)
    qt_ref[...] = (_group_norm_channels(q_t) * col(COL_Q_G)
                   * (HEAD_DIM ** -0.5 * LOG2_E)).astype(BF16)
    vtw_ref[:, SEQ_TILE:] = v_t.astype(BF16)

    first_head = lax.broadcasted_iota(jnp.int32, (LANES, Q_BLOCK), 0) < HEAD_DIM
    pen_rows = (lax.broadcasted_iota(jnp.int32, (LANES, PAIR_COLS), 0) == 0).astype(BF16)
    blocks = [(qb * Q_BLOCK, p) for qb in range(Q_BLOCKS) for p in range(HEAD_PAIRS)]

    def scores(n):
        k0, p = blocks[n]
        ch = slice(p * LANES, (p + 1) * LANES)
        qt = qt_ref[ch, k0:k0 + Q_BLOCK]
        zero = jnp.zeros_like(qt)
        rhs = jnp.concatenate([jnp.where(first_head, qt, zero),
                               jnp.where(first_head, zero, qt)], axis=1)
        rhs = jnp.concatenate([rhs, pen_rows], axis=0)
        lhs = jnp.concatenate([kw_ref[k0:k0 + KEY_SPAN, ch], pen_ref[k0:k0 + KEY_SPAN, :]],
                              axis=1)
        s_ref[n % 2] = jnp.dot(lhs, rhs, preferred_element_type=F32) + bias_ref[p]

    def attend(n):
        k0, p = blocks[n]
        ch = slice(p * LANES, (p + 1) * LANES)
        s = s_ref[n % 2]
        pr = jnp.exp2(s - jnp.max(s, axis=0, keepdims=True))
        denom = jnp.sum(pr, axis=0, keepdims=True)
        ot = jnp.dot(vtw_ref[ch, k0:k0 + KEY_SPAN], pr.astype(BF16),
                     preferred_element_type=F32) / denom
        yt_ref[p * LANES:p * LANES + HEAD_DIM, k0:k0 + Q_BLOCK] = ot[0:HEAD_DIM, 0:Q_BLOCK]
        yt_ref[p * LANES + HEAD_DIM:(p + 1) * LANES, k0:k0 + Q_BLOCK] = ot[HEAD_DIM:, Q_BLOCK:]

    def conv_half_out_proj(c):
        cols = slice(c * OUT_COL_CHUNK, (c + 1) * OUT_COL_CHUNK)
        hc_ref[:, cols] = x_ref[:, cols] + jnp.dot(yc_ref[...], wo_ref[0:CONV_WIDTH, cols],
                                                   preferred_element_type=F32)

    per_block = 2
    scores(0)
    for n in range(len(blocks)):
        if n + 1 < len(blocks):
            scores(n + 1)
        attend(n)
        if n < Q_BLOCKS:
            conv_half_out_proj(n)
        if n + 1 < len(blocks):
            for fn, c, j in mlp_pieces[n * per_block:(n + 1) * per_block]:
                fn(c, j)
    ynt = (_group_norm_channels(yt_ref[...]) * aog_ref[...]).astype(BF16)
    for fn, c, j in mlp_pieces[(len(blocks) - 1) * per_block:]:
        fn(c, j)
    ynt_ref[...] = ynt


def _layer(x, g1, w_in_t, col_params, bias, aog_col, w_out, g2, w1, w2):
    bsz, seq, _ = x.shape
    tiles_per_seq = seq // SEQ_TILE
    n_tiles = bsz * tiles_per_seq

    def tile_of(t):
        t = jnp.minimum(t, n_tiles - 1)
        return t // tiles_per_seq, t % tiles_per_seq, 0

    def out_tile_of(t):
        return tile_of(jnp.maximum(t - 1, 0))

    return pl.pallas_call(
        functools.partial(_layer_kernel, tiles_per_seq),
        grid=(n_tiles + 1,),
        in_specs=[pl.BlockSpec((None, SEQ_TILE, D_MODEL), tile_of),
                  _resident((1, D_MODEL)), _resident(w_in_t.shape),
                  _resident((CONV_WIDTH, N_COL_PARAMS)),
                  _resident((HEAD_PAIRS, KEY_SPAN, PAIR_COLS)), _resident((ATTN_WIDTH, 1)),
                  _resident((D_MODEL, D_MODEL)), _resident((1, D_MODEL)),
                  _resident((D_MODEL, D_FF)), _resident((D_FF, D_MODEL))],
        out_specs=pl.BlockSpec((None, SEQ_TILE, D_MODEL), out_tile_of),
        out_shape=jax.ShapeDtypeStruct(x.shape, F32),
        scratch_shapes=[pltpu.VMEM((CONV_WIDTH, LANES), F32),
                        pltpu.VMEM((SEQ_TILE, D_MODEL), BF16),
                        pltpu.VMEM((SEQ_TILE, CONV_WIDTH), BF16),
                        pltpu.VMEM((ATTN_WIDTH, SEQ_TILE), BF16),
                        pltpu.VMEM((2 * SEQ_TILE, ATTN_WIDTH), BF16),
                        pltpu.VMEM((ATTN_WIDTH, 2 * SEQ_TILE), BF16),
                        pltpu.VMEM((2 * SEQ_TILE, LANES), BF16),
                        pltpu.VMEM((ATTN_WIDTH, SEQ_TILE), F32),
                        pltpu.VMEM((2, KEY_SPAN, PAIR_COLS), F32),
                        pltpu.VMEM((SEQ_TILE, D_MODEL), F32),
                        pltpu.VMEM((ATTN_WIDTH, SEQ_TILE), BF16),
                        pltpu.VMEM((SEQ_TILE, D_MODEL), BF16),
                        pltpu.VMEM((2, SEQ_TILE, FF_CHUNK), BF16)],
        compiler_params=pltpu.CompilerParams(
            dimension_semantics=("arbitrary",), vmem_limit_bytes=VMEM_LIMIT),
        name="layer",
    )(x, g1, w_in_t, col_params, bias, aog_col, w_out, g2, w1, w2)


def kernel(x, norm_mix_g, w_in, conv_w, conv_b, q_norm_g, k_norm_g, rel_bias, conv_out_g,
           attn_out_g, w_out, norm_mlp_g, w_mlp_in, w_mlp_out):
    depth = w_in.shape[0]
    for l in range(depth):
        col_params = jnp.stack(
            [conv_w[l, 0], conv_w[l, 1], conv_w[l, 2], conv_b[l], conv_out_g[l],
             jnp.tile(q_norm_g[l], N_HEADS), jnp.tile(k_norm_g[l], N_HEADS),
             jnp.zeros((CONV_WIDTH,), F32)], axis=1)
        x = _layer(x, norm_mix_g[l][None], w_in[l].T.astype(BF16), col_params,
                   _bias_table(rel_bias[l]), attn_out_g[l][:, None], w_out[l].astype(BF16),
                   norm_mlp_g[l][None], w_mlp_in[l].astype(BF16), w_mlp_out[l].astype(BF16))
    return x
```

```python
import functools
import math

import jax
import jax.numpy as jnp
from jax import lax
from jax.experimental import pallas as pl
from jax.experimental.pallas import tpu as pltpu

D_MODEL = 1024
CHUNK = 64
LEFT_CHUNKS = 8
BAND = LEFT_CHUNKS + 1
CONV_WIDTH = D_MODEL // 2
GROUP = 64
HEAD_DIM = 64
N_HEADS = (D_MODEL - CONV_WIDTH) // HEAD_DIM
ATTN_WIDTH = N_HEADS * HEAD_DIM
REL_CLIP = 128
REL_TABLE = (CHUNK - 1) + REL_CLIP + 1
D_FF = 4 * D_MODEL
EPS = 1e-6
NEG_INF = -1e30
LOG2_E = math.log2(math.e)

LANES = 128
SEQ_TILE = LEFT_CHUNKS * CHUNK
Q_BLOCK = 2 * CHUNK
Q_BLOCKS = SEQ_TILE // Q_BLOCK
KEY_SPAN = (BAND + 1) * CHUNK
HEAD_PAIRS = N_HEADS * HEAD_DIM // LANES
PAIR_COLS = 2 * Q_BLOCK
BIAS_FLAT_ROWS = KEY_SPAN - 2 * Q_BLOCK
BIAS_SLAB = 32
OUT_COL_CHUNK = D_MODEL // Q_BLOCKS
FF_CHUNK = 1024
MLP_COLS = 256
WEIGHT_CHUNK = (512, 1024)
VMEM_LIMIT = 60000 * 1024

COL_CONV_W, COL_CONV_B, COL_CONV_G, COL_Q_G, COL_K_G, N_COL_PARAMS = 0, 3, 4, 5, 6, 8

F32 = jnp.float32
BF16 = jnp.bfloat16
NT_DIMS = (((1,), (1,)), ((), ()))
TN_DIMS = (((0,), (0,)), ((), ()))


def _resident(shape):
    return pl.BlockSpec(shape, lambda *_: (0,) * len(shape), pipeline_mode=pl.Buffered(1))


def _rms_norm_bf16(x, gain):
    ms = jnp.mean(x * x, axis=-1, keepdims=True)
    return (x * lax.rsqrt(ms + EPS) * gain).astype(BF16)


def _group_norm_channels(y):
    c, t = y.shape
    y3 = y.reshape(c // GROUP, GROUP, t)
    ms = jnp.mean(y3 * y3, axis=1, keepdims=True)
    return (y3 * lax.rsqrt(ms + EPS)).reshape(c, t)


def _bias_table_kernel(rel_ref, out_ref):
    def col_fields(shape):
        c = lax.broadcasted_iota(jnp.int32, shape, 1)
        return c < Q_BLOCK, (c // CHUNK) % 2, c % Q_BLOCK

    def in_band(r, j):
        return (r >= j * CHUNK) & (r < j * CHUNK + BAND * CHUNK)

    first, j, _ = col_fields((BIAS_FLAT_ROWS, PAIR_COLS))
    r = lax.broadcasted_iota(jnp.int32, (BIAS_FLAT_ROWS, PAIR_COLS), 0)
    for p in range(HEAD_PAIRS):
        far = jnp.where(first, rel_ref[2 * p, REL_TABLE - 1], rel_ref[2 * p + 1, REL_TABLE - 1])
        out_ref[p, 0:BIAS_FLAT_ROWS, :] = jnp.where(in_band(r, j), far * LOG2_E, NEG_INF)

    first, j, u = col_fields((BIAS_SLAB, PAIR_COLS))
    for r0 in range(BIAS_FLAT_ROWS, KEY_SPAN, BIAS_SLAB):
        r = lax.broadcasted_iota(jnp.int32, (BIAS_SLAB, PAIR_COLS), 0) + r0
        dist = LEFT_CHUNKS * CHUNK + u - r
        idx = jnp.clip(dist, -(CHUNK - 1), REL_CLIP) + (CHUNK - 1)
        lo = min(max(LEFT_CHUNKS * CHUNK - (r0 + BIAS_SLAB - 1), -(CHUNK - 1)), REL_CLIP) + CHUNK - 1
        hi = min(max(LEFT_CHUNKS * CHUNK + Q_BLOCK - 1 - r0, -(CHUNK - 1)), REL_CLIP) + CHUNK - 1

        def pick(t, accs):
            hit = idx == t
            return tuple(
                jnp.where(hit, jnp.where(first, rel_ref[2 * p, t], rel_ref[2 * p + 1, t]), acc)
                for p, acc in enumerate(accs))

        zero = jnp.zeros((BIAS_SLAB, PAIR_COLS), F32)
        accs = lax.fori_loop(lo, hi + 1, pick, (zero,) * HEAD_PAIRS)
        for p in range(HEAD_PAIRS):
            out_ref[p, r0:r0 + BIAS_SLAB, :] = jnp.where(in_band(r, j), accs[p] * LOG2_E, NEG_INF)


def _bias_table(rel_bias):
    return pl.pallas_call(
        _bias_table_kernel,
        in_specs=[pl.BlockSpec(memory_space=pltpu.SMEM)],
        out_specs=pl.BlockSpec(memory_space=pltpu.VMEM),
        out_shape=jax.ShapeDtypeStruct((HEAD_PAIRS, KEY_SPAN, PAIR_COLS), F32),
        name="rel_bias_table",
    )(rel_bias)


def _load_weights_bf16(w_in_hbm, w_out_hbm, w1_hbm, w2_hbm, wt_ref, wo_ref, w1_ref, w2_ref,
                       stage_ref, sem):
    rows, cols = stage_ref.shape[1:]
    chunks = []

    def add(src, dst, transposed=False):
        for r0 in range(0, src.shape[0], rows):
            for c0 in range(0, src.shape[1], cols):
                if transposed:
                    def store(v, r0=r0, c0=c0):
                        dst[c0:c0 + cols, r0:r0 + rows] = v.T.astype(BF16)
                else:
                    def store(v, r0=r0, c0=c0):
                        dst[r0:r0 + rows, c0:c0 + cols] = v.astype(BF16)
                chunks.append((src.at[r0:r0 + rows, c0:c0 + cols], store))

    add(w_in_hbm, wt_ref, transposed=True)
    add(w_out_hbm, wo_ref)
    add(w1_hbm, w1_ref)
    add(w2_hbm, w2_ref)

    def copy(n):
        return pltpu.make_async_copy(chunks[n][0], stage_ref.at[n % 2], sem.at[n % 2])

    copy(0).start()
    for n in range(len(chunks)):
        if n + 1 < len(chunks):
            copy(n + 1).start()
        copy(n).wait()
        chunks[n][1](stage_ref[n % 2])


def _layer_kernel(tiles_per_seq, x_ref, g1_ref, colp_ref, bias_ref, aog_ref, g2_ref,
                  w_in_hbm, w_out_hbm, w1_hbm, w2_hbm, o_ref,
                  wt_ref, wo_ref, w1_ref, w2_ref, stage_ref, sem,
                  carry_ref, xn_ref, yc_ref, qt_ref, kw_ref, vtw_ref, pen_ref, yt_ref, s_ref,
                  hc_ref, ynt_ref, xn2_ref, hid_ref):
    t = pl.program_id(0)
    n_tiles = pl.num_programs(0) - 1
    first_tile = lax.rem(jnp.minimum(t, n_tiles - 1), tiles_per_seq) == 0

    @pl.when(t == 0)
    def _():
        _load_weights_bf16(w_in_hbm, w_out_hbm, w1_hbm, w2_hbm, wt_ref, wo_ref, w1_ref, w2_ref,
                           stage_ref, sem)
        kw_ref[...] = jnp.zeros_like(kw_ref)
        vtw_ref[...] = jnp.zeros_like(vtw_ref)
        hc_ref[...] = jnp.zeros_like(hc_ref)
        ynt_ref[...] = jnp.zeros_like(ynt_ref)

    @pl.when(first_tile)
    def _():
        carry_ref[...] = jnp.zeros_like(carry_ref)

    o_ref[...] = hc_ref[...] + lax.dot_general(ynt_ref[...], wo_ref[CONV_WIDTH:, :], TN_DIMS,
                                               preferred_element_type=F32)
    xn_ref[...] = _rms_norm_bf16(x_ref[...], g1_ref[...])

    def mlp_prologue():
        xn2_ref[...] = _rms_norm_bf16(o_ref[...], g2_ref[...])

    def mlp_up(c, j):
        cols = slice(c * FF_CHUNK + j * MLP_COLS, c * FF_CHUNK + (j + 1) * MLP_COLS)
        hid = jnp.maximum(jnp.dot(xn2_ref[...], w1_ref[:, cols], preferred_element_type=F32), 0.0)
        hid_ref[c % 2, :, j * MLP_COLS:(j + 1) * MLP_COLS] = (hid * hid).astype(BF16)

    def mlp_down(c, j):
        cols = slice(j * MLP_COLS, (j + 1) * MLP_COLS)
        o_ref[:, cols] += jnp.dot(hid_ref[c % 2], w2_ref[c * FF_CHUNK:(c + 1) * FF_CHUNK, cols],
                                  preferred_element_type=F32)

    mlp_pieces = []
    for c in range(D_FF // FF_CHUNK):
        mlp_pieces += [(mlp_up, c, j) for j in range(FF_CHUNK // MLP_COLS)]
        mlp_pieces += [(mlp_down, c, j) for j in range(D_MODEL // MLP_COLS)]

    def proj_t(j):
        return lax.dot_general(wt_ref[j * CONV_WIDTH:(j + 1) * CONV_WIDTH, :], xn_ref[...],
                               NT_DIMS, preferred_element_type=F32)

    def col(j):
        return colp_ref[:, j:j + 1]

    kw_ref[0:SEQ_TILE, :] = kw_ref[SEQ_TILE:, :]
    vtw_ref[:, 0:SEQ_TILE] = vtw_ref[:, SEQ_TILE:]
    pen_lane = lax.broadcasted_iota(jnp.int32, (SEQ_TILE, LANES), 1) == 0
    pen_ref[0:SEQ_TILE, :] = jnp.where(pen_lane & first_tile, NEG_INF, 0.0).astype(BF16)
    pen_ref[SEQ_TILE:, :] = jnp.zeros((SEQ_TILE, LANES), BF16)

    gate_c = proj_t(1)
    gate_h = proj_t(2)
    gate_b = proj_t(0)
    mlp_prologue()
    k_t = proj_t(4)
    u = gate_c * gate_h
    ext = jnp.concatenate([carry_ref[...], u], axis=1)
    u1 = pltpu.roll(ext, 1, 1)[:, LANES:]
    u2 = pltpu.roll(ext, 2, 1)[:, LANES:]
    carry_ref[...] = u[:, SEQ_TILE - LANES:]
    y = col(COL_CONV_W) * u2 + col(COL_CONV_W + 1) * u1 + col(COL_CONV_W + 2) * u
    yc = gate_b * (y + col(COL_CONV_B))
    yc_ref[...] = (_group_norm_channels(yc) * col(COL_CONV_G)).T.astype(BF16)
    q_t = proj_t(3)
    kw_ref[SEQ_TILE:, :] = (_group_norm_channels(k_t) * col(COL_K_G)).T.astype(BF16)
    v_t = proj_t(5)
    qt_ref[...] = (_group_norm_channels(q_t) * col(COL_Q_G)
                   * (HEAD_DIM ** -0.5 * LOG2_E)).astype(BF16)
    vtw_ref[:, SEQ_TILE:] = v_t.astype(BF16)

    first_head = lax.broadcasted_iota(jnp.int32, (LANES, Q_BLOCK), 0) < HEAD_DIM
    pen_rows = (lax.broadcasted_iota(jnp.int32, (LANES, PAIR_COLS), 0) == 0).astype(BF16)
    blocks = [(qb * Q_BLOCK, p) for qb in range(Q_BLOCKS) for p in range(HEAD_PAIRS)]

    def scores(n):
        k0, p = blocks[n]
        ch = slice(p * LANES, (p + 1) * LANES)
        qt = qt_ref[ch, k0:k0 + Q_BLOCK]
        zero = jnp.zeros_like(qt)
        rhs = jnp.concatenate([jnp.where(first_head, qt, zero),
                               jnp.where(first_head, zero, qt)], axis=1)
        rhs = jnp.concatenate([rhs, pen_rows], axis=0)
        lhs = jnp.concatenate([kw_ref[k0:k0 + KEY_SPAN, ch], pen_ref[k0:k0 + KEY_SPAN, :]],
                              axis=1)
        s_ref[n % 2] = jnp.dot(lhs, rhs, preferred_element_type=F32) + bias_ref[p]

    def attend(n):
        k0, p = blocks[n]
        ch = slice(p * LANES, (p + 1) * LANES)
        s = s_ref[n % 2]
        pr = jnp.exp2(s - jnp.max(s, axis=0, keepdims=True))
        denom = jnp.sum(pr, axis=0, keepdims=True)
        ot = jnp.dot(vtw_ref[ch, k0:k0 + KEY_SPAN], pr.astype(BF16),
                     preferred_element_type=F32) / denom
        yt_ref[p * LANES:p * LANES + HEAD_DIM, k0:k0 + Q_BLOCK] = ot[0:HEAD_DIM, 0:Q_BLOCK]
        yt_ref[p * LANES + HEAD_DIM:(p + 1) * LANES, k0:k0 + Q_BLOCK] = ot[HEAD_DIM:, Q_BLOCK:]

    def conv_half_out_proj(c):
        cols = slice(c * OUT_COL_CHUNK, (c + 1) * OUT_COL_CHUNK)
        hc_ref[:, cols] = x_ref[:, cols] + jnp.dot(yc_ref[...], wo_ref[0:CONV_WIDTH, cols],
                                                   preferred_element_type=F32)

    per_block = 2
    scores(0)
    for n in range(len(blocks)):
        if n + 1 < len(blocks):
            scores(n + 1)
        attend(n)
        if n < Q_BLOCKS:
            conv_half_out_proj(n)
        if n + 1 < len(blocks):
            for fn, c, j in mlp_pieces[n * per_block:(n + 1) * per_block]:
                fn(c, j)
    ynt = (_group_norm_channels(yt_ref[...]) * aog_ref[...]).astype(BF16)
    for fn, c, j in mlp_pieces[(len(blocks) - 1) * per_block:]:
        fn(c, j)
    ynt_ref[...] = ynt


def _layer(x, g1, col_params, bias, aog_col, g2, w_in, w_out, w1, w2):
    bsz, seq, _ = x.shape
    in_hbm = pl.BlockSpec(memory_space=pl.ANY)
    tiles_per_seq = seq // SEQ_TILE
    n_tiles = bsz * tiles_per_seq

    def tile_of(t):
        t = jnp.minimum(t, n_tiles - 1)
        return t // tiles_per_seq, t % tiles_per_seq, 0

    def out_tile_of(t):
        return tile_of(jnp.maximum(t - 1, 0))

    return pl.pallas_call(
        functools.partial(_layer_kernel, tiles_per_seq),
        grid=(n_tiles + 1,),
        in_specs=[pl.BlockSpec((None, SEQ_TILE, D_MODEL), tile_of),
                  _resident((1, D_MODEL)), _resident((CONV_WIDTH, N_COL_PARAMS)),
                  _resident((HEAD_PAIRS, KEY_SPAN, PAIR_COLS)), _resident((ATTN_WIDTH, 1)),
                  _resident((1, D_MODEL)), in_hbm, in_hbm, in_hbm, in_hbm],
        out_specs=pl.BlockSpec((None, SEQ_TILE, D_MODEL), out_tile_of),
        out_shape=jax.ShapeDtypeStruct(x.shape, F32),
        scratch_shapes=[pltpu.VMEM(w_in.shape[::-1], BF16),
                        pltpu.VMEM(w_out.shape, BF16),
                        pltpu.VMEM(w1.shape, BF16),
                        pltpu.VMEM(w2.shape, BF16),
                        pltpu.VMEM((2,) + WEIGHT_CHUNK, F32),
                        pltpu.SemaphoreType.DMA((2,)),
                        pltpu.VMEM((CONV_WIDTH, LANES), F32),
                        pltpu.VMEM((SEQ_TILE, D_MODEL), BF16),
                        pltpu.VMEM((SEQ_TILE, CONV_WIDTH), BF16),
                        pltpu.VMEM((ATTN_WIDTH, SEQ_TILE), BF16),
                        pltpu.VMEM((2 * SEQ_TILE, ATTN_WIDTH), BF16),
                        pltpu.VMEM((ATTN_WIDTH, 2 * SEQ_TILE), BF16),
                        pltpu.VMEM((2 * SEQ_TILE, LANES), BF16),
                        pltpu.VMEM((ATTN_WIDTH, SEQ_TILE), F32),
                        pltpu.VMEM((2, KEY_SPAN, PAIR_COLS), F32),
                        pltpu.VMEM((SEQ_TILE, D_MODEL), F32),
                        pltpu.VMEM((ATTN_WIDTH, SEQ_TILE), BF16),
                        pltpu.VMEM((SEQ_TILE, D_MODEL), BF16),
                        pltpu.VMEM((2, SEQ_TILE, FF_CHUNK), BF16)],
        compiler_params=pltpu.CompilerParams(
            dimension_semantics=("arbitrary",), vmem_limit_bytes=VMEM_LIMIT),
        name="layer",
    )(x, g1, col_params, bias, aog_col, g2, w_in, w_out, w1, w2)


def kernel(x, norm_mix_g, w_in, conv_w, conv_b, q_norm_g, k_norm_g, rel_bias, conv_out_g,
           attn_out_g, w_out, norm_mlp_g, w_mlp_in, w_mlp_out):
    depth = w_in.shape[0]
    for l in range(depth):
        col_params = jnp.stack(
            [conv_w[l, 0], conv_w[l, 1], conv_w[l, 2], conv_b[l], conv_out_g[l],
             jnp.tile(q_norm_g[l], N_HEADS), jnp.tile(k_norm_g[l], N_HEADS),
             jnp.zeros((CONV_WIDTH,), F32)], axis=1)
        x = _layer(x, norm_mix_g[l][None], col_params, _bias_table(rel_bias[l]),
                   attn_out_g[l][:, None], norm_mlp_g[l][None],
                   w_in[l], w_out[l], w_mlp_in[l], w_mlp_out[l])
    return x
```

```python
import functools
import math

import jax
import jax.numpy as jnp
from jax import lax
from jax.experimental import pallas as pl
from jax.experimental.pallas import tpu as pltpu

D_MODEL = 1024
CHUNK = 64
LEFT_CHUNKS = 8
BAND = LEFT_CHUNKS + 1
CONV_WIDTH = D_MODEL // 2
GROUP = 64
HEAD_DIM = 64
N_HEADS = (D_MODEL - CONV_WIDTH) // HEAD_DIM
ATTN_WIDTH = N_HEADS * HEAD_DIM
REL_CLIP = 128
REL_TABLE = (CHUNK - 1) + REL_CLIP + 1
D_FF = 4 * D_MODEL
EPS = 1e-6
NEG_INF = -1e30
LOG2_E = math.log2(math.e)

LANES = 128
SEQ_TILE = LEFT_CHUNKS * CHUNK
Q_BLOCK = 2 * CHUNK
Q_BLOCKS = SEQ_TILE // Q_BLOCK
KEY_SPAN = (BAND + 1) * CHUNK
HEAD_PAIRS = N_HEADS * HEAD_DIM // LANES
PAIR_COLS = 2 * Q_BLOCK
BIAS_FLAT_ROWS = KEY_SPAN - 2 * Q_BLOCK
BIAS_RING = 4 * LANES
OUT_COL_CHUNK = D_MODEL // Q_BLOCKS
FF_CHUNK = 1024
MLP_COLS = 256
WEIGHT_CHUNK = (512, 1024)
VMEM_LIMIT = 60000 * 1024

COL_CONV_W, COL_CONV_B, COL_CONV_G, COL_Q_G, COL_K_G, N_COL_PARAMS = 0, 3, 4, 5, 6, 8

F32 = jnp.float32
BF16 = jnp.bfloat16
NT_DIMS = (((1,), (1,)), ((), ()))
TN_DIMS = (((0,), (0,)), ((), ()))


def _resident(shape):
    return pl.BlockSpec(shape, lambda *_: (0,) * len(shape), pipeline_mode=pl.Buffered(1))


def _rms_norm_bf16(x, gain):
    ms = jnp.mean(x * x, axis=-1, keepdims=True)
    return (x * lax.rsqrt(ms + EPS) * gain).astype(BF16)


def _group_norm_channels(y):
    c, t = y.shape
    y3 = y.reshape(c // GROUP, GROUP, t)
    ms = jnp.mean(y3 * y3, axis=1, keepdims=True)
    return (y3 * lax.rsqrt(ms + EPS)).reshape(c, t)


def _bias_table_kernel(ring_ref, out_ref):
    def fields(rows, r0):
        r = lax.broadcasted_iota(jnp.int32, (rows, PAIR_COLS), 0) + r0
        c = lax.broadcasted_iota(jnp.int32, (rows, PAIR_COLS), 1)
        j = (c // CHUNK) % 2
        return c < Q_BLOCK, (r >= j * CHUNK) & (r < j * CHUNK + BAND * CHUNK)

    near_rows = KEY_SPAN - BIAS_FLAT_ROWS
    first_far, in_band_far = fields(BIAS_FLAT_ROWS, 0)
    _, in_band_near = fields(near_rows, BIAS_FLAT_ROWS)
    for p in range(HEAD_PAIRS):
        far = jnp.where(first_far, ring_ref[2 * p:2 * p + 1, 0:1], ring_ref[2 * p + 1:2 * p + 2, 0:1])
        out_ref[p, 0:BIAS_FLAT_ROWS, :] = jnp.where(in_band_far, far * LOG2_E, NEG_INF)
        halves = []
        for e in range(2):
            ring = jnp.broadcast_to(ring_ref[2 * p + e:2 * p + e + 1, :], (near_rows, BIAS_RING))
            halves.append(pltpu.roll(ring, 0, 1, stride=1, stride_axis=0)[:, 0:Q_BLOCK])
        near = jnp.concatenate(halves, axis=1)
        out_ref[p, BIAS_FLAT_ROWS:, :] = jnp.where(in_band_near, near * LOG2_E, NEG_INF)


def _bias_ring(rel_bias):
    far = jnp.broadcast_to(rel_bias[:, -1:], (N_HEADS, BIAS_RING - 2 * Q_BLOCK + 1))
    past = jnp.broadcast_to(rel_bias[:, :1], (N_HEADS, CHUNK))
    return jnp.concatenate([far, past, rel_bias[:, :REL_TABLE - 1]], axis=1)


def _bias_table(rel_bias):
    return pl.pallas_call(
        _bias_table_kernel,
        in_specs=[pl.BlockSpec(memory_space=pltpu.VMEM)],
        out_specs=pl.BlockSpec(memory_space=pltpu.VMEM),
        out_shape=jax.ShapeDtypeStruct((HEAD_PAIRS, KEY_SPAN, PAIR_COLS), F32),
        name="rel_bias_table",
    )(_bias_ring(rel_bias))


def _load_weights_bf16(w_in_hbm, w_out_hbm, w1_hbm, w2_hbm, wt_ref, wo_ref, w1_ref, w2_ref,
                       stage_ref, sem):
    rows, cols = stage_ref.shape[1:]
    chunks = []

    def add(src, dst, transposed=False):
        for r0 in range(0, src.shape[0], rows):
            for c0 in range(0, src.shape[1], cols):
                if transposed:
                    def store(v, r0=r0, c0=c0):
                        dst[c0:c0 + cols, r0:r0 + rows] = v.T.astype(BF16)
                else:
                    def store(v, r0=r0, c0=c0):
                        dst[r0:r0 + rows, c0:c0 + cols] = v.astype(BF16)
                chunks.append((src.at[r0:r0 + rows, c0:c0 + cols], store))

    add(w_in_hbm, wt_ref, transposed=True)
    add(w_out_hbm, wo_ref)
    add(w1_hbm, w1_ref)
    add(w2_hbm, w2_ref)

    def copy(n):
        return pltpu.make_async_copy(chunks[n][0], stage_ref.at[n % 2], sem.at[n % 2])

    copy(0).start()
    for n in range(len(chunks)):
        if n + 1 < len(chunks):
            copy(n + 1).start()
        copy(n).wait()
        chunks[n][1](stage_ref[n % 2])


def _layer_kernel(tiles_per_seq, x_ref, g1_ref, colp_ref, bias_ref, aog_ref, g2_ref,
                  w_in_hbm, w_out_hbm, w1_hbm, w2_hbm, o_ref,
                  wt_ref, wo_ref, w1_ref, w2_ref, stage_ref, sem,
                  carry_ref, xn_ref, yc_ref, qt_ref, kw_ref, vtw_ref, pen_ref, yt_ref, s_ref,
                  hc_ref, ynt_ref, xn2_ref, hid_ref):
    t = pl.program_id(0)
    n_tiles = pl.num_programs(0) - 1
    first_tile = lax.rem(jnp.minimum(t, n_tiles - 1), tiles_per_seq) == 0

    @pl.when(t == 0)
    def _():
        _load_weights_bf16(w_in_hbm, w_out_hbm, w1_hbm, w2_hbm, wt_ref, wo_ref, w1_ref, w2_ref,
                           stage_ref, sem)
        kw_ref[...] = jnp.zeros_like(kw_ref)
        vtw_ref[...] = jnp.zeros_like(vtw_ref)
        hc_ref[...] = jnp.zeros_like(hc_ref)
        ynt_ref[...] = jnp.zeros_like(ynt_ref)

    @pl.when(first_tile)
    def _():
        carry_ref[...] = jnp.zeros_like(carry_ref)

    o_ref[...] = hc_ref[...] + lax.dot_general(ynt_ref[...], wo_ref[CONV_WIDTH:, :], TN_DIMS,
                                               preferred_element_type=F32)
    xn_ref[...] = _rms_norm_bf16(x_ref[...], g1_ref[...])

    def mlp_prologue():
        xn2_ref[...] = _rms_norm_bf16(o_ref[...], g2_ref[...])

    def mlp_up(c, j):
        cols = slice(c * FF_CHUNK + j * MLP_COLS, c * FF_CHUNK + (j + 1) * MLP_COLS)
        hid = jnp.maximum(jnp.dot(xn2_ref[...], w1_ref[:, cols], preferred_element_type=F32), 0.0)
        hid_ref[c % 2, :, j * MLP_COLS:(j + 1) * MLP_COLS] = (hid * hid).astype(BF16)

    def mlp_down(c, j):
        cols = slice(j * MLP_COLS, (j + 1) * MLP_COLS)
        o_ref[:, cols] += jnp.dot(hid_ref[c % 2], w2_ref[c * FF_CHUNK:(c + 1) * FF_CHUNK, cols],
                                  preferred_element_type=F32)

    mlp_pieces = []
    for c in range(D_FF // FF_CHUNK):
        mlp_pieces += [(mlp_up, c, j) for j in range(FF_CHUNK // MLP_COLS)]
        mlp_pieces += [(mlp_down, c, j) for j in range(D_MODEL // MLP_COLS)]

    def proj_t(j):
        return lax.dot_general(wt_ref[j * CONV_WIDTH:(j + 1) * CONV_WIDTH, :], xn_ref[...],
                               NT_DIMS, preferred_element_type=F32)

    def col(j):
        return colp_ref[:, j:j + 1]

    kw_ref[0:SEQ_TILE, :] = kw_ref[SEQ_TILE:, :]
    vtw_ref[:, 0:SEQ_TILE] = vtw_ref[:, SEQ_TILE:]
    pen_lane = lax.broadcasted_iota(jnp.int32, (SEQ_TILE, LANES), 1) == 0
    pen_ref[0:SEQ_TILE, :] = jnp.where(pen_lane & first_tile, NEG_INF, 0.0).astype(BF16)
    pen_ref[SEQ_TILE:, :] = jnp.zeros((SEQ_TILE, LANES), BF16)

    gate_c = proj_t(1)
    gate_h = proj_t(2)
    gate_b = proj_t(0)
    mlp_prologue()
    k_t = proj_t(4)
    u = gate_c * gate_h
    ext = jnp.concatenate([carry_ref[...], u], axis=1)
    u1 = pltpu.roll(ext, 1, 1)[:, LANES:]
    u2 = pltpu.roll(ext, 2, 1)[:, LANES:]
    carry_ref[...] = u[:, SEQ_TILE - LANES:]
    y = col(COL_CONV_W) * u2 + col(COL_CONV_W + 1) * u1 + col(COL_CONV_W + 2) * u
    yc = gate_b * (y + col(COL_CONV_B))
    yc_ref[...] = (_group_norm_channels(yc) * col(COL_CONV_G)).T.astype(BF16)
    q_t = proj_t(3)
    kw_ref[SEQ_TILE:, :] = (_group_norm_channels(k_t) * col(COL_K_G)).T.astype(BF16)
    v_t = proj_t(5)
    qt_ref[...] = (_group_norm_channels(q_t) * col(COL_Q_G)
                   * (HEAD_DIM ** -0.5 * LOG2_E)).astype(BF16)
    vtw_ref[:, SEQ_TILE:] = v_t.astype(BF16)

    first_head = lax.broadcasted_iota(jnp.int32, (LANES, Q_BLOCK), 0) < HEAD_DIM
    pen_rows = (lax.broadcasted_iota(jnp.int32, (LANES, PAIR_COLS), 0) == 0).astype(BF16)
    blocks = [(qb * Q_BLOCK, p) for qb in range(Q_BLOCKS) for p in range(HEAD_PAIRS)]

    def scores(n):
        k0, p = blocks[n]
        ch = slice(p * LANES, (p + 1) * LANES)
        qt = qt_ref[ch, k0:k0 + Q_BLOCK]
        zero = jnp.zeros_like(qt)
        rhs = jnp.concatenate([jnp.where(first_head, qt, zero),
                               jnp.where(first_head, zero, qt)], axis=1)
        rhs = jnp.concatenate([rhs, pen_rows], axis=0)
        lhs = jnp.concatenate([kw_ref[k0:k0 + KEY_SPAN, ch], pen_ref[k0:k0 + KEY_SPAN, :]],
                              axis=1)
        s_ref[n % 2] = jnp.dot(lhs, rhs, preferred_element_type=F32) + bias_ref[p]

    def attend(n):
        k0, p = blocks[n]
        ch = slice(p * LANES, (p + 1) * LANES)
        s = s_ref[n % 2]
        pr = jnp.exp2(s - jnp.max(s, axis=0, keepdims=True))
        denom = jnp.sum(pr, axis=0, keepdims=True)
        ot = jnp.dot(vtw_ref[ch, k0:k0 + KEY_SPAN], pr.astype(BF16),
                     preferred_element_type=F32) / denom
        yt_ref[p * LANES:p * LANES + HEAD_DIM, k0:k0 + Q_BLOCK] = ot[0:HEAD_DIM, 0:Q_BLOCK]
        yt_ref[p * LANES + HEAD_DIM:(p + 1) * LANES, k0:k0 + Q_BLOCK] = ot[HEAD_DIM:, Q_BLOCK:]

    def conv_half_out_proj(c):
        cols = slice(c * OUT_COL_CHUNK, (c + 1) * OUT_COL_CHUNK)
        hc_ref[:, cols] = x_ref[:, cols] + jnp.dot(yc_ref[...], wo_ref[0:CONV_WIDTH, cols],
                                                   preferred_element_type=F32)

    per_block = 2
    scores(0)
    for n in range(len(blocks)):
        if n + 1 < len(blocks):
            scores(n + 1)
        attend(n)
        if n < Q_BLOCKS:
            conv_half_out_proj(n)
        if n + 1 < len(blocks):
            for fn, c, j in mlp_pieces[n * per_block:(n + 1) * per_block]:
                fn(c, j)
    ynt = (_group_norm_channels(yt_ref[...]) * aog_ref[...]).astype(BF16)
    for fn, c, j in mlp_pieces[(len(blocks) - 1) * per_block:]:
        fn(c, j)
    ynt_ref[...] = ynt


def _layer(x, g1, col_params, bias, aog_col, g2, w_in, w_out, w1, w2):
    bsz, seq, _ = x.shape
    in_hbm = pl.BlockSpec(memory_space=pl.ANY)
    tiles_per_seq = seq // SEQ_TILE
    n_tiles = bsz * tiles_per_seq

    def tile_of(t):
        t = jnp.minimum(t, n_tiles - 1)
        return t // tiles_per_seq, t % tiles_per_seq, 0

    def out_tile_of(t):
        return tile_of(jnp.maximum(t - 1, 0))

    return pl.pallas_call(
        functools.partial(_layer_kernel, tiles_per_seq),
        grid=(n_tiles + 1,),
        in_specs=[pl.BlockSpec((None, SEQ_TILE, D_MODEL), tile_of),
                  _resident((1, D_MODEL)), _resident((CONV_WIDTH, N_COL_PARAMS)),
                  _resident((HEAD_PAIRS, KEY_SPAN, PAIR_COLS)), _resident((ATTN_WIDTH, 1)),
                  _resident((1, D_MODEL)), in_hbm, in_hbm, in_hbm, in_hbm],
        out_specs=pl.BlockSpec((None, SEQ_TILE, D_MODEL), out_tile_of),
        out_shape=jax.ShapeDtypeStruct(x.shape, F32),
        scratch_shapes=[pltpu.VMEM(w_in.shape[::-1], BF16),
                        pltpu.VMEM(w_out.shape, BF16),
                        pltpu.VMEM(w1.shape, BF16),
                        pltpu.VMEM(w2.shape, BF16),
                        pltpu.VMEM((2,) + WEIGHT_CHUNK, F32),
                        pltpu.SemaphoreType.DMA((2,)),
                        pltpu.VMEM((CONV_WIDTH, LANES), F32),
                        pltpu.VMEM((SEQ_TILE, D_MODEL), BF16),
                        pltpu.VMEM((SEQ_TILE, CONV_WIDTH), BF16),
                        pltpu.VMEM((ATTN_WIDTH, SEQ_TILE), BF16),
                        pltpu.VMEM((2 * SEQ_TILE, ATTN_WIDTH), BF16),
                        pltpu.VMEM((ATTN_WIDTH, 2 * SEQ_TILE), BF16),
                        pltpu.VMEM((2 * SEQ_TILE, LANES), BF16),
                        pltpu.VMEM((ATTN_WIDTH, SEQ_TILE), F32),
                        pltpu.VMEM((2, KEY_SPAN, PAIR_COLS), F32),
                        pltpu.VMEM((SEQ_TILE, D_MODEL), F32),
                        pltpu.VMEM((ATTN_WIDTH, SEQ_TILE), BF16),
                        pltpu.VMEM((SEQ_TILE, D_MODEL), BF16),
                        pltpu.VMEM((2, SEQ_TILE, FF_CHUNK), BF16)],
        compiler_params=pltpu.CompilerParams(
            dimension_semantics=("arbitrary",), vmem_limit_bytes=VMEM_LIMIT),
        name="layer",
    )(x, g1, col_params, bias, aog_col, g2, w_in, w_out, w1, w2)


def kernel(x, norm_mix_g, w_in, conv_w, conv_b, q_norm_g, k_norm_g, rel_bias, conv_out_g,
           attn_out_g, w_out, norm_mlp_g, w_mlp_in, w_mlp_out):
    depth = w_in.shape[0]
    for l in range(depth):
        col_params = jnp.stack(
            [conv_w[l, 0], conv_w[l, 1], conv_w[l, 2], conv_b[l], conv_out_g[l],
             jnp.tile(q_norm_g[l], N_HEADS), jnp.tile(k_norm_g[l], N_HEADS),
             jnp.zeros((CONV_WIDTH,), F32)], axis=1)
        x = _layer(x, norm_mix_g[l][None], col_params, _bias_table(rel_bias[l]),
                   attn_out_g[l][:, None], norm_mlp_g[l][None],
                   w_in[l], w_out[l], w_mlp_in[l], w_mlp_out[l])
    return x
```

```python
import functools
import math

import jax
import jax.numpy as jnp
from jax import lax
from jax.experimental import pallas as pl
from jax.experimental.pallas import tpu as pltpu

D_MODEL = 1024
CHUNK = 64
LEFT_CHUNKS = 8
BAND = LEFT_CHUNKS + 1
CONV_WIDTH = D_MODEL // 2
GROUP = 64
HEAD_DIM = 64
N_HEADS = (D_MODEL - CONV_WIDTH) // HEAD_DIM
ATTN_WIDTH = N_HEADS * HEAD_DIM
REL_CLIP = 128
REL_TABLE = (CHUNK - 1) + REL_CLIP + 1
D_FF = 4 * D_MODEL
EPS = 1e-6
NEG_INF = -1e30
LOG2_E = math.log2(math.e)

LANES = 128
SEQ_TILE = LEFT_CHUNKS * CHUNK
Q_BLOCK = 2 * CHUNK
Q_BLOCKS = SEQ_TILE // Q_BLOCK
KEY_SPAN = (BAND + 1) * CHUNK
HEAD_PAIRS = N_HEADS * HEAD_DIM // LANES
PAIR_COLS = 2 * Q_BLOCK
BIAS_FLAT_ROWS = KEY_SPAN - 2 * Q_BLOCK
BIAS_RING = 4 * LANES
OUT_COL_CHUNK = D_MODEL // Q_BLOCKS
FF_CHUNK = 1024
MLP_COLS = 256
WEIGHT_CHUNK = (512, 1024)
PIECES_BEFORE_ATTENTION = 2
PIECES_AFTER_FIRST_SCORES = 0
PIECES_PER_BLOCK = (2,) * 14 + (1, 0)
VMEM_LIMIT = 60000 * 1024

COL_CONV_W, COL_CONV_B, COL_CONV_G, COL_Q_G, COL_K_G, N_COL_PARAMS = 0, 3, 4, 5, 6, 8

F32 = jnp.float32
BF16 = jnp.bfloat16
NT_DIMS = (((1,), (1,)), ((), ()))
TN_DIMS = (((0,), (0,)), ((), ()))


def _resident(shape):
    return pl.BlockSpec(shape, lambda *_: (0,) * len(shape), pipeline_mode=pl.Buffered(1))


def _rms_norm_bf16(x, gain):
    ms = jnp.mean(x * x, axis=-1, keepdims=True)
    return (x * lax.rsqrt(ms + EPS) * gain).astype(BF16)


def _group_norm_channels(y):
    c, t = y.shape
    y3 = y.reshape(c // GROUP, GROUP, t)
    ms = jnp.mean(y3 * y3, axis=1, keepdims=True)
    return (y3 * lax.rsqrt(ms + EPS)).reshape(c, t)


def _bias_table_kernel(ring_ref, out_ref):
    def fields(rows, r0):
        r = lax.broadcasted_iota(jnp.int32, (rows, PAIR_COLS), 0) + r0
        c = lax.broadcasted_iota(jnp.int32, (rows, PAIR_COLS), 1)
        j = (c // CHUNK) % 2
        return c < Q_BLOCK, (r >= j * CHUNK) & (r < j * CHUNK + BAND * CHUNK)

    near_rows = KEY_SPAN - BIAS_FLAT_ROWS
    first_far, in_band_far = fields(BIAS_FLAT_ROWS, 0)
    _, in_band_near = fields(near_rows, BIAS_FLAT_ROWS)
    for p in range(HEAD_PAIRS):
        far = jnp.where(first_far, ring_ref[2 * p:2 * p + 1, 0:1], ring_ref[2 * p + 1:2 * p + 2, 0:1])
        out_ref[p, 0:BIAS_FLAT_ROWS, :] = jnp.where(in_band_far, far * LOG2_E, NEG_INF)
        halves = []
        for e in range(2):
            ring = jnp.broadcast_to(ring_ref[2 * p + e:2 * p + e + 1, :], (near_rows, BIAS_RING))
            halves.append(pltpu.roll(ring, 0, 1, stride=1, stride_axis=0)[:, 0:Q_BLOCK])
        near = jnp.concatenate(halves, axis=1)
        out_ref[p, BIAS_FLAT_ROWS:, :] = jnp.where(in_band_near, near * LOG2_E, NEG_INF)


def _bias_ring(rel_bias):
    far = jnp.broadcast_to(rel_bias[:, -1:], (N_HEADS, BIAS_RING - 2 * Q_BLOCK + 1))
    past = jnp.broadcast_to(rel_bias[:, :1], (N_HEADS, CHUNK))
    return jnp.concatenate([far, past, rel_bias[:, :REL_TABLE - 1]], axis=1)


def _bias_table(rel_bias):
    return pl.pallas_call(
        _bias_table_kernel,
        in_specs=[pl.BlockSpec(memory_space=pltpu.VMEM)],
        out_specs=pl.BlockSpec(memory_space=pltpu.VMEM),
        out_shape=jax.ShapeDtypeStruct((HEAD_PAIRS, KEY_SPAN, PAIR_COLS), F32),
        name="rel_bias_table",
    )(_bias_ring(rel_bias))


def _load_weights_bf16(w_in_hbm, w_out_hbm, w1_hbm, w2_hbm, wt_ref, wo_ref, w1_ref, w2_ref,
                       stage_ref, sem):
    rows, cols = stage_ref.shape[1:]
    chunks = []

    def add(src, dst, transposed=False):
        for r0 in range(0, src.shape[0], rows):
            for c0 in range(0, src.shape[1], cols):
                if transposed:
                    def store(v, r0=r0, c0=c0):
                        dst[c0:c0 + cols, r0:r0 + rows] = v.T.astype(BF16)
                else:
                    def store(v, r0=r0, c0=c0):
                        dst[r0:r0 + rows, c0:c0 + cols] = v.astype(BF16)
                chunks.append((src.at[r0:r0 + rows, c0:c0 + cols], store))

    add(w_in_hbm, wt_ref, transposed=True)
    add(w_out_hbm, wo_ref)
    add(w1_hbm, w1_ref)
    add(w2_hbm, w2_ref)

    def copy(n):
        return pltpu.make_async_copy(chunks[n][0], stage_ref.at[n % 2], sem.at[n % 2])

    copy(0).start()
    for n in range(len(chunks)):
        if n + 1 < len(chunks):
            copy(n + 1).start()
        copy(n).wait()
        chunks[n][1](stage_ref[n % 2])


def _layer_kernel(tiles_per_seq, x_ref, g1_ref, colp_ref, bias_ref, aog_ref, g2_ref,
                  w_in_hbm, w_out_hbm, w1_hbm, w2_hbm, o_ref,
                  wt_ref, wo_ref, w1_ref, w2_ref, stage_ref, sem,
                  carry_ref, xn_ref, yc_ref, qt_ref, kw_ref, vtw_ref, pen_ref, yt_ref, s_ref,
                  hc_ref, yn_ref, xn2_ref, hid_ref):
    t = pl.program_id(0)
    n_tiles = pl.num_programs(0) - 1
    first_tile = lax.rem(jnp.minimum(t, n_tiles - 1), tiles_per_seq) == 0

    @pl.when(t == 0)
    def _():
        _load_weights_bf16(w_in_hbm, w_out_hbm, w1_hbm, w2_hbm, wt_ref, wo_ref, w1_ref, w2_ref,
                           stage_ref, sem)
        kw_ref[...] = jnp.zeros_like(kw_ref)
        vtw_ref[...] = jnp.zeros_like(vtw_ref)
        hc_ref[...] = jnp.zeros_like(hc_ref)
        yn_ref[...] = jnp.zeros_like(yn_ref)

    @pl.when(first_tile)
    def _():
        carry_ref[...] = jnp.zeros_like(carry_ref)

    o_ref[...] = hc_ref[...] + jnp.dot(yn_ref[...], wo_ref[CONV_WIDTH:, :],
                                       preferred_element_type=F32)
    xn_ref[...] = _rms_norm_bf16(x_ref[...], g1_ref[...])

    def mlp_prologue():
        xn2_ref[...] = _rms_norm_bf16(o_ref[...], g2_ref[...])

    def mlp_up(c, j):
        cols = slice(c * FF_CHUNK + j * MLP_COLS, c * FF_CHUNK + (j + 1) * MLP_COLS)
        hid = jnp.maximum(jnp.dot(xn2_ref[...], w1_ref[:, cols], preferred_element_type=F32), 0.0)
        hid_ref[c % 2, :, j * MLP_COLS:(j + 1) * MLP_COLS] = (hid * hid).astype(BF16)

    def mlp_down(c, j):
        cols = slice(j * MLP_COLS, (j + 1) * MLP_COLS)
        o_ref[:, cols] += jnp.dot(hid_ref[c % 2], w2_ref[c * FF_CHUNK:(c + 1) * FF_CHUNK, cols],
                                  preferred_element_type=F32)

    mlp_pieces = []
    for c in range(D_FF // FF_CHUNK):
        mlp_pieces += [(mlp_up, c, j) for j in range(FF_CHUNK // MLP_COLS)]
        mlp_pieces += [(mlp_down, c, j) for j in range(D_MODEL // MLP_COLS)]
    assert len(mlp_pieces) == (PIECES_BEFORE_ATTENTION + PIECES_AFTER_FIRST_SCORES
                               + sum(PIECES_PER_BLOCK) + 1)

    def run_mlp_piece():
        fn, c, j = mlp_pieces.pop(0)
        fn(c, j)

    def proj_t(j):
        return lax.dot_general(wt_ref[j * CONV_WIDTH:(j + 1) * CONV_WIDTH, :], xn_ref[...],
                               NT_DIMS, preferred_element_type=F32)

    def col(j):
        return colp_ref[:, j:j + 1]

    kw_ref[0:SEQ_TILE, :] = kw_ref[SEQ_TILE:, :]
    vtw_ref[:, 0:SEQ_TILE] = vtw_ref[:, SEQ_TILE:]
    pen_lane = lax.broadcasted_iota(jnp.int32, (SEQ_TILE, LANES), 1) == 0
    pen_ref[0:SEQ_TILE, :] = jnp.where(pen_lane & first_tile, NEG_INF, 0.0).astype(BF16)
    pen_ref[SEQ_TILE:, :] = jnp.zeros((SEQ_TILE, LANES), BF16)

    gate_c = proj_t(1)
    gate_h = proj_t(2)
    gate_b = proj_t(0)
    mlp_prologue()
    k_t = proj_t(4)
    u = gate_c * gate_h
    ext = jnp.concatenate([carry_ref[...], u], axis=1)
    u1 = pltpu.roll(ext, 1, 1)[:, LANES:]
    u2 = pltpu.roll(ext, 2, 1)[:, LANES:]
    carry_ref[...] = u[:, SEQ_TILE - LANES:]
    y = col(COL_CONV_W) * u2 + col(COL_CONV_W + 1) * u1 + col(COL_CONV_W + 2) * u
    yc = gate_b * (y + col(COL_CONV_B))
    yc_ref[...] = (_group_norm_channels(yc) * col(COL_CONV_G)).T.astype(BF16)
    q_t = proj_t(3)
    kw_ref[SEQ_TILE:, :] = (_group_norm_channels(k_t) * col(COL_K_G)).T.astype(BF16)
    v_t = proj_t(5)
    for _ in range(PIECES_BEFORE_ATTENTION):
        run_mlp_piece()
    qt_ref[...] = (_group_norm_channels(q_t) * col(COL_Q_G)
                   * (HEAD_DIM ** -0.5 * LOG2_E)).astype(BF16)
    vtw_ref[:, SEQ_TILE:] = v_t.astype(BF16)

    first_head = lax.broadcasted_iota(jnp.int32, (LANES, Q_BLOCK), 0) < HEAD_DIM
    pen_rows = (lax.broadcasted_iota(jnp.int32, (LANES, PAIR_COLS), 0) == 0).astype(BF16)
    blocks = [(qb * Q_BLOCK, p) for qb in range(Q_BLOCKS) for p in range(HEAD_PAIRS)]

    def scores(n):
        k0, p = blocks[n]
        ch = slice(p * LANES, (p + 1) * LANES)
        qt = qt_ref[ch, k0:k0 + Q_BLOCK]
        zero = jnp.zeros_like(qt)
        rhs = jnp.concatenate([jnp.where(first_head, qt, zero),
                               jnp.where(first_head, zero, qt)], axis=1)
        rhs = jnp.concatenate([rhs, pen_rows], axis=0)
        lhs = jnp.concatenate([kw_ref[k0:k0 + KEY_SPAN, ch], pen_ref[k0:k0 + KEY_SPAN, :]],
                              axis=1)
        s_ref[n % 2] = jnp.dot(lhs, rhs, preferred_element_type=F32) + bias_ref[p]

    def attend(n):
        k0, p = blocks[n]
        ch = slice(p * LANES, (p + 1) * LANES)
        s = s_ref[n % 2]
        pr = jnp.exp2(s - jnp.max(s, axis=0, keepdims=True))
        denom = jnp.sum(pr, axis=0, keepdims=True)
        ot = jnp.dot(vtw_ref[ch, k0:k0 + KEY_SPAN], pr.astype(BF16),
                     preferred_element_type=F32) / denom
        yt_ref[p * LANES:p * LANES + HEAD_DIM, k0:k0 + Q_BLOCK] = ot[0:HEAD_DIM, 0:Q_BLOCK]
        yt_ref[p * LANES + HEAD_DIM:(p + 1) * LANES, k0:k0 + Q_BLOCK] = ot[HEAD_DIM:, Q_BLOCK:]

    def conv_half_out_proj(c):
        cols = slice(c * OUT_COL_CHUNK, (c + 1) * OUT_COL_CHUNK)
        hc_ref[:, cols] = x_ref[:, cols] + jnp.dot(yc_ref[...], wo_ref[0:CONV_WIDTH, cols],
                                                   preferred_element_type=F32)

    scores(0)
    for _ in range(PIECES_AFTER_FIRST_SCORES):
        run_mlp_piece()
    for n in range(len(blocks)):
        if n + 1 < len(blocks):
            scores(n + 1)
        attend(n)
        if n < Q_BLOCKS:
            conv_half_out_proj(n)
        for _ in range(PIECES_PER_BLOCK[n]):
            run_mlp_piece()
    yn_ref[...] = (_group_norm_channels(yt_ref[...]) * aog_ref[...]).T.astype(BF16)
    while mlp_pieces:
        run_mlp_piece()


def _layer(x, g1, col_params, bias, aog_col, g2, w_in, w_out, w1, w2):
    bsz, seq, _ = x.shape
    in_hbm = pl.BlockSpec(memory_space=pl.ANY)
    tiles_per_seq = seq // SEQ_TILE
    n_tiles = bsz * tiles_per_seq

    def tile_of(t):
        t = jnp.minimum(t, n_tiles - 1)
        return t // tiles_per_seq, t % tiles_per_seq, 0

    def out_tile_of(t):
        return tile_of(jnp.maximum(t - 1, 0))

    return pl.pallas_call(
        functools.partial(_layer_kernel, tiles_per_seq),
        grid=(n_tiles + 1,),
        in_specs=[pl.BlockSpec((None, SEQ_TILE, D_MODEL), tile_of),
                  _resident((1, D_MODEL)), _resident((CONV_WIDTH, N_COL_PARAMS)),
                  _resident((HEAD_PAIRS, KEY_SPAN, PAIR_COLS)), _resident((ATTN_WIDTH, 1)),
                  _resident((1, D_MODEL)), in_hbm, in_hbm, in_hbm, in_hbm],
        out_specs=pl.BlockSpec((None, SEQ_TILE, D_MODEL), out_tile_of),
        out_shape=jax.ShapeDtypeStruct(x.shape, F32),
        scratch_shapes=[pltpu.VMEM(w_in.shape[::-1], BF16),
                        pltpu.VMEM(w_out.shape, BF16),
                        pltpu.VMEM(w1.shape, BF16),
                        pltpu.VMEM(w2.shape, BF16),
                        pltpu.VMEM((2,) + WEIGHT_CHUNK, F32),
                        pltpu.SemaphoreType.DMA((2,)),
                        pltpu.VMEM((CONV_WIDTH, LANES), F32),
                        pltpu.VMEM((SEQ_TILE, D_MODEL), BF16),
                        pltpu.VMEM((SEQ_TILE, CONV_WIDTH), BF16),
                        pltpu.VMEM((ATTN_WIDTH, SEQ_TILE), BF16),
                        pltpu.VMEM((2 * SEQ_TILE, ATTN_WIDTH), BF16),
                        pltpu.VMEM((ATTN_WIDTH, 2 * SEQ_TILE), BF16),
                        pltpu.VMEM((2 * SEQ_TILE, LANES), BF16),
                        pltpu.VMEM((ATTN_WIDTH, SEQ_TILE), F32),
                        pltpu.VMEM((2, KEY_SPAN, PAIR_COLS), F32),
                        pltpu.VMEM((SEQ_TILE, D_MODEL), F32),
                        pltpu.VMEM((SEQ_TILE, ATTN_WIDTH), BF16),
                        pltpu.VMEM((SEQ_TILE, D_MODEL), BF16),
                        pltpu.VMEM((2, SEQ_TILE, FF_CHUNK), BF16)],
        compiler_params=pltpu.CompilerParams(
            dimension_semantics=("arbitrary",), vmem_limit_bytes=VMEM_LIMIT),
        name="layer",
    )(x, g1, col_params, bias, aog_col, g2, w_in, w_out, w1, w2)


def kernel(x, norm_mix_g, w_in, conv_w, conv_b, q_norm_g, k_norm_g, rel_bias, conv_out_g,
           attn_out_g, w_out, norm_mlp_g, w_mlp_in, w_mlp_out):
    depth = w_in.shape[0]
    for l in range(depth):
        col_params = jnp.stack(
            [conv_w[l, 0], conv_w[l, 1], conv_w[l, 2], conv_b[l], conv_out_g[l],
             jnp.tile(q_norm_g[l], N_HEADS), jnp.tile(k_norm_g[l], N_HEADS),
             jnp.zeros((CONV_WIDTH,), F32)], axis=1)
        x = _layer(x, norm_mix_g[l][None], col_params, _bias_table(rel_bias[l]),
                   attn_out_g[l][:, None], norm_mlp_g[l][None],
                   w_in[l], w_out[l], w_mlp_in[l], w_mlp_out[l])
    return x
```

```python
import functools
import math

import jax
import jax.numpy as jnp
from jax import lax
from jax.experimental import pallas as pl
from jax.experimental.pallas import tpu as pltpu

D_MODEL = 1024
CHUNK = 64
LEFT_CHUNKS = 8
BAND = LEFT_CHUNKS + 1
CONV_WIDTH = D_MODEL // 2
GROUP = 64
HEAD_DIM = 64
N_HEADS = (D_MODEL - CONV_WIDTH) // HEAD_DIM
ATTN_WIDTH = N_HEADS * HEAD_DIM
REL_CLIP = 128
REL_TABLE = (CHUNK - 1) + REL_CLIP + 1
D_FF = 4 * D_MODEL
EPS = 1e-6
NEG_INF = -1e30
LOG2_E = math.log2(math.e)

LANES = 128
SEQ_TILE = LEFT_CHUNKS * CHUNK
Q_BLOCK = 2 * CHUNK
Q_BLOCKS = SEQ_TILE // Q_BLOCK
KEY_SPAN = (BAND + 1) * CHUNK
HEAD_PAIRS = N_HEADS * HEAD_DIM // LANES
PAIR_COLS = 2 * Q_BLOCK
BIAS_FLAT_ROWS = KEY_SPAN - 2 * Q_BLOCK
BIAS_RING = 4 * LANES
OUT_COL_CHUNK = D_MODEL // Q_BLOCKS
FF_CHUNK = 1024
MLP_COLS = 256
WEIGHT_CHUNK = (512, 1024)
PIECES_BEFORE_ATTENTION = 2
PIECES_AFTER_FIRST_SCORES = 0
PIECES_PER_BLOCK = (1,) * 14 + (1, 1)
VMEM_LIMIT = 60000 * 1024

COL_CONV_W, COL_CONV_B, COL_CONV_G, COL_Q_G, COL_K_G, N_COL_PARAMS = 0, 3, 4, 5, 6, 8

F32 = jnp.float32
BF16 = jnp.bfloat16
NT_DIMS = (((1,), (1,)), ((), ()))
TN_DIMS = (((0,), (0,)), ((), ()))


def _resident(shape):
    return pl.BlockSpec(shape, lambda *_: (0,) * len(shape), pipeline_mode=pl.Buffered(1))


def _rms_norm_bf16(x, gain):
    ms = jnp.mean(x * x, axis=-1, keepdims=True)
    return (x * lax.rsqrt(ms + EPS) * gain).astype(BF16)


def _group_norm_channels(y):
    c, t = y.shape
    y3 = y.reshape(c // GROUP, GROUP, t)
    ms = jnp.mean(y3 * y3, axis=1, keepdims=True)
    return (y3 * lax.rsqrt(ms + EPS)).reshape(c, t)


def _bias_table_kernel(ring_ref, out_ref):
    def fields(rows, r0):
        r = lax.broadcasted_iota(jnp.int32, (rows, PAIR_COLS), 0) + r0
        c = lax.broadcasted_iota(jnp.int32, (rows, PAIR_COLS), 1)
        j = (c // CHUNK) % 2
        return c < Q_BLOCK, (r >= j * CHUNK) & (r < j * CHUNK + BAND * CHUNK)

    near_rows = KEY_SPAN - BIAS_FLAT_ROWS
    first_far, in_band_far = fields(BIAS_FLAT_ROWS, 0)
    _, in_band_near = fields(near_rows, BIAS_FLAT_ROWS)
    for p in range(HEAD_PAIRS):
        far = jnp.where(first_far, ring_ref[2 * p:2 * p + 1, 0:1], ring_ref[2 * p + 1:2 * p + 2, 0:1])
        out_ref[p, 0:BIAS_FLAT_ROWS, :] = jnp.where(in_band_far, far * LOG2_E, NEG_INF)
        halves = []
        for e in range(2):
            ring = jnp.broadcast_to(ring_ref[2 * p + e:2 * p + e + 1, :], (near_rows, BIAS_RING))
            halves.append(pltpu.roll(ring, 0, 1, stride=1, stride_axis=0)[:, 0:Q_BLOCK])
        near = jnp.concatenate(halves, axis=1)
        out_ref[p, BIAS_FLAT_ROWS:, :] = jnp.where(in_band_near, near * LOG2_E, NEG_INF)


def _bias_ring(rel_bias):
    far = jnp.broadcast_to(rel_bias[:, -1:], (N_HEADS, BIAS_RING - 2 * Q_BLOCK + 1))
    past = jnp.broadcast_to(rel_bias[:, :1], (N_HEADS, CHUNK))
    return jnp.concatenate([far, past, rel_bias[:, :REL_TABLE - 1]], axis=1)


def _bias_table(rel_bias):
    return pl.pallas_call(
        _bias_table_kernel,
        in_specs=[pl.BlockSpec(memory_space=pltpu.VMEM)],
        out_specs=pl.BlockSpec(memory_space=pltpu.VMEM),
        out_shape=jax.ShapeDtypeStruct((HEAD_PAIRS, KEY_SPAN, PAIR_COLS), F32),
        name="rel_bias_table",
    )(_bias_ring(rel_bias))


def _load_weights_bf16(w_in_hbm, w_out_hbm, w1_hbm, w2_hbm, wt_ref, wo_ref, w1_ref, w2_ref,
                       stage_ref, sem):
    rows, cols = stage_ref.shape[1:]
    chunks = []

    def add(src, dst, transposed=False):
        for r0 in range(0, src.shape[0], rows):
            for c0 in range(0, src.shape[1], cols):
                if transposed:
                    def store(v, r0=r0, c0=c0):
                        dst[c0:c0 + cols, r0:r0 + rows] = v.T.astype(BF16)
                else:
                    def store(v, r0=r0, c0=c0):
                        dst[r0:r0 + rows, c0:c0 + cols] = v.astype(BF16)
                chunks.append((src.at[r0:r0 + rows, c0:c0 + cols], store))

    add(w_in_hbm, wt_ref, transposed=True)
    add(w_out_hbm, wo_ref)
    add(w1_hbm, w1_ref)
    add(w2_hbm, w2_ref)

    def copy(n):
        return pltpu.make_async_copy(chunks[n][0], stage_ref.at[n % 2], sem.at[n % 2])

    copy(0).start()
    for n in range(len(chunks)):
        if n + 1 < len(chunks):
            copy(n + 1).start()
        copy(n).wait()
        chunks[n][1](stage_ref[n % 2])


def _layer_kernel(tiles_per_seq, x_ref, g1_ref, colp_ref, bias_ref, aog_ref, g2_ref,
                  w_in_hbm, w_out_hbm, w1_hbm, w2_hbm, o_ref,
                  wt_ref, wo_ref, w1_ref, w2_ref, stage_ref, sem,
                  carry_ref, xn_ref, yc_ref, qt_ref, kw_ref, vtw_ref, pen_ref, yt_ref, s_ref,
                  hc_ref, yn_ref, xn2_ref, hid_ref):
    t = pl.program_id(0)
    n_tiles = pl.num_programs(0) - 1
    first_tile = lax.rem(jnp.minimum(t, n_tiles - 1), tiles_per_seq) == 0

    @pl.when(t == 0)
    def _():
        _load_weights_bf16(w_in_hbm, w_out_hbm, w1_hbm, w2_hbm, wt_ref, wo_ref, w1_ref, w2_ref,
                           stage_ref, sem)
        kw_ref[...] = jnp.zeros_like(kw_ref)
        vtw_ref[...] = jnp.zeros_like(vtw_ref)
        hc_ref[...] = jnp.zeros_like(hc_ref)
        yn_ref[...] = jnp.zeros_like(yn_ref)

    @pl.when(first_tile)
    def _():
        carry_ref[...] = jnp.zeros_like(carry_ref)

    o_ref[...] = hc_ref[...] + jnp.dot(yn_ref[...], wo_ref[CONV_WIDTH:, :],
                                       preferred_element_type=F32)
    xn_ref[...] = _rms_norm_bf16(x_ref[...], g1_ref[...])

    def mlp_prologue():
        xn2_ref[...] = _rms_norm_bf16(o_ref[...], g2_ref[...])

    def mlp_up(j):
        cols = slice(j * MLP_COLS, (j + 1) * MLP_COLS)
        hid = jnp.maximum(jnp.dot(xn2_ref[...], w1_ref[:, cols], preferred_element_type=F32), 0.0)
        hid_ref[:, cols] = (hid * hid).astype(BF16)

    def mlp_down(j):
        cols = slice(j * MLP_COLS, (j + 1) * MLP_COLS)
        o_ref[:, cols] += jnp.dot(hid_ref[...], w2_ref[:, cols], preferred_element_type=F32)

    mlp_pieces = [(mlp_up, j) for j in range(D_FF // MLP_COLS)]
    mlp_pieces += [(mlp_down, j) for j in range(D_MODEL // MLP_COLS)]
    assert len(mlp_pieces) == (PIECES_BEFORE_ATTENTION + PIECES_AFTER_FIRST_SCORES
                               + sum(PIECES_PER_BLOCK) + 2)

    def run_mlp_piece():
        fn, j = mlp_pieces.pop(0)
        fn(j)

    def proj_t(j):
        return lax.dot_general(wt_ref[j * CONV_WIDTH:(j + 1) * CONV_WIDTH, :], xn_ref[...],
                               NT_DIMS, preferred_element_type=F32)

    def col(j):
        return colp_ref[:, j:j + 1]

    kw_ref[0:SEQ_TILE, :] = kw_ref[SEQ_TILE:, :]
    vtw_ref[:, 0:SEQ_TILE] = vtw_ref[:, SEQ_TILE:]
    pen_lane = lax.broadcasted_iota(jnp.int32, (SEQ_TILE, LANES), 1) == 0
    pen_ref[0:SEQ_TILE, :] = jnp.where(pen_lane & first_tile, NEG_INF, 0.0).astype(BF16)
    pen_ref[SEQ_TILE:, :] = jnp.zeros((SEQ_TILE, LANES), BF16)

    gate_c = proj_t(1)
    gate_h = proj_t(2)
    gate_b = proj_t(0)
    mlp_prologue()
    k_t = proj_t(4)
    u = gate_c * gate_h
    ext = jnp.concatenate([carry_ref[...], u], axis=1)
    u1 = pltpu.roll(ext, 1, 1)[:, LANES:]
    u2 = pltpu.roll(ext, 2, 1)[:, LANES:]
    carry_ref[...] = u[:, SEQ_TILE - LANES:]
    y = col(COL_CONV_W) * u2 + col(COL_CONV_W + 1) * u1 + col(COL_CONV_W + 2) * u
    yc = gate_b * (y + col(COL_CONV_B))
    yc_ref[...] = (_group_norm_channels(yc) * col(COL_CONV_G)).T.astype(BF16)
    q_t = proj_t(3)
    kw_ref[SEQ_TILE:, :] = (_group_norm_channels(k_t) * col(COL_K_G)).T.astype(BF16)
    v_t = proj_t(5)
    for _ in range(PIECES_BEFORE_ATTENTION):
        run_mlp_piece()
    qt_ref[...] = (_group_norm_channels(q_t) * col(COL_Q_G)
                   * (HEAD_DIM ** -0.5 * LOG2_E)).astype(BF16)
    vtw_ref[:, SEQ_TILE:] = v_t.astype(BF16)

    first_head = lax.broadcasted_iota(jnp.int32, (LANES, Q_BLOCK), 0) < HEAD_DIM
    pen_rows = (lax.broadcasted_iota(jnp.int32, (LANES, PAIR_COLS), 0) == 0).astype(BF16)
    blocks = [(qb * Q_BLOCK, p) for qb in range(Q_BLOCKS) for p in range(HEAD_PAIRS)]

    def scores(n):
        k0, p = blocks[n]
        ch = slice(p * LANES, (p + 1) * LANES)
        qt = qt_ref[ch, k0:k0 + Q_BLOCK]
        zero = jnp.zeros_like(qt)
        rhs = jnp.concatenate([jnp.where(first_head, qt, zero),
                               jnp.where(first_head, zero, qt)], axis=1)
        rhs = jnp.concatenate([rhs, pen_rows], axis=0)
        lhs = jnp.concatenate([kw_ref[k0:k0 + KEY_SPAN, ch], pen_ref[k0:k0 + KEY_SPAN, :]],
                              axis=1)
        s_ref[n % 2] = jnp.dot(lhs, rhs, preferred_element_type=F32) + bias_ref[p]

    def attend(n):
        k0, p = blocks[n]
        ch = slice(p * LANES, (p + 1) * LANES)
        s = s_ref[n % 2]
        pr = jnp.exp2(s - jnp.max(s, axis=0, keepdims=True))
        denom = jnp.sum(pr, axis=0, keepdims=True)
        ot = jnp.dot(vtw_ref[ch, k0:k0 + KEY_SPAN], pr.astype(BF16),
                     preferred_element_type=F32) / denom
        yt_ref[p * LANES:p * LANES + HEAD_DIM, k0:k0 + Q_BLOCK] = ot[0:HEAD_DIM, 0:Q_BLOCK]
        yt_ref[p * LANES + HEAD_DIM:(p + 1) * LANES, k0:k0 + Q_BLOCK] = ot[HEAD_DIM:, Q_BLOCK:]

    def conv_half_out_proj(c):
        cols = slice(c * OUT_COL_CHUNK, (c + 1) * OUT_COL_CHUNK)
        hc_ref[:, cols] = x_ref[:, cols] + jnp.dot(yc_ref[...], wo_ref[0:CONV_WIDTH, cols],
                                                   preferred_element_type=F32)

    scores(0)
    for _ in range(PIECES_AFTER_FIRST_SCORES):
        run_mlp_piece()
    for n in range(len(blocks)):
        if n + 1 < len(blocks):
            scores(n + 1)
        attend(n)
        if n < Q_BLOCKS:
            conv_half_out_proj(n)
        for _ in range(PIECES_PER_BLOCK[n]):
            run_mlp_piece()
    yn_ref[...] = (_group_norm_channels(yt_ref[...]) * aog_ref[...]).T.astype(BF16)
    while mlp_pieces:
        run_mlp_piece()


def _layer(x, g1, col_params, bias, aog_col, g2, w_in, w_out, w1, w2):
    bsz, seq, _ = x.shape
    in_hbm = pl.BlockSpec(memory_space=pl.ANY)
    tiles_per_seq = seq // SEQ_TILE
    n_tiles = bsz * tiles_per_seq

    def tile_of(t):
        t = jnp.minimum(t, n_tiles - 1)
        return t // tiles_per_seq, t % tiles_per_seq, 0

    def out_tile_of(t):
        return tile_of(jnp.maximum(t - 1, 0))

    return pl.pallas_call(
        functools.partial(_layer_kernel, tiles_per_seq),
        grid=(n_tiles + 1,),
        in_specs=[pl.BlockSpec((None, SEQ_TILE, D_MODEL), tile_of),
                  _resident((1, D_MODEL)), _resident((CONV_WIDTH, N_COL_PARAMS)),
                  _resident((HEAD_PAIRS, KEY_SPAN, PAIR_COLS)), _resident((ATTN_WIDTH, 1)),
                  _resident((1, D_MODEL)), in_hbm, in_hbm, in_hbm, in_hbm],
        out_specs=pl.BlockSpec((None, SEQ_TILE, D_MODEL), out_tile_of),
        out_shape=jax.ShapeDtypeStruct(x.shape, F32),
        scratch_shapes=[pltpu.VMEM(w_in.shape[::-1], BF16),
                        pltpu.VMEM(w_out.shape, BF16),
                        pltpu.VMEM(w1.shape, BF16),
                        pltpu.VMEM(w2.shape, BF16),
                        pltpu.VMEM((2,) + WEIGHT_CHUNK, F32),
                        pltpu.SemaphoreType.DMA((2,)),
                        pltpu.VMEM((CONV_WIDTH, LANES), F32),
                        pltpu.VMEM((SEQ_TILE, D_MODEL), BF16),
                        pltpu.VMEM((SEQ_TILE, CONV_WIDTH), BF16),
                        pltpu.VMEM((ATTN_WIDTH, SEQ_TILE), BF16),
                        pltpu.VMEM((2 * SEQ_TILE, ATTN_WIDTH), BF16),
                        pltpu.VMEM((ATTN_WIDTH, 2 * SEQ_TILE), BF16),
                        pltpu.VMEM((2 * SEQ_TILE, LANES), BF16),
                        pltpu.VMEM((ATTN_WIDTH, SEQ_TILE), F32),
                        pltpu.VMEM((2, KEY_SPAN, PAIR_COLS), F32),
                        pltpu.VMEM((SEQ_TILE, D_MODEL), F32),
                        pltpu.VMEM((SEQ_TILE, ATTN_WIDTH), BF16),
                        pltpu.VMEM((SEQ_TILE, D_MODEL), BF16),
                        pltpu.VMEM((SEQ_TILE, D_FF), BF16)],
        compiler_params=pltpu.CompilerParams(
            dimension_semantics=("arbitrary",), vmem_limit_bytes=VMEM_LIMIT),
        name="layer",
    )(x, g1, col_params, bias, aog_col, g2, w_in, w_out, w1, w2)


def kernel(x, norm_mix_g, w_in, conv_w, conv_b, q_norm_g, k_norm_g, rel_bias, conv_out_g,
           attn_out_g, w_out, norm_mlp_g, w_mlp_in, w_mlp_out):
    depth = w_in.shape[0]
    for l in range(depth):
        col_params = jnp.stack(
            [conv_w[l, 0], conv_w[l, 1], conv_w[l, 2], conv_b[l], conv_out_g[l],
             jnp.tile(q_norm_g[l], N_HEADS), jnp.tile(k_norm_g[l], N_HEADS),
             jnp.zeros((CONV_WIDTH,), F32)], axis=1)
        x = _layer(x, norm_mix_g[l][None], col_params, _bias_table(rel_bias[l]),
                   attn_out_g[l][:, None], norm_mlp_g[l][None],
                   w_in[l], w_out[l], w_mlp_in[l], w_mlp_out[l])
    return x
```

```python
import functools
import math

import jax
import jax.numpy as jnp
from jax import lax
from jax.experimental import pallas as pl
from jax.experimental.pallas import tpu as pltpu

D_MODEL = 1024
CHUNK = 64
LEFT_CHUNKS = 8
BAND = LEFT_CHUNKS + 1
CONV_WIDTH = D_MODEL // 2
GROUP = 64
HEAD_DIM = 64
N_HEADS = (D_MODEL - CONV_WIDTH) // HEAD_DIM
ATTN_WIDTH = N_HEADS * HEAD_DIM
REL_CLIP = 128
REL_TABLE = (CHUNK - 1) + REL_CLIP + 1
D_FF = 4 * D_MODEL
EPS = 1e-6
NEG_INF = -1e30
LOG2_E = math.log2(math.e)

LANES = 128
SEQ_TILE = LEFT_CHUNKS * CHUNK
Q_BLOCK = 2 * CHUNK
Q_BLOCKS = SEQ_TILE // Q_BLOCK
KEY_SPAN = (BAND + 1) * CHUNK
HEAD_PAIRS = N_HEADS * HEAD_DIM // LANES
PAIR_COLS = 2 * Q_BLOCK
BIAS_FLAT_ROWS = KEY_SPAN - 2 * Q_BLOCK
BIAS_RING = 4 * LANES
MLP_COLS = 256
MLP_PIECES = (D_FF + D_MODEL) // MLP_COLS
WEIGHT_CHUNK = (512, 1024)
PIECES_BEFORE_ATTENTION = 2
PIECES_CLOSING = MLP_PIECES - PIECES_BEFORE_ATTENTION - SEQ_TILE // Q_BLOCK * HEAD_PAIRS
VMEM_LIMIT = 60000 * 1024

COL_CONV_W, COL_CONV_B, COL_CONV_G, COL_Q_G, COL_K_G, N_COL_PARAMS = 0, 3, 4, 5, 6, 8

F32 = jnp.float32
BF16 = jnp.bfloat16
NT_DIMS = (((1,), (1,)), ((), ()))


def _resident(shape):
    return pl.BlockSpec(shape, lambda *_: (0,) * len(shape), pipeline_mode=pl.Buffered(1))


def _rms_norm_bf16(x, gain):
    ms = jnp.mean(x * x, axis=-1, keepdims=True)
    return (x * lax.rsqrt(ms + EPS) * gain).astype(BF16)


def _group_norm_channels(y):
    c, t = y.shape
    y3 = y.reshape(c // GROUP, GROUP, t)
    ms = jnp.mean(y3 * y3, axis=1, keepdims=True)
    return (y3 * lax.rsqrt(ms + EPS)).reshape(c, t)


def _bias_table_kernel(ring_ref, out_ref):
    def fields(rows, r0):
        r = lax.broadcasted_iota(jnp.int32, (rows, PAIR_COLS), 0) + r0
        c = lax.broadcasted_iota(jnp.int32, (rows, PAIR_COLS), 1)
        j = (c // CHUNK) % 2
        return c < Q_BLOCK, (r >= j * CHUNK) & (r < j * CHUNK + BAND * CHUNK)

    near_rows = KEY_SPAN - BIAS_FLAT_ROWS
    first_far, in_band_far = fields(BIAS_FLAT_ROWS, 0)
    _, in_band_near = fields(near_rows, BIAS_FLAT_ROWS)
    for p in range(HEAD_PAIRS):
        far = jnp.where(first_far, ring_ref[2 * p:2 * p + 1, 0:1], ring_ref[2 * p + 1:2 * p + 2, 0:1])
        out_ref[p, 0:BIAS_FLAT_ROWS, :] = jnp.where(in_band_far, far * LOG2_E, NEG_INF)
        halves = []
        for e in range(2):
            ring = jnp.broadcast_to(ring_ref[2 * p + e:2 * p + e + 1, :], (near_rows, BIAS_RING))
            halves.append(pltpu.roll(ring, 0, 1, stride=1, stride_axis=0)[:, 0:Q_BLOCK])
        near = jnp.concatenate(halves, axis=1)
        out_ref[p, BIAS_FLAT_ROWS:, :] = jnp.where(in_band_near, near * LOG2_E, NEG_INF)


def _bias_ring(rel_bias):
    far = jnp.broadcast_to(rel_bias[:, -1:], (N_HEADS, BIAS_RING - 2 * Q_BLOCK + 1))
    past = jnp.broadcast_to(rel_bias[:, :1], (N_HEADS, CHUNK))
    return jnp.concatenate([far, past, rel_bias[:, :REL_TABLE - 1]], axis=1)


def _bias_table(rel_bias):
    return pl.pallas_call(
        _bias_table_kernel,
        in_specs=[pl.BlockSpec(memory_space=pltpu.VMEM)],
        out_specs=pl.BlockSpec(memory_space=pltpu.VMEM),
        out_shape=jax.ShapeDtypeStruct((HEAD_PAIRS, KEY_SPAN, PAIR_COLS), F32),
        name="rel_bias_table",
    )(_bias_ring(rel_bias))


def _load_weights_bf16(w_in_hbm, w_out_hbm, w1_hbm, w2_hbm, wt_ref, wo_ref, w1_ref, w2_ref,
                       stage_ref, sem):
    rows, cols = stage_ref.shape[1:]
    chunks = []

    def add(src, dst, transposed=False):
        for r0 in range(0, src.shape[0], rows):
            for c0 in range(0, src.shape[1], cols):
                if transposed:
                    def store(v, r0=r0, c0=c0):
                        dst[c0:c0 + cols, r0:r0 + rows] = v.T.astype(BF16)
                else:
                    def store(v, r0=r0, c0=c0):
                        dst[r0:r0 + rows, c0:c0 + cols] = v.astype(BF16)
                chunks.append((src.at[r0:r0 + rows, c0:c0 + cols], store))

    add(w_in_hbm, wt_ref, transposed=True)
    add(w_out_hbm, wo_ref)
    add(w1_hbm, w1_ref)
    add(w2_hbm, w2_ref)

    def copy(n):
        return pltpu.make_async_copy(chunks[n][0], stage_ref.at[n % 2], sem.at[n % 2])

    copy(0).start()
    for n in range(len(chunks)):
        if n + 1 < len(chunks):
            copy(n + 1).start()
        copy(n).wait()
        chunks[n][1](stage_ref[n % 2])


def _layer_kernel(tiles_per_seq, x_ref, xprev_ref, g1_ref, colp_ref, bias_ref, aog_ref, g2_ref,
                  w_in_hbm, w_out_hbm, w1_hbm, w2_hbm, o_ref,
                  wt_ref, wo_ref, w1_ref, w2_ref, stage_ref, sem,
                  carry_ref, xn_ref, ymix_ref, qt_ref, kw_ref, vtw_ref, pen_ref, yt_ref, s_ref,
                  xn2_ref, hid_ref):
    t = pl.program_id(0)
    n_tiles = pl.num_programs(0) - 1
    first_tile = lax.rem(jnp.minimum(t, n_tiles - 1), tiles_per_seq) == 0

    @pl.when(t == 0)
    def _():
        _load_weights_bf16(w_in_hbm, w_out_hbm, w1_hbm, w2_hbm, wt_ref, wo_ref, w1_ref, w2_ref,
                           stage_ref, sem)
        kw_ref[...] = jnp.zeros_like(kw_ref)
        vtw_ref[...] = jnp.zeros_like(vtw_ref)
        ymix_ref[...] = jnp.zeros_like(ymix_ref)

    @pl.when(first_tile)
    def _():
        carry_ref[...] = jnp.zeros_like(carry_ref)

    for j in range(D_MODEL // MLP_COLS):
        cols = slice(j * MLP_COLS, (j + 1) * MLP_COLS)
        o_ref[:, cols] = xprev_ref[:, cols] + jnp.dot(ymix_ref[...], wo_ref[:, cols],
                                                      preferred_element_type=F32)
    xn_ref[...] = _rms_norm_bf16(x_ref[...], g1_ref[...])

    def mlp_prologue():
        xn2_ref[...] = _rms_norm_bf16(o_ref[...], g2_ref[...])

    def mlp_up(j):
        cols = slice(j * MLP_COLS, (j + 1) * MLP_COLS)
        hid = jnp.maximum(jnp.dot(xn2_ref[...], w1_ref[:, cols], preferred_element_type=F32), 0.0)
        hid_ref[:, cols] = (hid * hid).astype(BF16)

    def mlp_down(j):
        cols = slice(j * MLP_COLS, (j + 1) * MLP_COLS)
        o_ref[:, cols] += jnp.dot(hid_ref[...], w2_ref[:, cols], preferred_element_type=F32)

    mlp_pieces = [(mlp_up, j) for j in range(D_FF // MLP_COLS)]
    mlp_pieces += [(mlp_down, j) for j in range(D_MODEL // MLP_COLS)]

    def run_mlp_piece():
        fn, j = mlp_pieces.pop(0)
        fn(j)

    def proj_t(j):
        return lax.dot_general(wt_ref[j * CONV_WIDTH:(j + 1) * CONV_WIDTH, :], xn_ref[...],
                               NT_DIMS, preferred_element_type=F32)

    def col(j):
        return colp_ref[:, j:j + 1]

    kw_ref[0:SEQ_TILE, :] = kw_ref[SEQ_TILE:, :]
    vtw_ref[:, 0:SEQ_TILE] = vtw_ref[:, SEQ_TILE:]
    pen_lane = lax.broadcasted_iota(jnp.int32, (SEQ_TILE, LANES), 1) == 0
    pen_ref[0:SEQ_TILE, :] = jnp.where(pen_lane & first_tile, NEG_INF, 0.0).astype(BF16)
    pen_ref[SEQ_TILE:, :] = jnp.zeros((SEQ_TILE, LANES), BF16)

    gate_c = proj_t(1)
    gate_h = proj_t(2)
    gate_b = proj_t(0)
    mlp_prologue()
    k_t = proj_t(4)
    u = gate_c * gate_h
    ext = jnp.concatenate([carry_ref[...], u], axis=1)
    u1 = pltpu.roll(ext, 1, 1)[:, LANES:]
    u2 = pltpu.roll(ext, 2, 1)[:, LANES:]
    carry_ref[...] = u[:, SEQ_TILE - LANES:]
    y = col(COL_CONV_W) * u2 + col(COL_CONV_W + 1) * u1 + col(COL_CONV_W + 2) * u
    yc = gate_b * (y + col(COL_CONV_B))
    ymix_ref[:, 0:CONV_WIDTH] = (_group_norm_channels(yc) * col(COL_CONV_G)).T.astype(BF16)
    q_t = proj_t(3)
    kw_ref[SEQ_TILE:, :] = (_group_norm_channels(k_t) * col(COL_K_G)).T.astype(BF16)
    v_t = proj_t(5)
    for _ in range(PIECES_BEFORE_ATTENTION):
        run_mlp_piece()
    qt_ref[...] = (_group_norm_channels(q_t) * col(COL_Q_G)
                   * (HEAD_DIM ** -0.5 * LOG2_E)).astype(BF16)
    vtw_ref[:, SEQ_TILE:] = v_t.astype(BF16)

    first_head = lax.broadcasted_iota(jnp.int32, (LANES, Q_BLOCK), 0) < HEAD_DIM
    pen_rows = (lax.broadcasted_iota(jnp.int32, (LANES, PAIR_COLS), 0) == 0).astype(BF16)
    blocks = [(qb * Q_BLOCK, p) for qb in range(Q_BLOCKS) for p in range(HEAD_PAIRS)]

    def scores(n):
        k0, p = blocks[n]
        ch = slice(p * LANES, (p + 1) * LANES)
        qt = qt_ref[ch, k0:k0 + Q_BLOCK]
        zero = jnp.zeros_like(qt)
        rhs = jnp.concatenate([jnp.where(first_head, qt, zero),
                               jnp.where(first_head, zero, qt)], axis=1)
        rhs = jnp.concatenate([rhs, pen_rows], axis=0)
        lhs = jnp.concatenate([kw_ref[k0:k0 + KEY_SPAN, ch], pen_ref[k0:k0 + KEY_SPAN, :]],
                              axis=1)
        s_ref[n % 2] = jnp.dot(lhs, rhs, preferred_element_type=F32) + bias_ref[p]

    def attend(n):
        k0, p = blocks[n]
        ch = slice(p * LANES, (p + 1) * LANES)
        s = s_ref[n % 2]
        pr = jnp.exp2(s - jnp.max(s, axis=0, keepdims=True))
        denom = jnp.sum(pr, axis=0, keepdims=True)
        ot = jnp.dot(vtw_ref[ch, k0:k0 + KEY_SPAN], pr.astype(BF16),
                     preferred_element_type=F32) / denom
        yt_ref[p * LANES:p * LANES + HEAD_DIM, k0:k0 + Q_BLOCK] = ot[0:HEAD_DIM, 0:Q_BLOCK]
        yt_ref[p * LANES + HEAD_DIM:(p + 1) * LANES, k0:k0 + Q_BLOCK] = ot[HEAD_DIM:, Q_BLOCK:]

    scores(0)
    for n in range(len(blocks)):
        if n + 1 < len(blocks):
            scores(n + 1)
        attend(n)
        run_mlp_piece()
    ymix_ref[:, CONV_WIDTH:] = (_group_norm_channels(yt_ref[...]) * aog_ref[...]).T.astype(BF16)
    for _ in range(PIECES_CLOSING):
        run_mlp_piece()
    assert not mlp_pieces


def _layer(x, g1, col_params, bias, aog_col, g2, w_in, w_out, w1, w2):
    bsz, seq, _ = x.shape
    in_hbm = pl.BlockSpec(memory_space=pl.ANY)
    tiles_per_seq = seq // SEQ_TILE
    n_tiles = bsz * tiles_per_seq

    def tile_of(t):
        t = jnp.minimum(t, n_tiles - 1)
        return t // tiles_per_seq, t % tiles_per_seq, 0

    def out_tile_of(t):
        return tile_of(jnp.maximum(t - 1, 0))

    return pl.pallas_call(
        functools.partial(_layer_kernel, tiles_per_seq),
        grid=(n_tiles + 1,),
        in_specs=[pl.BlockSpec((None, SEQ_TILE, D_MODEL), tile_of),
                  pl.BlockSpec((None, SEQ_TILE, D_MODEL), out_tile_of),
                  _resident((1, D_MODEL)), _resident((CONV_WIDTH, N_COL_PARAMS)),
                  _resident((HEAD_PAIRS, KEY_SPAN, PAIR_COLS)), _resident((ATTN_WIDTH, 1)),
                  _resident((1, D_MODEL)), in_hbm, in_hbm, in_hbm, in_hbm],
        out_specs=pl.BlockSpec((None, SEQ_TILE, D_MODEL), out_tile_of),
        out_shape=jax.ShapeDtypeStruct(x.shape, F32),
        scratch_shapes=[pltpu.VMEM(w_in.shape[::-1], BF16),
                        pltpu.VMEM(w_out.shape, BF16),
                        pltpu.VMEM(w1.shape, BF16),
                        pltpu.VMEM(w2.shape, BF16),
                        pltpu.VMEM((2,) + WEIGHT_CHUNK, F32),
                        pltpu.SemaphoreType.DMA((2,)),
                        pltpu.VMEM((CONV_WIDTH, LANES), F32),
                        pltpu.VMEM((SEQ_TILE, D_MODEL), BF16),
                        pltpu.VMEM((SEQ_TILE, D_MODEL), BF16),
                        pltpu.VMEM((ATTN_WIDTH, SEQ_TILE), BF16),
                        pltpu.VMEM((2 * SEQ_TILE, ATTN_WIDTH), BF16),
                        pltpu.VMEM((ATTN_WIDTH, 2 * SEQ_TILE), BF16),
                        pltpu.VMEM((2 * SEQ_TILE, LANES), BF16),
                        pltpu.VMEM((ATTN_WIDTH, SEQ_TILE), F32),
                        pltpu.VMEM((2, KEY_SPAN, PAIR_COLS), F32),
                        pltpu.VMEM((SEQ_TILE, D_MODEL), BF16),
                        pltpu.VMEM((SEQ_TILE, D_FF), BF16)],
        compiler_params=pltpu.CompilerParams(
            dimension_semantics=("arbitrary",), vmem_limit_bytes=VMEM_LIMIT),
        name="layer",
    )(x, x, g1, col_params, bias, aog_col, g2, w_in, w_out, w1, w2)


def kernel(x, norm_mix_g, w_in, conv_w, conv_b, q_norm_g, k_norm_g, rel_bias, conv_out_g,
           attn_out_g, w_out, norm_mlp_g, w_mlp_in, w_mlp_out):
    depth = w_in.shape[0]
    for l in range(depth):
        col_params = jnp.stack(
            [conv_w[l, 0], conv_w[l, 1], conv_w[l, 2], conv_b[l], conv_out_g[l],
             jnp.tile(q_norm_g[l], N_HEADS), jnp.tile(k_norm_g[l], N_HEADS),
             jnp.zeros((CONV_WIDTH,), F32)], axis=1)
        x = _layer(x, norm_mix_g[l][None], col_params, _bias_table(rel_bias[l]),
                   attn_out_g[l][:, None], norm_mlp_g[l][None],
                   w_in[l], w_out[l], w_mlp_in[l], w_mlp_out[l])
    return x
```

```python
import functools
import math

import jax
import jax.numpy as jnp
from jax import lax
from jax.experimental import pallas as pl
from jax.experimental.pallas import tpu as pltpu

D_MODEL = 1024
CHUNK = 64
LEFT_CHUNKS = 8
BAND = LEFT_CHUNKS + 1
CONV_WIDTH = D_MODEL // 2
GROUP = 64
HEAD_DIM = 64
N_HEADS = (D_MODEL - CONV_WIDTH) // HEAD_DIM
ATTN_WIDTH = N_HEADS * HEAD_DIM
REL_CLIP = 128
REL_TABLE = (CHUNK - 1) + REL_CLIP + 1
D_FF = 4 * D_MODEL
EPS = 1e-6
NEG_INF = -1e30
LOG2_E = math.log2(math.e)

LANES = 128
SEQ_TILE = LEFT_CHUNKS * CHUNK
Q_BLOCK = 2 * CHUNK
Q_BLOCKS = SEQ_TILE // Q_BLOCK
KEY_SPAN = (BAND + 1) * CHUNK
HEAD_PAIRS = N_HEADS * HEAD_DIM // LANES
PAIR_COLS = 2 * Q_BLOCK
BIAS_FLAT_ROWS = KEY_SPAN - 2 * Q_BLOCK
BIAS_RING = 4 * LANES
MLP_COLS = 256
MLP_PIECES = (D_FF + D_MODEL) // MLP_COLS
WEIGHT_CHUNK = (512, 1024)
PIECES_BEFORE_ATTENTION = 2
PIECES_CLOSING = MLP_PIECES - PIECES_BEFORE_ATTENTION - SEQ_TILE // Q_BLOCK * HEAD_PAIRS
VMEM_LIMIT = 60000 * 1024

COL_CONV_W, COL_CONV_B, COL_CONV_G, COL_Q_G, COL_K_G, N_COL_PARAMS = 0, 3, 4, 5, 6, 8

F32 = jnp.float32
BF16 = jnp.bfloat16
NT_DIMS = (((1,), (1,)), ((), ()))


def _resident(shape):
    return pl.BlockSpec(shape, lambda *_: (0,) * len(shape), pipeline_mode=pl.Buffered(1))


def _rms_norm_bf16(x, gain):
    ms = jnp.mean(x * x, axis=-1, keepdims=True)
    return (x * lax.rsqrt(ms + EPS) * gain).astype(BF16)


def _group_norm_channels(y):
    c, t = y.shape
    y3 = y.reshape(c // GROUP, GROUP, t)
    ms = jnp.mean(y3 * y3, axis=1, keepdims=True)
    return (y3 * lax.rsqrt(ms + EPS)).reshape(c, t)


def _bias_table_kernel(ring_ref, out_ref):
    def fields(rows, r0):
        r = lax.broadcasted_iota(jnp.int32, (rows, PAIR_COLS), 0) + r0
        c = lax.broadcasted_iota(jnp.int32, (rows, PAIR_COLS), 1)
        j = (c // CHUNK) % 2
        return c < Q_BLOCK, (r >= j * CHUNK) & (r < j * CHUNK + BAND * CHUNK)

    near_rows = KEY_SPAN - BIAS_FLAT_ROWS
    first_far, in_band_far = fields(BIAS_FLAT_ROWS, 0)
    _, in_band_near = fields(near_rows, BIAS_FLAT_ROWS)
    for p in range(HEAD_PAIRS):
        far = jnp.where(first_far, ring_ref[2 * p:2 * p + 1, 0:1], ring_ref[2 * p + 1:2 * p + 2, 0:1])
        out_ref[p, 0:BIAS_FLAT_ROWS, :] = jnp.where(in_band_far, far * LOG2_E, NEG_INF)
        halves = []
        for e in range(2):
            ring = jnp.broadcast_to(ring_ref[2 * p + e:2 * p + e + 1, :], (near_rows, BIAS_RING))
            halves.append(pltpu.roll(ring, 0, 1, stride=1, stride_axis=0)[:, 0:Q_BLOCK])
        near = jnp.concatenate(halves, axis=1)
        out_ref[p, BIAS_FLAT_ROWS:, :] = jnp.where(in_band_near, near * LOG2_E, NEG_INF)


def _bias_ring(rel_bias):
    far = jnp.broadcast_to(rel_bias[:, -1:], (N_HEADS, BIAS_RING - 2 * Q_BLOCK + 1))
    past = jnp.broadcast_to(rel_bias[:, :1], (N_HEADS, CHUNK))
    return jnp.concatenate([far, past, rel_bias[:, :REL_TABLE - 1]], axis=1)


def _bias_table(rel_bias):
    return pl.pallas_call(
        _bias_table_kernel,
        in_specs=[pl.BlockSpec(memory_space=pltpu.VMEM)],
        out_specs=pl.BlockSpec(memory_space=pltpu.VMEM),
        out_shape=jax.ShapeDtypeStruct((HEAD_PAIRS, KEY_SPAN, PAIR_COLS), F32),
        name="rel_bias_table",
    )(_bias_ring(rel_bias))


def _load_weights_bf16(w_in_hbm, w_out_hbm, w1_hbm, w2_hbm, wt_ref, wo_ref, w1_ref, w2_ref,
                       stage_ref, sem):
    rows, cols = stage_ref.shape[1:]
    chunks = []

    def add(src, dst, transposed=False):
        for r0 in range(0, src.shape[0], rows):
            for c0 in range(0, src.shape[1], cols):
                if transposed:
                    def store(v, r0=r0, c0=c0):
                        dst[c0:c0 + cols, r0:r0 + rows] = v.T.astype(BF16)
                else:
                    def store(v, r0=r0, c0=c0):
                        dst[r0:r0 + rows, c0:c0 + cols] = v.astype(BF16)
                chunks.append((src.at[r0:r0 + rows, c0:c0 + cols], store))

    add(w_in_hbm, wt_ref, transposed=True)
    add(w_out_hbm, wo_ref)
    add(w1_hbm, w1_ref)
    add(w2_hbm, w2_ref)

    def copy(n):
        return pltpu.make_async_copy(chunks[n][0], stage_ref.at[n % 2], sem.at[n % 2])

    copy(0).start()
    for n in range(len(chunks)):
        if n + 1 < len(chunks):
            copy(n + 1).start()
        copy(n).wait()
        chunks[n][1](stage_ref[n % 2])


def _layer_kernel(tiles_per_seq, x_ref, g1_ref, colp_ref, bias_ref, aog_ref, g2_ref,
                  w_in_hbm, w_out_hbm, w1_hbm, w2_hbm, o_ref,
                  wt_ref, wo_ref, w1_ref, w2_ref, stage_ref, sem,
                  carry_ref, xn_ref, yc_ref, qt_ref, kw_ref, vtw_ref, pen_ref, yt_ref, s_ref,
                  hc_ref, yn_ref, xn2_ref, hid_ref):
    t = pl.program_id(0)
    n_tiles = pl.num_programs(0) - 1
    first_tile = lax.rem(jnp.minimum(t, n_tiles - 1), tiles_per_seq) == 0

    @pl.when(t == 0)
    def _():
        _load_weights_bf16(w_in_hbm, w_out_hbm, w1_hbm, w2_hbm, wt_ref, wo_ref, w1_ref, w2_ref,
                           stage_ref, sem)
        kw_ref[...] = jnp.zeros_like(kw_ref)
        vtw_ref[...] = jnp.zeros_like(vtw_ref)
        hc_ref[...] = jnp.zeros_like(hc_ref)
        yn_ref[...] = jnp.zeros_like(yn_ref)

    @pl.when(first_tile)
    def _():
        carry_ref[...] = jnp.zeros_like(carry_ref)

    o_ref[...] = hc_ref[...] + jnp.dot(yn_ref[...], wo_ref[CONV_WIDTH:, :],
                                       preferred_element_type=F32)
    xn_ref[...] = _rms_norm_bf16(x_ref[...], g1_ref[...])

    def mlp_prologue():
        xn2_ref[...] = _rms_norm_bf16(o_ref[...], g2_ref[...])

    def mlp_up(j):
        cols = slice(j * MLP_COLS, (j + 1) * MLP_COLS)
        hid = jnp.maximum(jnp.dot(xn2_ref[...], w1_ref[:, cols], preferred_element_type=F32), 0.0)
        hid_ref[:, cols] = (hid * hid).astype(BF16)

    def mlp_down(j):
        cols = slice(j * MLP_COLS, (j + 1) * MLP_COLS)
        o_ref[:, cols] += jnp.dot(hid_ref[...], w2_ref[:, cols], preferred_element_type=F32)

    mlp_pieces = [(mlp_up, j) for j in range(D_FF // MLP_COLS)]
    mlp_pieces += [(mlp_down, j) for j in range(D_MODEL // MLP_COLS)]

    def run_mlp_piece():
        fn, j = mlp_pieces.pop(0)
        fn(j)

    def proj_t(j):
        return lax.dot_general(wt_ref[j * CONV_WIDTH:(j + 1) * CONV_WIDTH, :], xn_ref[...],
                               NT_DIMS, preferred_element_type=F32)

    def col(j):
        return colp_ref[:, j:j + 1]

    kw_ref[0:SEQ_TILE, :] = kw_ref[SEQ_TILE:, :]
    vtw_ref[:, 0:SEQ_TILE] = vtw_ref[:, SEQ_TILE:]
    pen_lane = lax.broadcasted_iota(jnp.int32, (SEQ_TILE, LANES), 1) == 0
    pen_ref[0:SEQ_TILE, :] = jnp.where(pen_lane & first_tile, NEG_INF, 0.0).astype(BF16)
    pen_ref[SEQ_TILE:, :] = jnp.zeros((SEQ_TILE, LANES), BF16)

    gate_c = proj_t(1)
    gate_h = proj_t(2)
    gate_b = proj_t(0)
    mlp_prologue()
    k_t = proj_t(4)
    u = gate_c * gate_h
    ext = jnp.concatenate([carry_ref[...], u], axis=1)
    u1 = pltpu.roll(ext, 1, 1)[:, LANES:]
    u2 = pltpu.roll(ext, 2, 1)[:, LANES:]
    carry_ref[...] = u[:, SEQ_TILE - LANES:]
    y = col(COL_CONV_W) * u2 + col(COL_CONV_W + 1) * u1 + col(COL_CONV_W + 2) * u
    yc = gate_b * (y + col(COL_CONV_B))
    yc_ref[...] = (_group_norm_channels(yc) * col(COL_CONV_G)).T.astype(BF16)
    q_t = proj_t(3)
    kw_ref[SEQ_TILE:, :] = (_group_norm_channels(k_t) * col(COL_K_G)).T.astype(BF16)
    v_t = proj_t(5)
    for _ in range(PIECES_BEFORE_ATTENTION):
        run_mlp_piece()
    qt_ref[...] = (_group_norm_channels(q_t) * col(COL_Q_G)
                   * (HEAD_DIM ** -0.5 * LOG2_E)).astype(BF16)
    vtw_ref[:, SEQ_TILE:] = v_t.astype(BF16)

    first_head = lax.broadcasted_iota(jnp.int32, (LANES, Q_BLOCK), 0) < HEAD_DIM
    pen_rows = (lax.broadcasted_iota(jnp.int32, (LANES, PAIR_COLS), 0) == 0).astype(BF16)
    blocks = [(qb * Q_BLOCK, p) for qb in range(Q_BLOCKS) for p in range(HEAD_PAIRS)]

    def scores(n):
        k0, p = blocks[n]
        ch = slice(p * LANES, (p + 1) * LANES)
        qt = qt_ref[ch, k0:k0 + Q_BLOCK]
        zero = jnp.zeros_like(qt)
        rhs = jnp.concatenate([jnp.where(first_head, qt, zero),
                               jnp.where(first_head, zero, qt)], axis=1)
        rhs = jnp.concatenate([rhs, pen_rows], axis=0)
        lhs = jnp.concatenate([kw_ref[k0:k0 + KEY_SPAN, ch], pen_ref[k0:k0 + KEY_SPAN, :]],
                              axis=1)
        s_ref[n % 2] = jnp.dot(lhs, rhs, preferred_element_type=F32) + bias_ref[p]

    def attend(n):
        k0, p = blocks[n]
        ch = slice(p * LANES, (p + 1) * LANES)
        s = s_ref[n % 2]
        pr = jnp.exp2(s - jnp.max(s, axis=0, keepdims=True))
        denom = jnp.sum(pr, axis=0, keepdims=True)
        ot = jnp.dot(vtw_ref[ch, k0:k0 + KEY_SPAN], pr.astype(BF16),
                     preferred_element_type=F32) / denom
        yt_ref[p * LANES:p * LANES + HEAD_DIM, k0:k0 + Q_BLOCK] = ot[0:HEAD_DIM, 0:Q_BLOCK]
        yt_ref[p * LANES + HEAD_DIM:(p + 1) * LANES, k0:k0 + Q_BLOCK] = ot[HEAD_DIM:, Q_BLOCK:]

    def conv_half_out_proj(c):
        cols = slice(c * MLP_COLS, (c + 1) * MLP_COLS)
        hc_ref[:, cols] = x_ref[:, cols] + jnp.dot(yc_ref[...], wo_ref[0:CONV_WIDTH, cols],
                                                   preferred_element_type=F32)

    scores(0)
    for n in range(len(blocks)):
        if n + 1 < len(blocks):
            scores(n + 1)
        attend(n)
        if n < D_MODEL // MLP_COLS:
            conv_half_out_proj(n)
        run_mlp_piece()
    yn_ref[...] = (_group_norm_channels(yt_ref[...]) * aog_ref[...]).T.astype(BF16)
    for _ in range(PIECES_CLOSING):
        run_mlp_piece()
    assert not mlp_pieces


def _layer(x, g1, col_params, bias, aog_col, g2, w_in, w_out, w1, w2):
    bsz, seq, _ = x.shape
    in_hbm = pl.BlockSpec(memory_space=pl.ANY)
    tiles_per_seq = seq // SEQ_TILE
    n_tiles = bsz * tiles_per_seq

    def tile_of(t):
        t = jnp.minimum(t, n_tiles - 1)
        return t // tiles_per_seq, t % tiles_per_seq, 0

    def out_tile_of(t):
        return tile_of(jnp.maximum(t - 1, 0))

    return pl.pallas_call(
        functools.partial(_layer_kernel, tiles_per_seq),
        grid=(n_tiles + 1,),
        in_specs=[pl.BlockSpec((None, SEQ_TILE, D_MODEL), tile_of),
                  _resident((1, D_MODEL)), _resident((CONV_WIDTH, N_COL_PARAMS)),
                  _resident((HEAD_PAIRS, KEY_SPAN, PAIR_COLS)), _resident((ATTN_WIDTH, 1)),
                  _resident((1, D_MODEL)), in_hbm, in_hbm, in_hbm, in_hbm],
        out_specs=pl.BlockSpec((None, SEQ_TILE, D_MODEL), out_tile_of),
        out_shape=jax.ShapeDtypeStruct(x.shape, F32),
        scratch_shapes=[pltpu.VMEM(w_in.shape[::-1], BF16),
                        pltpu.VMEM(w_out.shape, BF16),
                        pltpu.VMEM(w1.shape, BF16),
                        pltpu.VMEM(w2.shape, BF16),
                        pltpu.VMEM((2,) + WEIGHT_CHUNK, F32),
                        pltpu.SemaphoreType.DMA((2,)),
                        pltpu.VMEM((CONV_WIDTH, LANES), F32),
                        pltpu.VMEM((SEQ_TILE, D_MODEL), BF16),
                        pltpu.VMEM((SEQ_TILE, CONV_WIDTH), BF16),
                        pltpu.VMEM((ATTN_WIDTH, SEQ_TILE), BF16),
                        pltpu.VMEM((2 * SEQ_TILE, ATTN_WIDTH), BF16),
                        pltpu.VMEM((ATTN_WIDTH, 2 * SEQ_TILE), BF16),
                        pltpu.VMEM((2 * SEQ_TILE, LANES), BF16),
                        pltpu.VMEM((ATTN_WIDTH, SEQ_TILE), F32),
                        pltpu.VMEM((2, KEY_SPAN, PAIR_COLS), F32),
                        pltpu.VMEM((SEQ_TILE, D_MODEL), F32),
                        pltpu.VMEM((SEQ_TILE, ATTN_WIDTH), BF16),
                        pltpu.VMEM((SEQ_TILE, D_MODEL), BF16),
                        pltpu.VMEM((SEQ_TILE, D_FF), BF16)],
        compiler_params=pltpu.CompilerParams(
            dimension_semantics=("arbitrary",), vmem_limit_bytes=VMEM_LIMIT),
        name="layer",
    )(x, g1, col_params, bias, aog_col, g2, w_in, w_out, w1, w2)


def kernel(x, norm_mix_g, w_in, conv_w, conv_b, q_norm_g, k_norm_g, rel_bias, conv_out_g,
           attn_out_g, w_out, norm_mlp_g, w_mlp_in, w_mlp_out):
    depth = w_in.shape[0]
    for l in range(depth):
        col_params = jnp.stack(
            [conv_w[l, 0], conv_w[l, 1], conv_w[l, 2], conv_b[l], conv_out_g[l],
             jnp.tile(q_norm_g[l], N_HEADS), jnp.tile(k_norm_g[l], N_HEADS),
             jnp.zeros((CONV_WIDTH,), F32)], axis=1)
        x = _layer(x, norm_mix_g[l][None], col_params, _bias_table(rel_bias[l]),
                   attn_out_g[l][:, None], norm_mlp_g[l][None],
                   w_in[l], w_out[l], w_mlp_in[l], w_mlp_out[l])
    return x
```

```python
import functools
import math

import jax
import jax.numpy as jnp
from jax import lax
from jax.experimental import pallas as pl
from jax.experimental.pallas import tpu as pltpu

D_MODEL = 1024
CHUNK = 64
LEFT_CHUNKS = 8
BAND = LEFT_CHUNKS + 1
CONV_WIDTH = D_MODEL // 2
GROUP = 64
HEAD_DIM = 64
N_HEADS = (D_MODEL - CONV_WIDTH) // HEAD_DIM
ATTN_WIDTH = N_HEADS * HEAD_DIM
REL_CLIP = 128
REL_TABLE = (CHUNK - 1) + REL_CLIP + 1
D_FF = 4 * D_MODEL
EPS = 1e-6
NEG_INF = -1e30
LOG2_E = math.log2(math.e)

LANES = 128
SEQ_TILE = LEFT_CHUNKS * CHUNK
Q_BLOCK = 2 * CHUNK
Q_BLOCKS = SEQ_TILE // Q_BLOCK
KEY_SPAN = (BAND + 1) * CHUNK
HEAD_PAIRS = N_HEADS * HEAD_DIM // LANES
PAIR_COLS = 2 * Q_BLOCK
BIAS_FLAT_ROWS = KEY_SPAN - 2 * Q_BLOCK
BIAS_RING = 4 * LANES
MLP_COLS = 256
MLP_PIECES = (D_FF + D_MODEL) // MLP_COLS
WEIGHT_CHUNK = (512, 1024)
PIECES_BEFORE_ATTENTION = 2
PIECES_CLOSING = MLP_PIECES - PIECES_BEFORE_ATTENTION - SEQ_TILE // Q_BLOCK * HEAD_PAIRS
VMEM_LIMIT = 60000 * 1024

COL_CONV_W, COL_CONV_B, COL_CONV_G, COL_Q_G, COL_K_G, N_COL_PARAMS = 0, 3, 4, 5, 6, 8
ROW_MIX_G, ROW_MLP_G, N_ROW_PARAMS = 0, 1, 8

F32 = jnp.float32
BF16 = jnp.bfloat16
NT_DIMS = (((1,), (1,)), ((), ()))


def _resident(shape):
    return pl.BlockSpec(shape, lambda *_: (0,) * len(shape), pipeline_mode=pl.Buffered(1))


def _rms_norm_bf16(x, gain):
    ms = jnp.mean(x * x, axis=-1, keepdims=True)
    return (x * lax.rsqrt(ms + EPS) * gain).astype(BF16)


def _group_norm_channels(y):
    c, t = y.shape
    y3 = y.reshape(c // GROUP, GROUP, t)
    ms = jnp.mean(y3 * y3, axis=1, keepdims=True)
    return (y3 * lax.rsqrt(ms + EPS)).reshape(c, t)


def _bias_table_kernel(ring_ref, out_ref):
    def fields(rows, r0):
        r = lax.broadcasted_iota(jnp.int32, (rows, PAIR_COLS), 0) + r0
        c = lax.broadcasted_iota(jnp.int32, (rows, PAIR_COLS), 1)
        j = (c // CHUNK) % 2
        return c < Q_BLOCK, (r >= j * CHUNK) & (r < j * CHUNK + BAND * CHUNK)

    near_rows = KEY_SPAN - BIAS_FLAT_ROWS
    first_far, in_band_far = fields(BIAS_FLAT_ROWS, 0)
    _, in_band_near = fields(near_rows, BIAS_FLAT_ROWS)
    for p in range(HEAD_PAIRS):
        far = jnp.where(first_far, ring_ref[2 * p:2 * p + 1, 0:1], ring_ref[2 * p + 1:2 * p + 2, 0:1])
        out_ref[p, 0:BIAS_FLAT_ROWS, :] = jnp.where(in_band_far, far * LOG2_E, NEG_INF)
        halves = []
        for e in range(2):
            ring = jnp.broadcast_to(ring_ref[2 * p + e:2 * p + e + 1, :], (near_rows, BIAS_RING))
            halves.append(pltpu.roll(ring, 0, 1, stride=1, stride_axis=0)[:, 0:Q_BLOCK])
        near = jnp.concatenate(halves, axis=1)
        out_ref[p, BIAS_FLAT_ROWS:, :] = jnp.where(in_band_near, near * LOG2_E, NEG_INF)


def _bias_ring(rel_bias):
    far = jnp.broadcast_to(rel_bias[:, -1:], (N_HEADS, BIAS_RING - 2 * Q_BLOCK + 1))
    past = jnp.broadcast_to(rel_bias[:, :1], (N_HEADS, CHUNK))
    return jnp.concatenate([far, past, rel_bias[:, :REL_TABLE - 1]], axis=1)


def _bias_table(rel_bias):
    return pl.pallas_call(
        _bias_table_kernel,
        in_specs=[pl.BlockSpec(memory_space=pltpu.VMEM)],
        out_specs=pl.BlockSpec(memory_space=pltpu.VMEM),
        out_shape=jax.ShapeDtypeStruct((HEAD_PAIRS, KEY_SPAN, PAIR_COLS), F32),
        name="rel_bias_table",
    )(_bias_ring(rel_bias))


def _load_weights_bf16(w_in_hbm, w_out_hbm, w1_hbm, w2_hbm, wt_ref, wo_ref, w1_ref, w2_ref,
                       stage_ref, sem):
    rows, cols = stage_ref.shape[1:]
    chunks = []

    def add(src, dst, transposed=False):
        for r0 in range(0, src.shape[0], rows):
            for c0 in range(0, src.shape[1], cols):
                if transposed:
                    def store(v, r0=r0, c0=c0):
                        dst[c0:c0 + cols, r0:r0 + rows] = v.T.astype(BF16)
                else:
                    def store(v, r0=r0, c0=c0):
                        dst[r0:r0 + rows, c0:c0 + cols] = v.astype(BF16)
                chunks.append((src.at[r0:r0 + rows, c0:c0 + cols], store))

    add(w_in_hbm, wt_ref, transposed=True)
    add(w_out_hbm, wo_ref)
    add(w1_hbm, w1_ref)
    add(w2_hbm, w2_ref)

    def copy(n):
        return pltpu.make_async_copy(chunks[n][0], stage_ref.at[n % 2], sem.at[n % 2])

    copy(0).start()
    for n in range(len(chunks)):
        if n + 1 < len(chunks):
            copy(n + 1).start()
        copy(n).wait()
        chunks[n][1](stage_ref[n % 2])


def _layer_kernel(tiles_per_seq, x_ref, rowp_ref, colp_ref, bias_ref, aog_ref,
                  w_in_hbm, w_out_hbm, w1_hbm, w2_hbm, o_ref,
                  wt_ref, wo_ref, w1_ref, w2_ref, stage_ref, sem,
                  carry_ref, xn_ref, yc_ref, qt_ref, kw_ref, vtw_ref, pen_ref, yt_ref, s_ref,
                  hc_ref, yn_ref, xn2_ref, hid_ref):
    t = pl.program_id(0)
    n_tiles = pl.num_programs(0) - 1
    first_tile = lax.rem(jnp.minimum(t, n_tiles - 1), tiles_per_seq) == 0

    @pl.when(t == 0)
    def _():
        _load_weights_bf16(w_in_hbm, w_out_hbm, w1_hbm, w2_hbm, wt_ref, wo_ref, w1_ref, w2_ref,
                           stage_ref, sem)
        kw_ref[...] = jnp.zeros_like(kw_ref)
        vtw_ref[...] = jnp.zeros_like(vtw_ref)
        hc_ref[...] = jnp.zeros_like(hc_ref)
        yn_ref[...] = jnp.zeros_like(yn_ref)

    @pl.when(first_tile)
    def _():
        carry_ref[...] = jnp.zeros_like(carry_ref)

    o_ref[...] = hc_ref[...] + jnp.dot(yn_ref[...], wo_ref[CONV_WIDTH:, :],
                                       preferred_element_type=F32)
    xn_ref[...] = _rms_norm_bf16(x_ref[...], rowp_ref[ROW_MIX_G:ROW_MIX_G + 1, :])

    def mlp_prologue():
        xn2_ref[...] = _rms_norm_bf16(o_ref[...], rowp_ref[ROW_MLP_G:ROW_MLP_G + 1, :])

    def mlp_up(j):
        cols = slice(j * MLP_COLS, (j + 1) * MLP_COLS)
        hid = jnp.maximum(jnp.dot(xn2_ref[...], w1_ref[:, cols], preferred_element_type=F32), 0.0)
        hid_ref[:, cols] = (hid * hid).astype(BF16)

    def mlp_down(j):
        cols = slice(j * MLP_COLS, (j + 1) * MLP_COLS)
        o_ref[:, cols] += jnp.dot(hid_ref[...], w2_ref[:, cols], preferred_element_type=F32)

    mlp_pieces = [(mlp_up, j) for j in range(D_FF // MLP_COLS)]
    mlp_pieces += [(mlp_down, j) for j in range(D_MODEL // MLP_COLS)]

    def run_mlp_piece():
        fn, j = mlp_pieces.pop(0)
        fn(j)

    def proj_t(j):
        return lax.dot_general(wt_ref[j * CONV_WIDTH:(j + 1) * CONV_WIDTH, :], xn_ref[...],
                               NT_DIMS, preferred_element_type=F32)

    def col(j):
        return colp_ref[:, j:j + 1]

    kw_ref[0:SEQ_TILE, :] = kw_ref[SEQ_TILE:, :]
    vtw_ref[:, 0:SEQ_TILE] = vtw_ref[:, SEQ_TILE:]
    pen_lane = lax.broadcasted_iota(jnp.int32, (SEQ_TILE, LANES), 1) == 0
    pen_ref[0:SEQ_TILE, :] = jnp.where(pen_lane & first_tile, NEG_INF, 0.0).astype(BF16)
    pen_ref[SEQ_TILE:, :] = jnp.zeros((SEQ_TILE, LANES), BF16)

    gate_c = proj_t(1)
    gate_h = proj_t(2)
    gate_b = proj_t(0)
    mlp_prologue()
    k_t = proj_t(4)
    u = gate_c * gate_h
    ext = jnp.concatenate([carry_ref[...], u], axis=1)
    u1 = pltpu.roll(ext, 1, 1)[:, LANES:]
    u2 = pltpu.roll(ext, 2, 1)[:, LANES:]
    carry_ref[...] = u[:, SEQ_TILE - LANES:]
    y = col(COL_CONV_W) * u2 + col(COL_CONV_W + 1) * u1 + col(COL_CONV_W + 2) * u
    yc = gate_b * (y + col(COL_CONV_B))
    yc_ref[...] = (_group_norm_channels(yc) * col(COL_CONV_G)).T.astype(BF16)
    q_t = proj_t(3)
    kw_ref[SEQ_TILE:, :] = (_group_norm_channels(k_t) * col(COL_K_G)).T.astype(BF16)
    v_t = proj_t(5)
    for _ in range(PIECES_BEFORE_ATTENTION):
        run_mlp_piece()
    qt_ref[...] = (_group_norm_channels(q_t) * col(COL_Q_G)
                   * (HEAD_DIM ** -0.5 * LOG2_E)).astype(BF16)
    vtw_ref[:, SEQ_TILE:] = v_t.astype(BF16)

    first_head = lax.broadcasted_iota(jnp.int32, (LANES, Q_BLOCK), 0) < HEAD_DIM
    pen_rows = (lax.broadcasted_iota(jnp.int32, (LANES, PAIR_COLS), 0) == 0).astype(BF16)
    blocks = [(qb * Q_BLOCK, p) for qb in range(Q_BLOCKS) for p in range(HEAD_PAIRS)]

    def scores(n):
        k0, p = blocks[n]
        ch = slice(p * LANES, (p + 1) * LANES)
        qt = qt_ref[ch, k0:k0 + Q_BLOCK]
        zero = jnp.zeros_like(qt)
        rhs = jnp.concatenate([jnp.where(first_head, qt, zero),
                               jnp.where(first_head, zero, qt)], axis=1)
        rhs = jnp.concatenate([rhs, pen_rows], axis=0)
        lhs = jnp.concatenate([kw_ref[k0:k0 + KEY_SPAN, ch], pen_ref[k0:k0 + KEY_SPAN, :]],
                              axis=1)
        s_ref[n % 2] = jnp.dot(lhs, rhs, preferred_element_type=F32) + bias_ref[p]

    def attend(n):
        k0, p = blocks[n]
        ch = slice(p * LANES, (p + 1) * LANES)
        s = s_ref[n % 2]
        pr = jnp.exp2(s - jnp.max(s, axis=0, keepdims=True))
        denom = jnp.sum(pr, axis=0, keepdims=True)
        ot = jnp.dot(vtw_ref[ch, k0:k0 + KEY_SPAN], pr.astype(BF16),
                     preferred_element_type=F32) / denom
        yt_ref[p * LANES:p * LANES + HEAD_DIM, k0:k0 + Q_BLOCK] = ot[0:HEAD_DIM, 0:Q_BLOCK]
        yt_ref[p * LANES + HEAD_DIM:(p + 1) * LANES, k0:k0 + Q_BLOCK] = ot[HEAD_DIM:, Q_BLOCK:]

    def conv_half_out_proj(c):
        cols = slice(c * MLP_COLS, (c + 1) * MLP_COLS)
        hc_ref[:, cols] = x_ref[:, cols] + jnp.dot(yc_ref[...], wo_ref[0:CONV_WIDTH, cols],
                                                   preferred_element_type=F32)

    scores(0)
    for n in range(len(blocks)):
        if n + 1 < len(blocks):
            scores(n + 1)
        attend(n)
        if n < D_MODEL // MLP_COLS:
            conv_half_out_proj(n)
        run_mlp_piece()
    yn_ref[...] = (_group_norm_channels(yt_ref[...]) * aog_ref[...]).T.astype(BF16)
    for _ in range(PIECES_CLOSING):
        run_mlp_piece()
    assert not mlp_pieces


def _layer(x, row_params, col_params, bias, aog_col, w_in, w_out, w1, w2):
    bsz, seq, _ = x.shape
    in_hbm = pl.BlockSpec(memory_space=pl.ANY)
    tiles_per_seq = seq // SEQ_TILE
    n_tiles = bsz * tiles_per_seq

    def tile_of(t):
        t = jnp.minimum(t, n_tiles - 1)
        return t // tiles_per_seq, t % tiles_per_seq, 0

    def out_tile_of(t):
        return tile_of(jnp.maximum(t - 1, 0))

    return pl.pallas_call(
        functools.partial(_layer_kernel, tiles_per_seq),
        grid=(n_tiles + 1,),
        in_specs=[pl.BlockSpec((None, SEQ_TILE, D_MODEL), tile_of),
                  _resident((N_ROW_PARAMS, D_MODEL)), _resident((CONV_WIDTH, N_COL_PARAMS)),
                  _resident((HEAD_PAIRS, KEY_SPAN, PAIR_COLS)), _resident((ATTN_WIDTH, 1)),
                  in_hbm, in_hbm, in_hbm, in_hbm],
        out_specs=pl.BlockSpec((None, SEQ_TILE, D_MODEL), out_tile_of),
        out_shape=jax.ShapeDtypeStruct(x.shape, F32),
        scratch_shapes=[pltpu.VMEM(w_in.shape[::-1], BF16),
                        pltpu.VMEM(w_out.shape, BF16),
                        pltpu.VMEM(w1.shape, BF16),
                        pltpu.VMEM(w2.shape, BF16),
                        pltpu.VMEM((2,) + WEIGHT_CHUNK, F32),
                        pltpu.SemaphoreType.DMA((2,)),
                        pltpu.VMEM((CONV_WIDTH, LANES), F32),
                        pltpu.VMEM((SEQ_TILE, D_MODEL), BF16),
                        pltpu.VMEM((SEQ_TILE, CONV_WIDTH), BF16),
                        pltpu.VMEM((ATTN_WIDTH, SEQ_TILE), BF16),
                        pltpu.VMEM((2 * SEQ_TILE, ATTN_WIDTH), BF16),
                        pltpu.VMEM((ATTN_WIDTH, 2 * SEQ_TILE), BF16),
                        pltpu.VMEM((2 * SEQ_TILE, LANES), BF16),
                        pltpu.VMEM((ATTN_WIDTH, SEQ_TILE), F32),
                        pltpu.VMEM((2, KEY_SPAN, PAIR_COLS), F32),
                        pltpu.VMEM((SEQ_TILE, D_MODEL), F32),
                        pltpu.VMEM((SEQ_TILE, ATTN_WIDTH), BF16),
                        pltpu.VMEM((SEQ_TILE, D_MODEL), BF16),
                        pltpu.VMEM((SEQ_TILE, D_FF), BF16)],
        compiler_params=pltpu.CompilerParams(
            dimension_semantics=("arbitrary",), vmem_limit_bytes=VMEM_LIMIT),
        name="layer",
    )(x, row_params, col_params, bias, aog_col, w_in, w_out, w1, w2)


def kernel(x, norm_mix_g, w_in, conv_w, conv_b, q_norm_g, k_norm_g, rel_bias, conv_out_g,
           attn_out_g, w_out, norm_mlp_g, w_mlp_in, w_mlp_out):
    depth = w_in.shape[0]
    for l in range(depth):
        col_params = jnp.stack(
            [conv_w[l, 0], conv_w[l, 1], conv_w[l, 2], conv_b[l], conv_out_g[l],
             jnp.tile(q_norm_g[l], N_HEADS), jnp.tile(k_norm_g[l], N_HEADS),
             jnp.zeros((CONV_WIDTH,), F32)], axis=1)
        row_params = jnp.zeros((N_ROW_PARAMS, x.shape[-1]), F32)
        row_params = row_params.at[ROW_MIX_G].set(norm_mix_g[l]).at[ROW_MLP_G].set(norm_mlp_g[l])
        x = _layer(x, row_params, col_params, _bias_table(rel_bias[l]), attn_out_g[l][:, None],
                   w_in[l], w_out[l], w_mlp_in[l], w_mlp_out[l])
    return x
```

```python
import functools
import math

import jax
import jax.numpy as jnp
from jax import lax
from jax.experimental import pallas as pl
from jax.experimental.pallas import tpu as pltpu

D_MODEL = 1024
CHUNK = 64
LEFT_CHUNKS = 8
BAND = LEFT_CHUNKS + 1
CONV_WIDTH = D_MODEL // 2
GROUP = 64
HEAD_DIM = 64
N_HEADS = (D_MODEL - CONV_WIDTH) // HEAD_DIM
ATTN_WIDTH = N_HEADS * HEAD_DIM
REL_CLIP = 128
REL_TABLE = (CHUNK - 1) + REL_CLIP + 1
D_FF = 4 * D_MODEL
EPS = 1e-6
NEG_INF = -1e30
LOG2_E = math.log2(math.e)

LANES = 128
SEQ_TILE = LEFT_CHUNKS * CHUNK
Q_BLOCK = 2 * CHUNK
Q_BLOCKS = SEQ_TILE // Q_BLOCK
KEY_SPAN = (BAND + 1) * CHUNK
HEAD_PAIRS = N_HEADS * HEAD_DIM // LANES
PAIR_COLS = 2 * Q_BLOCK
BIAS_FLAT_ROWS = KEY_SPAN - 2 * Q_BLOCK
BIAS_RING = 4 * LANES
MLP_COLS = 256
MLP_PIECES = (D_FF + D_MODEL) // MLP_COLS
WEIGHT_CHUNK = (512, 1024)
PIECES_BEFORE_ATTENTION = 2
PIECES_CLOSING = MLP_PIECES - PIECES_BEFORE_ATTENTION - SEQ_TILE // Q_BLOCK * HEAD_PAIRS
VMEM_LIMIT = 60000 * 1024

COL_CONV_W, COL_CONV_B, COL_CONV_G, COL_Q_G, COL_K_G, N_COL_PARAMS = 0, 3, 4, 5, 6, 8
ROW_MIX_G, ROW_MLP_G, N_ROW_PARAMS = 0, 1, 8

F32 = jnp.float32
BF16 = jnp.bfloat16
NT_DIMS = (((1,), (1,)), ((), ()))


def _resident(shape):
    return pl.BlockSpec(shape, lambda *_: (0,) * len(shape), pipeline_mode=pl.Buffered(1))


def _rms_norm(x, gain):
    ms = jnp.mean(x * x, axis=-1, keepdims=True)
    return x * lax.rsqrt(ms + EPS) * gain


def _rms_norm_bf16(x, gain):
    return _rms_norm(x, gain).astype(BF16)


def _group_norm_channels(y):
    c, t = y.shape
    y3 = y.reshape(c // GROUP, GROUP, t)
    ms = jnp.mean(y3 * y3, axis=1, keepdims=True)
    return (y3 * lax.rsqrt(ms + EPS)).reshape(c, t)


def _bias_table_kernel(ring_ref, out_ref):
    def fields(rows, r0):
        r = lax.broadcasted_iota(jnp.int32, (rows, PAIR_COLS), 0) + r0
        c = lax.broadcasted_iota(jnp.int32, (rows, PAIR_COLS), 1)
        j = (c // CHUNK) % 2
        return c < Q_BLOCK, (r >= j * CHUNK) & (r < j * CHUNK + BAND * CHUNK)

    near_rows = KEY_SPAN - BIAS_FLAT_ROWS
    first_far, in_band_far = fields(BIAS_FLAT_ROWS, 0)
    _, in_band_near = fields(near_rows, BIAS_FLAT_ROWS)
    for p in range(HEAD_PAIRS):
        far = jnp.where(first_far, ring_ref[2 * p:2 * p + 1, 0:1], ring_ref[2 * p + 1:2 * p + 2, 0:1])
        out_ref[p, 0:BIAS_FLAT_ROWS, :] = jnp.where(in_band_far, far * LOG2_E, NEG_INF)
        halves = []
        for e in range(2):
            ring = jnp.broadcast_to(ring_ref[2 * p + e:2 * p + e + 1, :], (near_rows, BIAS_RING))
            halves.append(pltpu.roll(ring, 0, 1, stride=1, stride_axis=0)[:, 0:Q_BLOCK])
        near = jnp.concatenate(halves, axis=1)
        out_ref[p, BIAS_FLAT_ROWS:, :] = jnp.where(in_band_near, near * LOG2_E, NEG_INF)


def _bias_ring(rel_bias):
    far = jnp.broadcast_to(rel_bias[:, -1:], (N_HEADS, BIAS_RING - 2 * Q_BLOCK + 1))
    past = jnp.broadcast_to(rel_bias[:, :1], (N_HEADS, CHUNK))
    return jnp.concatenate([far, past, rel_bias[:, :REL_TABLE - 1]], axis=1)


def _bias_table(rel_bias):
    return pl.pallas_call(
        _bias_table_kernel,
        in_specs=[pl.BlockSpec(memory_space=pltpu.VMEM)],
        out_specs=pl.BlockSpec(memory_space=pltpu.VMEM),
        out_shape=jax.ShapeDtypeStruct((HEAD_PAIRS, KEY_SPAN, PAIR_COLS), F32),
        name="rel_bias_table",
    )(_bias_ring(rel_bias))


def _load_weights_bf16(w_in_hbm, w_out_hbm, w1_hbm, w2_hbm, wt_ref, wo_ref, w1_ref, w2_ref,
                       stage_ref, sem):
    rows, cols = stage_ref.shape[1:]
    chunks = []

    def add(src, dst, transposed=False):
        for r0 in range(0, src.shape[0], rows):
            for c0 in range(0, src.shape[1], cols):
                if transposed:
                    def store(v, r0=r0, c0=c0):
                        dst[c0:c0 + cols, r0:r0 + rows] = v.T.astype(BF16)
                else:
                    def store(v, r0=r0, c0=c0):
                        dst[r0:r0 + rows, c0:c0 + cols] = v.astype(BF16)
                chunks.append((src.at[r0:r0 + rows, c0:c0 + cols], store))

    add(w_in_hbm, wt_ref, transposed=True)
    add(w_out_hbm, wo_ref)
    add(w1_hbm, w1_ref)
    add(w2_hbm, w2_ref)

    def copy(n):
        return pltpu.make_async_copy(chunks[n][0], stage_ref.at[n % 2], sem.at[n % 2])

    copy(0).start()
    for n in range(len(chunks)):
        if n + 1 < len(chunks):
            copy(n + 1).start()
        copy(n).wait()
        chunks[n][1](stage_ref[n % 2])


def _layer_kernel(tiles_per_seq, x_ref, rowp_ref, colp_ref, bias_ref, aog_ref,
                  w_in_hbm, w_out_hbm, w1_hbm, w2_hbm, o_ref,
                  wt_ref, wo_ref, w1_ref, w2_ref, stage_ref, sem,
                  carry_ref, xn_ref, yc_ref, qt_ref, kw_ref, vtw_ref, pen_ref, yt_ref, s_ref,
                  hc_ref, yn_ref, xn2_ref, hid_ref):
    t = pl.program_id(0)
    n_tiles = pl.num_programs(0) - 1
    first_tile = lax.rem(jnp.minimum(t, n_tiles - 1), tiles_per_seq) == 0

    @pl.when(t == 0)
    def _():
        _load_weights_bf16(w_in_hbm, w_out_hbm, w1_hbm, w2_hbm, wt_ref, wo_ref, w1_ref, w2_ref,
                           stage_ref, sem)
        kw_ref[...] = jnp.zeros_like(kw_ref)
        vtw_ref[...] = jnp.zeros_like(vtw_ref)
        hc_ref[...] = jnp.zeros_like(hc_ref)
        yn_ref[...] = jnp.zeros_like(yn_ref)

    @pl.when(first_tile)
    def _():
        carry_ref[...] = jnp.zeros_like(carry_ref)

    o_ref[...] = hc_ref[...] + jnp.dot(yn_ref[...], wo_ref[CONV_WIDTH:, :],
                                       preferred_element_type=F32)
    xn_ref[...] = _rms_norm(x_ref[...], rowp_ref[ROW_MIX_G:ROW_MIX_G + 1, :]).T.astype(BF16)

    def mlp_prologue():
        xn2_ref[...] = _rms_norm_bf16(o_ref[...], rowp_ref[ROW_MLP_G:ROW_MLP_G + 1, :])

    def mlp_up(j):
        cols = slice(j * MLP_COLS, (j + 1) * MLP_COLS)
        hid = jnp.maximum(jnp.dot(xn2_ref[...], w1_ref[:, cols], preferred_element_type=F32), 0.0)
        hid_ref[:, cols] = (hid * hid).astype(BF16)

    def mlp_down(j):
        cols = slice(j * MLP_COLS, (j + 1) * MLP_COLS)
        o_ref[:, cols] += jnp.dot(hid_ref[...], w2_ref[:, cols], preferred_element_type=F32)

    mlp_pieces = [(mlp_up, j) for j in range(D_FF // MLP_COLS)]
    mlp_pieces += [(mlp_down, j) for j in range(D_MODEL // MLP_COLS)]

    def run_mlp_piece():
        fn, j = mlp_pieces.pop(0)
        fn(j)

    def proj_t(j):
        return jnp.dot(wt_ref[j * CONV_WIDTH:(j + 1) * CONV_WIDTH, :], xn_ref[...],
                       preferred_element_type=F32)

    def col(j):
        return colp_ref[:, j:j + 1]

    kw_ref[0:SEQ_TILE, :] = kw_ref[SEQ_TILE:, :]
    vtw_ref[:, 0:SEQ_TILE] = vtw_ref[:, SEQ_TILE:]
    pen_lane = lax.broadcasted_iota(jnp.int32, (SEQ_TILE, LANES), 1) == 0
    pen_ref[0:SEQ_TILE, :] = jnp.where(pen_lane & first_tile, NEG_INF, 0.0).astype(BF16)
    pen_ref[SEQ_TILE:, :] = jnp.zeros((SEQ_TILE, LANES), BF16)

    gate_c = proj_t(1)
    gate_h = proj_t(2)
    gate_b = proj_t(0)
    mlp_prologue()
    k_t = proj_t(4)
    u = gate_c * gate_h
    ext = jnp.concatenate([carry_ref[...], u], axis=1)
    u1 = pltpu.roll(ext, 1, 1)[:, LANES:]
    u2 = pltpu.roll(ext, 2, 1)[:, LANES:]
    carry_ref[...] = u[:, SEQ_TILE - LANES:]
    y = col(COL_CONV_W) * u2 + col(COL_CONV_W + 1) * u1 + col(COL_CONV_W + 2) * u
    yc = gate_b * (y + col(COL_CONV_B))
    yc_ref[...] = (_group_norm_channels(yc) * col(COL_CONV_G)).T.astype(BF16)
    q_t = proj_t(3)
    kw_ref[SEQ_TILE:, :] = (_group_norm_channels(k_t) * col(COL_K_G)).T.astype(BF16)
    v_t = proj_t(5)
    for _ in range(PIECES_BEFORE_ATTENTION):
        run_mlp_piece()
    qt_ref[...] = (_group_norm_channels(q_t) * col(COL_Q_G)
                   * (HEAD_DIM ** -0.5 * LOG2_E)).astype(BF16)
    vtw_ref[:, SEQ_TILE:] = v_t.astype(BF16)

    first_head = lax.broadcasted_iota(jnp.int32, (LANES, Q_BLOCK), 0) < HEAD_DIM
    pen_rows = (lax.broadcasted_iota(jnp.int32, (LANES, PAIR_COLS), 0) == 0).astype(BF16)
    blocks = [(qb * Q_BLOCK, p) for qb in range(Q_BLOCKS) for p in range(HEAD_PAIRS)]

    def scores(n):
        k0, p = blocks[n]
        ch = slice(p * LANES, (p + 1) * LANES)
        qt = qt_ref[ch, k0:k0 + Q_BLOCK]
        zero = jnp.zeros_like(qt)
        rhs = jnp.concatenate([jnp.where(first_head, qt, zero),
                               jnp.where(first_head, zero, qt)], axis=1)
        rhs = jnp.concatenate([rhs, pen_rows], axis=0)
        lhs = jnp.concatenate([kw_ref[k0:k0 + KEY_SPAN, ch], pen_ref[k0:k0 + KEY_SPAN, :]],
                              axis=1)
        s_ref[n % 2] = jnp.dot(lhs, rhs, preferred_element_type=F32) + bias_ref[p]

    def attend(n):
        k0, p = blocks[n]
        ch = slice(p * LANES, (p + 1) * LANES)
        s = s_ref[n % 2]
        pr = jnp.exp2(s - jnp.max(s, axis=0, keepdims=True))
        denom = jnp.sum(pr, axis=0, keepdims=True)
        ot = jnp.dot(vtw_ref[ch, k0:k0 + KEY_SPAN], pr.astype(BF16),
                     preferred_element_type=F32) / denom
        yt_ref[p * LANES:p * LANES + HEAD_DIM, k0:k0 + Q_BLOCK] = ot[0:HEAD_DIM, 0:Q_BLOCK]
        yt_ref[p * LANES + HEAD_DIM:(p + 1) * LANES, k0:k0 + Q_BLOCK] = ot[HEAD_DIM:, Q_BLOCK:]

    def conv_half_out_proj(c):
        cols = slice(c * MLP_COLS, (c + 1) * MLP_COLS)
        hc_ref[:, cols] = x_ref[:, cols] + jnp.dot(yc_ref[...], wo_ref[0:CONV_WIDTH, cols],
                                                   preferred_element_type=F32)

    scores(0)
    for n in range(len(blocks)):
        if n + 1 < len(blocks):
            scores(n + 1)
        attend(n)
        if n < D_MODEL // MLP_COLS:
            conv_half_out_proj(n)
        run_mlp_piece()
    yn_ref[...] = (_group_norm_channels(yt_ref[...]) * aog_ref[...]).T.astype(BF16)
    for _ in range(PIECES_CLOSING):
        run_mlp_piece()
    assert not mlp_pieces


def _layer(x, row_params, col_params, bias, aog_col, w_in, w_out, w1, w2):
    bsz, seq, _ = x.shape
    in_hbm = pl.BlockSpec(memory_space=pl.ANY)
    tiles_per_seq = seq // SEQ_TILE
    n_tiles = bsz * tiles_per_seq

    def tile_of(t):
        t = jnp.minimum(t, n_tiles - 1)
        return t // tiles_per_seq, t % tiles_per_seq, 0

    def out_tile_of(t):
        return tile_of(jnp.maximum(t - 1, 0))

    return pl.pallas_call(
        functools.partial(_layer_kernel, tiles_per_seq),
        grid=(n_tiles + 1,),
        in_specs=[pl.BlockSpec((None, SEQ_TILE, D_MODEL), tile_of),
                  _resident((N_ROW_PARAMS, D_MODEL)), _resident((CONV_WIDTH, N_COL_PARAMS)),
                  _resident((HEAD_PAIRS, KEY_SPAN, PAIR_COLS)), _resident((ATTN_WIDTH, 1)),
                  in_hbm, in_hbm, in_hbm, in_hbm],
        out_specs=pl.BlockSpec((None, SEQ_TILE, D_MODEL), out_tile_of),
        out_shape=jax.ShapeDtypeStruct(x.shape, F32),
        scratch_shapes=[pltpu.VMEM(w_in.shape[::-1], BF16),
                        pltpu.VMEM(w_out.shape, BF16),
                        pltpu.VMEM(w1.shape, BF16),
                        pltpu.VMEM(w2.shape, BF16),
                        pltpu.VMEM((2,) + WEIGHT_CHUNK, F32),
                        pltpu.SemaphoreType.DMA((2,)),
                        pltpu.VMEM((CONV_WIDTH, LANES), F32),
                        pltpu.VMEM((D_MODEL, SEQ_TILE), BF16),
                        pltpu.VMEM((SEQ_TILE, CONV_WIDTH), BF16),
                        pltpu.VMEM((ATTN_WIDTH, SEQ_TILE), BF16),
                        pltpu.VMEM((2 * SEQ_TILE, ATTN_WIDTH), BF16),
                        pltpu.VMEM((ATTN_WIDTH, 2 * SEQ_TILE), BF16),
                        pltpu.VMEM((2 * SEQ_TILE, LANES), BF16),
                        pltpu.VMEM((ATTN_WIDTH, SEQ_TILE), F32),
                        pltpu.VMEM((2, KEY_SPAN, PAIR_COLS), F32),
                        pltpu.VMEM((SEQ_TILE, D_MODEL), F32),
                        pltpu.VMEM((SEQ_TILE, ATTN_WIDTH), BF16),
                        pltpu.VMEM((SEQ_TILE, D_MODEL), BF16),
                        pltpu.VMEM((SEQ_TILE, D_FF), BF16)],
        compiler_params=pltpu.CompilerParams(
            dimension_semantics=("arbitrary",), vmem_limit_bytes=VMEM_LIMIT),
        name="layer",
    )(x, row_params, col_params, bias, aog_col, w_in, w_out, w1, w2)


def kernel(x, norm_mix_g, w_in, conv_w, conv_b, q_norm_g, k_norm_g, rel_bias, conv_out_g,
           attn_out_g, w_out, norm_mlp_g, w_mlp_in, w_mlp_out):
    depth = w_in.shape[0]
    for l in range(depth):
        col_params = jnp.stack(
            [conv_w[l, 0], conv_w[l, 1], conv_w[l, 2], conv_b[l], conv_out_g[l],
             jnp.tile(q_norm_g[l], N_HEADS), jnp.tile(k_norm_g[l], N_HEADS),
             jnp.zeros((CONV_WIDTH,), F32)], axis=1)
        row_params = jnp.zeros((N_ROW_PARAMS, x.shape[-1]), F32)
        row_params = row_params.at[ROW_MIX_G].set(norm_mix_g[l]).at[ROW_MLP_G].set(norm_mlp_g[l])
        x = _layer(x, row_params, col_params, _bias_table(rel_bias[l]), attn_out_g[l][:, None],
                   w_in[l], w_out[l], w_mlp_in[l], w_mlp_out[l])
    return x
```

```python
import functools
import math

import jax
import jax.numpy as jnp
from jax import lax
from jax.experimental import pallas as pl
from jax.experimental.pallas import tpu as pltpu

D_MODEL = 1024
CHUNK = 64
LEFT_CHUNKS = 8
BAND = LEFT_CHUNKS + 1
CONV_WIDTH = D_MODEL // 2
GROUP = 64
HEAD_DIM = 64
N_HEADS = (D_MODEL - CONV_WIDTH) // HEAD_DIM
ATTN_WIDTH = N_HEADS * HEAD_DIM
REL_CLIP = 128
REL_TABLE = (CHUNK - 1) + REL_CLIP + 1
D_FF = 4 * D_MODEL
EPS = 1e-6
NEG_INF = -1e30
LOG2_E = math.log2(math.e)

LANES = 128
SEQ_TILE = LEFT_CHUNKS * CHUNK
Q_BLOCK = 2 * CHUNK
Q_BLOCKS = SEQ_TILE // Q_BLOCK
KEY_SPAN = (BAND + 1) * CHUNK
HEAD_PAIRS = N_HEADS * HEAD_DIM // LANES
PAIR_COLS = 2 * Q_BLOCK
BIAS_FLAT_ROWS = KEY_SPAN - 2 * Q_BLOCK
BIAS_RING = 4 * LANES
MLP_COLS = 256
MLP_PIECES = (D_FF + D_MODEL) // MLP_COLS
WEIGHT_CHUNK = (512, 1024)
PIECES_BEFORE_ATTENTION = 2
PIECES_CLOSING = MLP_PIECES - PIECES_BEFORE_ATTENTION - SEQ_TILE // Q_BLOCK * HEAD_PAIRS
VMEM_LIMIT = 60000 * 1024

COL_CONV_W, COL_CONV_B, COL_CONV_G, COL_Q_G, COL_K_G, N_COL_PARAMS = 0, 3, 4, 5, 6, 8

F32 = jnp.float32
BF16 = jnp.bfloat16
NT_DIMS = (((1,), (1,)), ((), ()))


def _resident(shape):
    return pl.BlockSpec(shape, lambda *_: (0,) * len(shape), pipeline_mode=pl.Buffered(1))


def _rms_norm_bf16(x, gain):
    ms = jnp.mean(x * x, axis=-1, keepdims=True)
    return (x * lax.rsqrt(ms + EPS) * gain).astype(BF16)


def _group_norm_channels(y):
    c, t = y.shape
    y3 = y.reshape(c // GROUP, GROUP, t)
    ms = jnp.mean(y3 * y3, axis=1, keepdims=True)
    return (y3 * lax.rsqrt(ms + EPS)).reshape(c, t)


def _bias_table_kernel(ring_ref, out_ref):
    def fields(rows, r0):
        r = lax.broadcasted_iota(jnp.int32, (rows, PAIR_COLS), 0) + r0
        c = lax.broadcasted_iota(jnp.int32, (rows, PAIR_COLS), 1)
        j = (c // CHUNK) % 2
        return c < Q_BLOCK, (r >= j * CHUNK) & (r < j * CHUNK + BAND * CHUNK)

    near_rows = KEY_SPAN - BIAS_FLAT_ROWS
    first_far, in_band_far = fields(BIAS_FLAT_ROWS, 0)
    _, in_band_near = fields(near_rows, BIAS_FLAT_ROWS)
    for p in range(HEAD_PAIRS):
        far = jnp.where(first_far, ring_ref[2 * p:2 * p + 1, 0:1], ring_ref[2 * p + 1:2 * p + 2, 0:1])
        out_ref[p, 0:BIAS_FLAT_ROWS, :] = jnp.where(in_band_far, far * LOG2_E, NEG_INF)
        halves = []
        for e in range(2):
            ring = jnp.broadcast_to(ring_ref[2 * p + e:2 * p + e + 1, :], (near_rows, BIAS_RING))
            halves.append(pltpu.roll(ring, 0, 1, stride=1, stride_axis=0)[:, 0:Q_BLOCK])
        near = jnp.concatenate(halves, axis=1)
        out_ref[p, BIAS_FLAT_ROWS:, :] = jnp.where(in_band_near, near * LOG2_E, NEG_INF)


def _bias_ring(rel_bias):
    far = jnp.broadcast_to(rel_bias[:, -1:], (N_HEADS, BIAS_RING - 2 * Q_BLOCK + 1))
    past = jnp.broadcast_to(rel_bias[:, :1], (N_HEADS, CHUNK))
    return jnp.concatenate([far, past, rel_bias[:, :REL_TABLE - 1]], axis=1)


def _bias_table(rel_bias):
    return pl.pallas_call(
        _bias_table_kernel,
        in_specs=[pl.BlockSpec(memory_space=pltpu.VMEM)],
        out_specs=pl.BlockSpec(memory_space=pltpu.VMEM),
        out_shape=jax.ShapeDtypeStruct((HEAD_PAIRS, KEY_SPAN, PAIR_COLS), F32),
        name="rel_bias_table",
    )(_bias_ring(rel_bias))


def _load_weights_bf16(w_in_hbm, w_out_hbm, w1_hbm, w2_hbm, wt_ref, wo_ref, w1_ref, w2_ref,
                       stage_ref, sem):
    rows, cols = stage_ref.shape[1:]
    chunks = []

    def add(src, dst, transposed=False):
        for r0 in range(0, src.shape[0], rows):
            for c0 in range(0, src.shape[1], cols):
                if transposed:
                    def store(v, r0=r0, c0=c0):
                        dst[c0:c0 + cols, r0:r0 + rows] = v.T.astype(BF16)
                else:
                    def store(v, r0=r0, c0=c0):
                        dst[r0:r0 + rows, c0:c0 + cols] = v.astype(BF16)
                chunks.append((src.at[r0:r0 + rows, c0:c0 + cols], store))

    add(w_in_hbm, wt_ref, transposed=True)
    add(w_out_hbm, wo_ref)
    add(w1_hbm, w1_ref)
    add(w2_hbm, w2_ref)

    def copy(n):
        return pltpu.make_async_copy(chunks[n][0], stage_ref.at[n % 2], sem.at[n % 2])

    copy(0).start()
    for n in range(len(chunks)):
        if n + 1 < len(chunks):
            copy(n + 1).start()
        copy(n).wait()
        chunks[n][1](stage_ref[n % 2])


def _layer_kernel(tiles_per_seq, x_ref, g1_ref, colp_ref, bias_ref, aog_ref, g2_ref,
                  w_in_hbm, w_out_hbm, w1_hbm, w2_hbm, o_ref,
                  wt_ref, wo_ref, w1_ref, w2_ref, stage_ref, sem,
                  carry_ref, xn_ref, yc_ref, qt_ref, kw_ref, vtw_ref, pen_ref, yt_ref, s_ref,
                  hc_ref, yn_ref, xn2_ref, hid_ref):
    t = pl.program_id(0)
    n_tiles = pl.num_programs(0) - 1
    first_tile = lax.rem(jnp.minimum(t, n_tiles - 1), tiles_per_seq) == 0

    @pl.when(t == 0)
    def _():
        _load_weights_bf16(w_in_hbm, w_out_hbm, w1_hbm, w2_hbm, wt_ref, wo_ref, w1_ref, w2_ref,
                           stage_ref, sem)
        kw_ref[...] = jnp.zeros_like(kw_ref)
        vtw_ref[...] = jnp.zeros_like(vtw_ref)
        hc_ref[...] = jnp.zeros_like(hc_ref)
        yn_ref[...] = jnp.zeros_like(yn_ref)

    @pl.when(first_tile)
    def _():
        carry_ref[...] = jnp.zeros_like(carry_ref)

    o_ref[...] = hc_ref[...] + jnp.dot(yn_ref[...], wo_ref[CONV_WIDTH:, :],
                                       preferred_element_type=F32)
    xn_ref[...] = _rms_norm_bf16(x_ref[...], g1_ref[...])

    def mlp_prologue():
        xn2_ref[...] = _rms_norm_bf16(o_ref[...], g2_ref[...])

    def mlp_up(j):
        cols = slice(j * MLP_COLS, (j + 1) * MLP_COLS)
        hid = jnp.maximum(jnp.dot(xn2_ref[...], w1_ref[:, cols], preferred_element_type=F32), 0.0)
        hid_ref[:, cols] = (hid * hid).astype(BF16)

    def mlp_down(j):
        cols = slice(j * MLP_COLS, (j + 1) * MLP_COLS)
        o_ref[:, cols] += jnp.dot(hid_ref[...], w2_ref[:, cols], preferred_element_type=F32)

    mlp_pieces = [(mlp_up, j) for j in range(D_FF // MLP_COLS)]
    mlp_pieces += [(mlp_down, j) for j in range(D_MODEL // MLP_COLS)]

    def run_mlp_piece():
        fn, j = mlp_pieces.pop(0)
        fn(j)

    def proj_t(j):
        return lax.dot_general(wt_ref[j * CONV_WIDTH:(j + 1) * CONV_WIDTH, :], xn_ref[...],
                               NT_DIMS, preferred_element_type=F32)

    def col(j):
        return colp_ref[:, j:j + 1]

    kw_ref[0:SEQ_TILE, :] = kw_ref[SEQ_TILE:, :]
    vtw_ref[:, 0:SEQ_TILE] = vtw_ref[:, SEQ_TILE:]
    pen_lane = lax.broadcasted_iota(jnp.int32, (SEQ_TILE, LANES), 1) == 0
    pen_ref[0:SEQ_TILE, :] = jnp.where(pen_lane & first_tile, NEG_INF, 0.0).astype(BF16)
    pen_ref[SEQ_TILE:, :] = jnp.zeros((SEQ_TILE, LANES), BF16)

    gate_c = proj_t(1)
    gate_h = proj_t(2)
    gate_b = proj_t(0)
    mlp_prologue()
    k_t = proj_t(4)
    u = gate_c * gate_h
    ext = jnp.concatenate([carry_ref[...], u], axis=1)
    u1 = pltpu.roll(ext, 1, 1)[:, LANES:]
    u2 = pltpu.roll(ext, 2, 1)[:, LANES:]
    carry_ref[...] = u[:, SEQ_TILE - LANES:]
    y = col(COL_CONV_W) * u2 + col(COL_CONV_W + 1) * u1 + col(COL_CONV_W + 2) * u
    yc = gate_b * (y + col(COL_CONV_B))
    yc_ref[...] = (_group_norm_channels(yc) * col(COL_CONV_G)).astype(BF16).T
    q_t = proj_t(3)
    kw_ref[SEQ_TILE:, :] = (_group_norm_channels(k_t) * col(COL_K_G)).astype(BF16).T
    v_t = proj_t(5)
    for _ in range(PIECES_BEFORE_ATTENTION):
        run_mlp_piece()
    qt_ref[...] = (_group_norm_channels(q_t) * col(COL_Q_G)
                   * (HEAD_DIM ** -0.5 * LOG2_E)).astype(BF16)
    vtw_ref[:, SEQ_TILE:] = v_t.astype(BF16)

    first_head = lax.broadcasted_iota(jnp.int32, (LANES, Q_BLOCK), 0) < HEAD_DIM
    pen_rows = (lax.broadcasted_iota(jnp.int32, (LANES, PAIR_COLS), 0) == 0).astype(BF16)
    blocks = [(qb * Q_BLOCK, p) for qb in range(Q_BLOCKS) for p in range(HEAD_PAIRS)]

    def scores(n):
        k0, p = blocks[n]
        ch = slice(p * LANES, (p + 1) * LANES)
        qt = qt_ref[ch, k0:k0 + Q_BLOCK]
        zero = jnp.zeros_like(qt)
        rhs = jnp.concatenate([jnp.where(first_head, qt, zero),
                               jnp.where(first_head, zero, qt)], axis=1)
        rhs = jnp.concatenate([rhs, pen_rows], axis=0)
        lhs = jnp.concatenate([kw_ref[k0:k0 + KEY_SPAN, ch], pen_ref[k0:k0 + KEY_SPAN, :]],
                              axis=1)
        s_ref[n % 2] = jnp.dot(lhs, rhs, preferred_element_type=F32) + bias_ref[p]

    def attend(n):
        k0, p = blocks[n]
        ch = slice(p * LANES, (p + 1) * LANES)
        s = s_ref[n % 2]
        pr = jnp.exp2(s - jnp.max(s, axis=0, keepdims=True))
        denom = jnp.sum(pr, axis=0, keepdims=True)
        ot = jnp.dot(vtw_ref[ch, k0:k0 + KEY_SPAN], pr.astype(BF16),
                     preferred_element_type=F32) / denom
        yt_ref[p * LANES:p * LANES + HEAD_DIM, k0:k0 + Q_BLOCK] = ot[0:HEAD_DIM, 0:Q_BLOCK]
        yt_ref[p * LANES + HEAD_DIM:(p + 1) * LANES, k0:k0 + Q_BLOCK] = ot[HEAD_DIM:, Q_BLOCK:]

    def conv_half_out_proj(c):
        cols = slice(c * MLP_COLS, (c + 1) * MLP_COLS)
        hc_ref[:, cols] = x_ref[:, cols] + jnp.dot(yc_ref[...], wo_ref[0:CONV_WIDTH, cols],
                                                   preferred_element_type=F32)

    scores(0)
    for n in range(len(blocks)):
        if n + 1 < len(blocks):
            scores(n + 1)
        attend(n)
        if n < D_MODEL // MLP_COLS:
            conv_half_out_proj(n)
        run_mlp_piece()
    yn_ref[...] = (_group_norm_channels(yt_ref[...]) * aog_ref[...]).astype(BF16).T
    for _ in range(PIECES_CLOSING):
        run_mlp_piece()
    assert not mlp_pieces


def _layer(x, g1, col_params, bias, aog_col, g2, w_in, w_out, w1, w2):
    bsz, seq, _ = x.shape
    in_hbm = pl.BlockSpec(memory_space=pl.ANY)
    tiles_per_seq = seq // SEQ_TILE
    n_tiles = bsz * tiles_per_seq

    def tile_of(t):
        t = jnp.minimum(t, n_tiles - 1)
        return t // tiles_per_seq, t % tiles_per_seq, 0

    def out_tile_of(t):
        return tile_of(jnp.maximum(t - 1, 0))

    return pl.pallas_call(
        functools.partial(_layer_kernel, tiles_per_seq),
        grid=(n_tiles + 1,),
        in_specs=[pl.BlockSpec((None, SEQ_TILE, D_MODEL), tile_of),
                  _resident((1, D_MODEL)), _resident((CONV_WIDTH, N_COL_PARAMS)),
                  _resident((HEAD_PAIRS, KEY_SPAN, PAIR_COLS)), _resident((ATTN_WIDTH, 1)),
                  _resident((1, D_MODEL)), in_hbm, in_hbm, in_hbm, in_hbm],
        out_specs=pl.BlockSpec((None, SEQ_TILE, D_MODEL), out_tile_of),
        out_shape=jax.ShapeDtypeStruct(x.shape, F32),
        scratch_shapes=[pltpu.VMEM(w_in.shape[::-1], BF16),
                        pltpu.VMEM(w_out.shape, BF16),
                        pltpu.VMEM(w1.shape, BF16),
                        pltpu.VMEM(w2.shape, BF16),
                        pltpu.VMEM((2,) + WEIGHT_CHUNK, F32),
                        pltpu.SemaphoreType.DMA((2,)),
                        pltpu.VMEM((CONV_WIDTH, LANES), F32),
                        pltpu.VMEM((SEQ_TILE, D_MODEL), BF16),
                        pltpu.VMEM((SEQ_TILE, CONV_WIDTH), BF16),
                        pltpu.VMEM((ATTN_WIDTH, SEQ_TILE), BF16),
                        pltpu.VMEM((2 * SEQ_TILE, ATTN_WIDTH), BF16),
                        pltpu.VMEM((ATTN_WIDTH, 2 * SEQ_TILE), BF16),
                        pltpu.VMEM((2 * SEQ_TILE, LANES), BF16),
                        pltpu.VMEM((ATTN_WIDTH, SEQ_TILE), F32),
                        pltpu.VMEM((2, KEY_SPAN, PAIR_COLS), F32),
                        pltpu.VMEM((SEQ_TILE, D_MODEL), F32),
                        pltpu.VMEM((SEQ_TILE, ATTN_WIDTH), BF16),
                        pltpu.VMEM((SEQ_TILE, D_MODEL), BF16),
                        pltpu.VMEM((SEQ_TILE, D_FF), BF16)],
        compiler_params=pltpu.CompilerParams(
            dimension_semantics=("arbitrary",), vmem_limit_bytes=VMEM_LIMIT),
        name="layer",
    )(x, g1, col_params, bias, aog_col, g2, w_in, w_out, w1, w2)


def kernel(x, norm_mix_g, w_in, conv_w, conv_b, q_norm_g, k_norm_g, rel_bias, conv_out_g,
           attn_out_g, w_out, norm_mlp_g, w_mlp_in, w_mlp_out):
    depth = w_in.shape[0]
    for l in range(depth):
        col_params = jnp.stack(
            [conv_w[l, 0], conv_w[l, 1], conv_w[l, 2], conv_b[l], conv_out_g[l],
             jnp.tile(q_norm_g[l], N_HEADS), jnp.tile(k_norm_g[l], N_HEADS),
             jnp.zeros((CONV_WIDTH,), F32)], axis=1)
        x = _layer(x, norm_mix_g[l][None], col_params, _bias_table(rel_bias[l]),
                   attn_out_g[l][:, None], norm_mlp_g[l][None],
                   w_in[l], w_out[l], w_mlp_in[l], w_mlp_out[l])
    return x
```

```python
import functools
import math

import jax
import jax.numpy as jnp
from jax import lax
from jax.experimental import pallas as pl
from jax.experimental.pallas import tpu as pltpu

D_MODEL = 1024
CHUNK = 64
LEFT_CHUNKS = 8
BAND = LEFT_CHUNKS + 1
CONV_WIDTH = D_MODEL // 2
GROUP = 64
HEAD_DIM = 64
N_HEADS = (D_MODEL - CONV_WIDTH) // HEAD_DIM
ATTN_WIDTH = N_HEADS * HEAD_DIM
REL_CLIP = 128
REL_TABLE = (CHUNK - 1) + REL_CLIP + 1
D_FF = 4 * D_MODEL
EPS = 1e-6
NEG_INF = -1e30
LOG2_E = math.log2(math.e)

LANES = 128
SEQ_TILE = LEFT_CHUNKS * CHUNK
Q_BLOCK = 2 * CHUNK
Q_BLOCKS = SEQ_TILE // Q_BLOCK
KEY_SPAN = (BAND + 1) * CHUNK
HEAD_PAIRS = N_HEADS * HEAD_DIM // LANES
PAIR_COLS = 2 * Q_BLOCK
BIAS_FLAT_ROWS = KEY_SPAN - 2 * Q_BLOCK
BIAS_RING = 4 * LANES
MLP_COLS = 256
MLP_PIECES = (D_FF + D_MODEL) // MLP_COLS
WEIGHT_CHUNK = (512, 1024)
PIECES_BEFORE_ATTENTION = 2
PIECES_CLOSING = MLP_PIECES - PIECES_BEFORE_ATTENTION - SEQ_TILE // Q_BLOCK * HEAD_PAIRS
VMEM_LIMIT = 60000 * 1024

COL_CONV_W, COL_CONV_B, COL_CONV_G, COL_Q_G, COL_K_G, COL_ATTN_G, N_COL_PARAMS = 0, 3, 4, 5, 6, 7, 8

F32 = jnp.float32
BF16 = jnp.bfloat16
NT_DIMS = (((1,), (1,)), ((), ()))


def _resident(shape):
    return pl.BlockSpec(shape, lambda *_: (0,) * len(shape), pipeline_mode=pl.Buffered(1))


def _rms_norm_bf16(x, gain):
    ms = jnp.mean(x * x, axis=-1, keepdims=True)
    return (x * lax.rsqrt(ms + EPS) * gain).astype(BF16)


def _group_norm_channels(y):
    c, t = y.shape
    y3 = y.reshape(c // GROUP, GROUP, t)
    ms = jnp.mean(y3 * y3, axis=1, keepdims=True)
    return (y3 * lax.rsqrt(ms + EPS)).reshape(c, t)


def _bias_table_kernel(ring_ref, out_ref):
    def fields(rows, r0):
        r = lax.broadcasted_iota(jnp.int32, (rows, PAIR_COLS), 0) + r0
        c = lax.broadcasted_iota(jnp.int32, (rows, PAIR_COLS), 1)
        j = (c // CHUNK) % 2
        return c < Q_BLOCK, (r >= j * CHUNK) & (r < j * CHUNK + BAND * CHUNK)

    near_rows = KEY_SPAN - BIAS_FLAT_ROWS
    first_far, in_band_far = fields(BIAS_FLAT_ROWS, 0)
    _, in_band_near = fields(near_rows, BIAS_FLAT_ROWS)
    for p in range(HEAD_PAIRS):
        far = jnp.where(first_far, ring_ref[2 * p:2 * p + 1, 0:1], ring_ref[2 * p + 1:2 * p + 2, 0:1])
        out_ref[p, 0:BIAS_FLAT_ROWS, :] = jnp.where(in_band_far, far * LOG2_E, NEG_INF)
        halves = []
        for e in range(2):
            ring = jnp.broadcast_to(ring_ref[2 * p + e:2 * p + e + 1, :], (near_rows, BIAS_RING))
            halves.append(pltpu.roll(ring, 0, 1, stride=1, stride_axis=0)[:, 0:Q_BLOCK])
        near = jnp.concatenate(halves, axis=1)
        out_ref[p, BIAS_FLAT_ROWS:, :] = jnp.where(in_band_near, near * LOG2_E, NEG_INF)


def _bias_ring(rel_bias):
    far = jnp.broadcast_to(rel_bias[:, -1:], (N_HEADS, BIAS_RING - 2 * Q_BLOCK + 1))
    past = jnp.broadcast_to(rel_bias[:, :1], (N_HEADS, CHUNK))
    return jnp.concatenate([far, past, rel_bias[:, :REL_TABLE - 1]], axis=1)


def _bias_table(rel_bias):
    return pl.pallas_call(
        _bias_table_kernel,
        in_specs=[pl.BlockSpec(memory_space=pltpu.VMEM)],
        out_specs=pl.BlockSpec(memory_space=pltpu.VMEM),
        out_shape=jax.ShapeDtypeStruct((HEAD_PAIRS, KEY_SPAN, PAIR_COLS), F32),
        name="rel_bias_table",
    )(_bias_ring(rel_bias))


def _load_weights_bf16(w_in_hbm, w_out_hbm, w1_hbm, w2_hbm, wt_ref, wo_ref, w1_ref, w2_ref,
                       stage_ref, sem):
    rows, cols = stage_ref.shape[1:]
    chunks = []

    def add(src, dst, transposed=False):
        for r0 in range(0, src.shape[0], rows):
            for c0 in range(0, src.shape[1], cols):
                if transposed:
                    def store(v, r0=r0, c0=c0):
                        dst[c0:c0 + cols, r0:r0 + rows] = v.T.astype(BF16)
                else:
                    def store(v, r0=r0, c0=c0):
                        dst[r0:r0 + rows, c0:c0 + cols] = v.astype(BF16)
                chunks.append((src.at[r0:r0 + rows, c0:c0 + cols], store))

    add(w_in_hbm, wt_ref, transposed=True)
    add(w_out_hbm, wo_ref)
    add(w1_hbm, w1_ref)
    add(w2_hbm, w2_ref)

    def copy(n):
        return pltpu.make_async_copy(chunks[n][0], stage_ref.at[n % 2], sem.at[n % 2])

    copy(0).start()
    for n in range(len(chunks)):
        if n + 1 < len(chunks):
            copy(n + 1).start()
        copy(n).wait()
        chunks[n][1](stage_ref[n % 2])


def _layer_kernel(tiles_per_seq, x_ref, g1_ref, colp_ref, bias_ref, g2_ref,
                  w_in_hbm, w_out_hbm, w1_hbm, w2_hbm, o_ref,
                  wt_ref, wo_ref, w1_ref, w2_ref, stage_ref, sem,
                  carry_ref, xn_ref, yc_ref, qt_ref, kw_ref, vtw_ref, pen_ref, yt_ref, s_ref,
                  hc_ref, yn_ref, xn2_ref, hid_ref):
    t = pl.program_id(0)
    n_tiles = pl.num_programs(0) - 1
    first_tile = lax.rem(jnp.minimum(t, n_tiles - 1), tiles_per_seq) == 0

    @pl.when(t == 0)
    def _():
        _load_weights_bf16(w_in_hbm, w_out_hbm, w1_hbm, w2_hbm, wt_ref, wo_ref, w1_ref, w2_ref,
                           stage_ref, sem)
        kw_ref[...] = jnp.zeros_like(kw_ref)
        vtw_ref[...] = jnp.zeros_like(vtw_ref)
        hc_ref[...] = jnp.zeros_like(hc_ref)
        yn_ref[...] = jnp.zeros_like(yn_ref)

    @pl.when(first_tile)
    def _():
        carry_ref[...] = jnp.zeros_like(carry_ref)

    o_ref[...] = hc_ref[...] + jnp.dot(yn_ref[...], wo_ref[CONV_WIDTH:, :],
                                       preferred_element_type=F32)
    xn_ref[...] = _rms_norm_bf16(x_ref[...], g1_ref[...])

    def mlp_prologue():
        xn2_ref[...] = _rms_norm_bf16(o_ref[...], g2_ref[...])

    def mlp_up(j):
        cols = slice(j * MLP_COLS, (j + 1) * MLP_COLS)
        hid = jnp.maximum(jnp.dot(xn2_ref[...], w1_ref[:, cols], preferred_element_type=F32), 0.0)
        hid_ref[:, cols] = (hid * hid).astype(BF16)

    def mlp_down(j):
        cols = slice(j * MLP_COLS, (j + 1) * MLP_COLS)
        o_ref[:, cols] += jnp.dot(hid_ref[...], w2_ref[:, cols], preferred_element_type=F32)

    mlp_pieces = [(mlp_up, j) for j in range(D_FF // MLP_COLS)]
    mlp_pieces += [(mlp_down, j) for j in range(D_MODEL // MLP_COLS)]

    def run_mlp_piece():
        fn, j = mlp_pieces.pop(0)
        fn(j)

    def proj_t(j):
        return lax.dot_general(wt_ref[j * CONV_WIDTH:(j + 1) * CONV_WIDTH, :], xn_ref[...],
                               NT_DIMS, preferred_element_type=F32)

    def col(j):
        return jnp.concatenate([colp_ref[j]] * (SEQ_TILE // LANES), axis=1)

    kw_ref[0:SEQ_TILE, :] = kw_ref[SEQ_TILE:, :]
    vtw_ref[:, 0:SEQ_TILE] = vtw_ref[:, SEQ_TILE:]
    pen_lane = lax.broadcasted_iota(jnp.int32, (SEQ_TILE, LANES), 1) == 0
    pen_ref[0:SEQ_TILE, :] = jnp.where(pen_lane & first_tile, NEG_INF, 0.0).astype(BF16)
    pen_ref[SEQ_TILE:, :] = jnp.zeros((SEQ_TILE, LANES), BF16)

    gate_c = proj_t(1)
    gate_h = proj_t(2)
    gate_b = proj_t(0)
    mlp_prologue()
    k_t = proj_t(4)
    u = gate_c * gate_h
    ext = jnp.concatenate([carry_ref[...], u], axis=1)
    u1 = pltpu.roll(ext, 1, 1)[:, LANES:]
    u2 = pltpu.roll(ext, 2, 1)[:, LANES:]
    carry_ref[...] = u[:, SEQ_TILE - LANES:]
    y = col(COL_CONV_W) * u2 + col(COL_CONV_W + 1) * u1 + col(COL_CONV_W + 2) * u
    yc = gate_b * (y + col(COL_CONV_B))
    yc_ref[...] = (_group_norm_channels(yc) * col(COL_CONV_G)).astype(BF16).T
    q_t = proj_t(3)
    kw_ref[SEQ_TILE:, :] = (_group_norm_channels(k_t) * col(COL_K_G)).astype(BF16).T
    v_t = proj_t(5)
    for _ in range(PIECES_BEFORE_ATTENTION):
        run_mlp_piece()
    qt_ref[...] = (_group_norm_channels(q_t) * col(COL_Q_G)
                   * (HEAD_DIM ** -0.5 * LOG2_E)).astype(BF16)
    vtw_ref[:, SEQ_TILE:] = v_t.astype(BF16)

    first_head = lax.broadcasted_iota(jnp.int32, (LANES, Q_BLOCK), 0) < HEAD_DIM
    pen_rows = (lax.broadcasted_iota(jnp.int32, (LANES, PAIR_COLS), 0) == 0).astype(BF16)
    blocks = [(qb * Q_BLOCK, p) for qb in range(Q_BLOCKS) for p in range(HEAD_PAIRS)]

    def scores(n):
        k0, p = blocks[n]
        ch = slice(p * LANES, (p + 1) * LANES)
        qt = qt_ref[ch, k0:k0 + Q_BLOCK]
        zero = jnp.zeros_like(qt)
        rhs = jnp.concatenate([jnp.where(first_head, qt, zero),
                               jnp.where(first_head, zero, qt)], axis=1)
        rhs = jnp.concatenate([rhs, pen_rows], axis=0)
        lhs = jnp.concatenate([kw_ref[k0:k0 + KEY_SPAN, ch], pen_ref[k0:k0 + KEY_SPAN, :]],
                              axis=1)
        s_ref[n % 2] = jnp.dot(lhs, rhs, preferred_element_type=F32) + bias_ref[p]

    def attend(n):
        k0, p = blocks[n]
        ch = slice(p * LANES, (p + 1) * LANES)
        s = s_ref[n % 2]
        pr = jnp.exp2(s - jnp.max(s, axis=0, keepdims=True))
        denom = jnp.sum(pr, axis=0, keepdims=True)
        ot = jnp.dot(vtw_ref[ch, k0:k0 + KEY_SPAN], pr.astype(BF16),
                     preferred_element_type=F32) / denom
        yt_ref[p * LANES:p * LANES + HEAD_DIM, k0:k0 + Q_BLOCK] = ot[0:HEAD_DIM, 0:Q_BLOCK]
        yt_ref[p * LANES + HEAD_DIM:(p + 1) * LANES, k0:k0 + Q_BLOCK] = ot[HEAD_DIM:, Q_BLOCK:]

    def conv_half_out_proj(c):
        cols = slice(c * MLP_COLS, (c + 1) * MLP_COLS)
        hc_ref[:, cols] = x_ref[:, cols] + jnp.dot(yc_ref[...], wo_ref[0:CONV_WIDTH, cols],
                                                   preferred_element_type=F32)

    scores(0)
    for n in range(len(blocks)):
        if n + 1 < len(blocks):
            scores(n + 1)
        attend(n)
        if n < D_MODEL // MLP_COLS:
            conv_half_out_proj(n)
        run_mlp_piece()
    yn_ref[...] = (_group_norm_channels(yt_ref[...]) * col(COL_ATTN_G)).astype(BF16).T
    for _ in range(PIECES_CLOSING):
        run_mlp_piece()
    assert not mlp_pieces


def _layer(x, g1, col_params, bias, g2, w_in, w_out, w1, w2):
    bsz, seq, _ = x.shape
    in_hbm = pl.BlockSpec(memory_space=pl.ANY)
    tiles_per_seq = seq // SEQ_TILE
    n_tiles = bsz * tiles_per_seq

    def tile_of(t):
        t = jnp.minimum(t, n_tiles - 1)
        return t // tiles_per_seq, t % tiles_per_seq, 0

    def out_tile_of(t):
        return tile_of(jnp.maximum(t - 1, 0))

    return pl.pallas_call(
        functools.partial(_layer_kernel, tiles_per_seq),
        grid=(n_tiles + 1,),
        in_specs=[pl.BlockSpec((None, SEQ_TILE, D_MODEL), tile_of),
                  _resident((1, D_MODEL)), _resident((N_COL_PARAMS, CONV_WIDTH, LANES)),
                  _resident((HEAD_PAIRS, KEY_SPAN, PAIR_COLS)),
                  _resident((1, D_MODEL)), in_hbm, in_hbm, in_hbm, in_hbm],
        out_specs=pl.BlockSpec((None, SEQ_TILE, D_MODEL), out_tile_of),
        out_shape=jax.ShapeDtypeStruct(x.shape, F32),
        scratch_shapes=[pltpu.VMEM(w_in.shape[::-1], BF16),
                        pltpu.VMEM(w_out.shape, BF16),
                        pltpu.VMEM(w1.shape, BF16),
                        pltpu.VMEM(w2.shape, BF16),
                        pltpu.VMEM((2,) + WEIGHT_CHUNK, F32),
                        pltpu.SemaphoreType.DMA((2,)),
                        pltpu.VMEM((CONV_WIDTH, LANES), F32),
                        pltpu.VMEM((SEQ_TILE, D_MODEL), BF16),
                        pltpu.VMEM((SEQ_TILE, CONV_WIDTH), BF16),
                        pltpu.VMEM((ATTN_WIDTH, SEQ_TILE), BF16),
                        pltpu.VMEM((2 * SEQ_TILE, ATTN_WIDTH), BF16),
                        pltpu.VMEM((ATTN_WIDTH, 2 * SEQ_TILE), BF16),
                        pltpu.VMEM((2 * SEQ_TILE, LANES), BF16),
                        pltpu.VMEM((ATTN_WIDTH, SEQ_TILE), F32),
                        pltpu.VMEM((2, KEY_SPAN, PAIR_COLS), F32),
                        pltpu.VMEM((SEQ_TILE, D_MODEL), F32),
                        pltpu.VMEM((SEQ_TILE, ATTN_WIDTH), BF16),
                        pltpu.VMEM((SEQ_TILE, D_MODEL), BF16),
                        pltpu.VMEM((SEQ_TILE, D_FF), BF16)],
        compiler_params=pltpu.CompilerParams(
            dimension_semantics=("arbitrary",), vmem_limit_bytes=VMEM_LIMIT),
        name="layer",
    )(x, g1, col_params, bias, g2, w_in, w_out, w1, w2)


def kernel(x, norm_mix_g, w_in, conv_w, conv_b, q_norm_g, k_norm_g, rel_bias, conv_out_g,
           attn_out_g, w_out, norm_mlp_g, w_mlp_in, w_mlp_out):
    depth = w_in.shape[0]
    for l in range(depth):
        col_params = jnp.stack(
            [conv_w[l, 0], conv_w[l, 1], conv_w[l, 2], conv_b[l], conv_out_g[l],
             jnp.tile(q_norm_g[l], N_HEADS), jnp.tile(k_norm_g[l], N_HEADS), attn_out_g[l]])
        col_params = jnp.broadcast_to(col_params[:, :, None], col_params.shape + (LANES,))
        x = _layer(x, norm_mix_g[l][None], col_params, _bias_table(rel_bias[l]),
                   norm_mlp_g[l][None], w_in[l], w_out[l], w_mlp_in[l], w_mlp_out[l])
    return x
```

```python
import functools
import math

import jax
import jax.numpy as jnp
from jax import lax
from jax.experimental import pallas as pl
from jax.experimental.pallas import tpu as pltpu

D_MODEL = 1024
CHUNK = 64
LEFT_CHUNKS = 8
BAND = LEFT_CHUNKS + 1
CONV_WIDTH = D_MODEL // 2
GROUP = 64
HEAD_DIM = 64
N_HEADS = (D_MODEL - CONV_WIDTH) // HEAD_DIM
ATTN_WIDTH = N_HEADS * HEAD_DIM
REL_CLIP = 128
REL_TABLE = (CHUNK - 1) + REL_CLIP + 1
D_FF = 4 * D_MODEL
EPS = 1e-6
NEG_INF = -1e30
LOG2_E = math.log2(math.e)

LANES = 128
SEQ_TILE = LEFT_CHUNKS * CHUNK
Q_BLOCK = 2 * CHUNK
Q_BLOCKS = SEQ_TILE // Q_BLOCK
KEY_SPAN = (BAND + 1) * CHUNK
HEAD_PAIRS = N_HEADS * HEAD_DIM // LANES
PAIR_COLS = 2 * Q_BLOCK
BIAS_FLAT_ROWS = KEY_SPAN - 2 * Q_BLOCK
BIAS_RING = 4 * LANES
MLP_COLS = 256
MLP_DOWN_COLS = 512
WEIGHT_CHUNK = (512, 1024)
PIECES_BEFORE_ATTENTION = 2
PIECES_CLOSING = D_MODEL // MLP_DOWN_COLS
VMEM_LIMIT = 60000 * 1024

COL_CONV_W, COL_CONV_B, COL_CONV_G, COL_Q_G, COL_K_G, COL_ATTN_G, N_COL_PARAMS = 0, 3, 4, 5, 6, 7, 8

F32 = jnp.float32
BF16 = jnp.bfloat16
NT_DIMS = (((1,), (1,)), ((), ()))


def _resident(shape):
    return pl.BlockSpec(shape, lambda *_: (0,) * len(shape), pipeline_mode=pl.Buffered(1))


def _rms_norm_bf16(x, gain):
    ms = jnp.mean(x * x, axis=-1, keepdims=True)
    return (x * lax.rsqrt(ms + EPS) * gain).astype(BF16)


def _group_norm_channels(y):
    c, t = y.shape
    y3 = y.reshape(c // GROUP, GROUP, t)
    ms = jnp.mean(y3 * y3, axis=1, keepdims=True)
    return (y3 * lax.rsqrt(ms + EPS)).reshape(c, t)


def _bias_table_kernel(ring_ref, out_ref):
    def fields(rows, r0):
        r = lax.broadcasted_iota(jnp.int32, (rows, PAIR_COLS), 0) + r0
        c = lax.broadcasted_iota(jnp.int32, (rows, PAIR_COLS), 1)
        j = (c // CHUNK) % 2
        return c < Q_BLOCK, (r >= j * CHUNK) & (r < j * CHUNK + BAND * CHUNK)

    near_rows = KEY_SPAN - BIAS_FLAT_ROWS
    first_far, in_band_far = fields(BIAS_FLAT_ROWS, 0)
    _, in_band_near = fields(near_rows, BIAS_FLAT_ROWS)
    for p in range(HEAD_PAIRS):
        far = jnp.where(first_far, ring_ref[2 * p:2 * p + 1, 0:1], ring_ref[2 * p + 1:2 * p + 2, 0:1])
        out_ref[p, 0:BIAS_FLAT_ROWS, :] = jnp.where(in_band_far, far * LOG2_E, NEG_INF)
        halves = []
        for e in range(2):
            ring = jnp.broadcast_to(ring_ref[2 * p + e:2 * p + e + 1, :], (near_rows, BIAS_RING))
            halves.append(pltpu.roll(ring, 0, 1, stride=1, stride_axis=0)[:, 0:Q_BLOCK])
        near = jnp.concatenate(halves, axis=1)
        out_ref[p, BIAS_FLAT_ROWS:, :] = jnp.where(in_band_near, near * LOG2_E, NEG_INF)


def _bias_ring(rel_bias):
    far = jnp.broadcast_to(rel_bias[:, -1:], (N_HEADS, BIAS_RING - 2 * Q_BLOCK + 1))
    past = jnp.broadcast_to(rel_bias[:, :1], (N_HEADS, CHUNK))
    return jnp.concatenate([far, past, rel_bias[:, :REL_TABLE - 1]], axis=1)


def _bias_table(rel_bias):
    return pl.pallas_call(
        _bias_table_kernel,
        in_specs=[pl.BlockSpec(memory_space=pltpu.VMEM)],
        out_specs=pl.BlockSpec(memory_space=pltpu.VMEM),
        out_shape=jax.ShapeDtypeStruct((HEAD_PAIRS, KEY_SPAN, PAIR_COLS), F32),
        name="rel_bias_table",
    )(_bias_ring(rel_bias))


def _load_weights_bf16(w_in_hbm, w_out_hbm, w1_hbm, w2_hbm, wt_ref, wo_ref, w1_ref, w2_ref,
                       stage_ref, sem):
    rows, cols = stage_ref.shape[1:]
    chunks = []

    def add(src, dst, transposed=False):
        for r0 in range(0, src.shape[0], rows):
            for c0 in range(0, src.shape[1], cols):
                if transposed:
                    def store(v, r0=r0, c0=c0):
                        dst[c0:c0 + cols, r0:r0 + rows] = v.T.astype(BF16)
                else:
                    def store(v, r0=r0, c0=c0):
                        dst[r0:r0 + rows, c0:c0 + cols] = v.astype(BF16)
                chunks.append((src.at[r0:r0 + rows, c0:c0 + cols], store))

    add(w_in_hbm, wt_ref, transposed=True)
    add(w_out_hbm, wo_ref)
    add(w1_hbm, w1_ref)
    add(w2_hbm, w2_ref)

    def copy(n):
        return pltpu.make_async_copy(chunks[n][0], stage_ref.at[n % 2], sem.at[n % 2])

    copy(0).start()
    for n in range(len(chunks)):
        if n + 1 < len(chunks):
            copy(n + 1).start()
        copy(n).wait()
        chunks[n][1](stage_ref[n % 2])


def _layer_kernel(tiles_per_seq, x_ref, g1_ref, colp_ref, bias_ref, g2_ref,
                  w_in_hbm, w_out_hbm, w1_hbm, w2_hbm, o_ref,
                  wt_ref, wo_ref, w1_ref, w2_ref, stage_ref, sem,
                  carry_ref, xn_ref, yc_ref, qt_ref, kw_ref, vtw_ref, pen_ref, yt_ref, s_ref,
                  hc_ref, yn_ref, xn2_ref, hid_ref):
    t = pl.program_id(0)
    n_tiles = pl.num_programs(0) - 1
    first_tile = lax.rem(jnp.minimum(t, n_tiles - 1), tiles_per_seq) == 0

    @pl.when(t == 0)
    def _():
        _load_weights_bf16(w_in_hbm, w_out_hbm, w1_hbm, w2_hbm, wt_ref, wo_ref, w1_ref, w2_ref,
                           stage_ref, sem)
        kw_ref[...] = jnp.zeros_like(kw_ref)
        vtw_ref[...] = jnp.zeros_like(vtw_ref)
        hc_ref[...] = jnp.zeros_like(hc_ref)
        yn_ref[...] = jnp.zeros_like(yn_ref)

    @pl.when(first_tile)
    def _():
        carry_ref[...] = jnp.zeros_like(carry_ref)

    o_ref[...] = hc_ref[...] + jnp.dot(yn_ref[...], wo_ref[CONV_WIDTH:, :],
                                       preferred_element_type=F32)
    xn_ref[...] = _rms_norm_bf16(x_ref[...], g1_ref[...])

    def mlp_prologue():
        xn2_ref[...] = _rms_norm_bf16(o_ref[...], g2_ref[...])

    def mlp_up(j):
        cols = slice(j * MLP_COLS, (j + 1) * MLP_COLS)
        hid = jnp.maximum(jnp.dot(xn2_ref[...], w1_ref[:, cols], preferred_element_type=F32), 0.0)
        hid_ref[:, cols] = (hid * hid).astype(BF16)

    def mlp_down(j):
        cols = slice(j * MLP_DOWN_COLS, (j + 1) * MLP_DOWN_COLS)
        o_ref[:, cols] += jnp.dot(hid_ref[...], w2_ref[:, cols], preferred_element_type=F32)

    mlp_pieces = [(mlp_up, j) for j in range(D_FF // MLP_COLS)]
    mlp_pieces += [(mlp_down, j) for j in range(D_MODEL // MLP_DOWN_COLS)]

    def run_mlp_piece():
        fn, j = mlp_pieces.pop(0)
        fn(j)

    def proj_t(j):
        return lax.dot_general(wt_ref[j * CONV_WIDTH:(j + 1) * CONV_WIDTH, :], xn_ref[...],
                               NT_DIMS, preferred_element_type=F32)

    def col(j):
        return jnp.concatenate([colp_ref[j]] * (SEQ_TILE // LANES), axis=1)

    kw_ref[0:SEQ_TILE, :] = kw_ref[SEQ_TILE:, :]
    vtw_ref[:, 0:SEQ_TILE] = vtw_ref[:, SEQ_TILE:]
    pen_lane = lax.broadcasted_iota(jnp.int32, (SEQ_TILE, LANES), 1) == 0
    pen_ref[0:SEQ_TILE, :] = jnp.where(pen_lane & first_tile, NEG_INF, 0.0).astype(BF16)
    pen_ref[SEQ_TILE:, :] = jnp.zeros((SEQ_TILE, LANES), BF16)

    gate_c = proj_t(1)
    gate_h = proj_t(2)
    gate_b = proj_t(0)
    mlp_prologue()
    k_t = proj_t(4)
    u = gate_c * gate_h
    ext = jnp.concatenate([carry_ref[...], u], axis=1)
    u1 = pltpu.roll(ext, 1, 1)[:, LANES:]
    u2 = pltpu.roll(ext, 2, 1)[:, LANES:]
    carry_ref[...] = u[:, SEQ_TILE - LANES:]
    y = col(COL_CONV_W) * u2 + col(COL_CONV_W + 1) * u1 + col(COL_CONV_W + 2) * u
    yc = gate_b * (y + col(COL_CONV_B))
    yc_ref[...] = (_group_norm_channels(yc) * col(COL_CONV_G)).astype(BF16).T
    q_t = proj_t(3)
    kw_ref[SEQ_TILE:, :] = (_group_norm_channels(k_t) * col(COL_K_G)).astype(BF16).T
    v_t = proj_t(5)
    for _ in range(PIECES_BEFORE_ATTENTION):
        run_mlp_piece()
    qt_ref[...] = (_group_norm_channels(q_t) * col(COL_Q_G)
                   * (HEAD_DIM ** -0.5 * LOG2_E)).astype(BF16)
    vtw_ref[:, SEQ_TILE:] = v_t.astype(BF16)

    first_head = lax.broadcasted_iota(jnp.int32, (LANES, Q_BLOCK), 0) < HEAD_DIM
    pen_rows = (lax.broadcasted_iota(jnp.int32, (LANES, PAIR_COLS), 0) == 0).astype(BF16)
    blocks = [(qb * Q_BLOCK, p) for qb in range(Q_BLOCKS) for p in range(HEAD_PAIRS)]

    def scores(n):
        k0, p = blocks[n]
        ch = slice(p * LANES, (p + 1) * LANES)
        qt = qt_ref[ch, k0:k0 + Q_BLOCK]
        zero = jnp.zeros_like(qt)
        rhs = jnp.concatenate([jnp.where(first_head, qt, zero),
                               jnp.where(first_head, zero, qt)], axis=1)
        rhs = jnp.concatenate([rhs, pen_rows], axis=0)
        lhs = jnp.concatenate([kw_ref[k0:k0 + KEY_SPAN, ch], pen_ref[k0:k0 + KEY_SPAN, :]],
                              axis=1)
        s_ref[n % 2] = jnp.dot(lhs, rhs, preferred_element_type=F32) + bias_ref[p]

    def attend(n):
        k0, p = blocks[n]
        ch = slice(p * LANES, (p + 1) * LANES)
        s = s_ref[n % 2]
        pr = jnp.exp2(s - jnp.max(s, axis=0, keepdims=True))
        denom = jnp.sum(pr, axis=0, keepdims=True)
        ot = jnp.dot(vtw_ref[ch, k0:k0 + KEY_SPAN], pr.astype(BF16),
                     preferred_element_type=F32) / denom
        yt_ref[p * LANES:p * LANES + HEAD_DIM, k0:k0 + Q_BLOCK] = ot[0:HEAD_DIM, 0:Q_BLOCK]
        yt_ref[p * LANES + HEAD_DIM:(p + 1) * LANES, k0:k0 + Q_BLOCK] = ot[HEAD_DIM:, Q_BLOCK:]

    def conv_half_out_proj(c):
        cols = slice(c * MLP_COLS, (c + 1) * MLP_COLS)
        hc_ref[:, cols] = x_ref[:, cols] + jnp.dot(yc_ref[...], wo_ref[0:CONV_WIDTH, cols],
                                                   preferred_element_type=F32)

    scores(0)
    for n in range(len(blocks)):
        if n + 1 < len(blocks):
            scores(n + 1)
        attend(n)
        if n < D_MODEL // MLP_COLS:
            conv_half_out_proj(n)
        if len(mlp_pieces) > PIECES_CLOSING:
            run_mlp_piece()
    yn_ref[...] = (_group_norm_channels(yt_ref[...]) * col(COL_ATTN_G)).astype(BF16).T
    for _ in range(PIECES_CLOSING):
        run_mlp_piece()
    assert not mlp_pieces


def _layer(x, g1, col_params, bias, g2, w_in, w_out, w1, w2):
    bsz, seq, _ = x.shape
    in_hbm = pl.BlockSpec(memory_space=pl.ANY)
    tiles_per_seq = seq // SEQ_TILE
    n_tiles = bsz * tiles_per_seq

    def tile_of(t):
        t = jnp.minimum(t, n_tiles - 1)
        return t // tiles_per_seq, t % tiles_per_seq, 0

    def out_tile_of(t):
        return tile_of(jnp.maximum(t - 1, 0))

    return pl.pallas_call(
        functools.partial(_layer_kernel, tiles_per_seq),
        grid=(n_tiles + 1,),
        in_specs=[pl.BlockSpec((None, SEQ_TILE, D_MODEL), tile_of),
                  _resident((1, D_MODEL)), _resident((N_COL_PARAMS, CONV_WIDTH, LANES)),
                  _resident((HEAD_PAIRS, KEY_SPAN, PAIR_COLS)),
                  _resident((1, D_MODEL)), in_hbm, in_hbm, in_hbm, in_hbm],
        out_specs=pl.BlockSpec((None, SEQ_TILE, D_MODEL), out_tile_of),
        out_shape=jax.ShapeDtypeStruct(x.shape, F32),
        scratch_shapes=[pltpu.VMEM(w_in.shape[::-1], BF16),
                        pltpu.VMEM(w_out.shape, BF16),
                        pltpu.VMEM(w1.shape, BF16),
                        pltpu.VMEM(w2.shape, BF16),
                        pltpu.VMEM((2,) + WEIGHT_CHUNK, F32),
                        pltpu.SemaphoreType.DMA((2,)),
                        pltpu.VMEM((CONV_WIDTH, LANES), F32),
                        pltpu.VMEM((SEQ_TILE, D_MODEL), BF16),
                        pltpu.VMEM((SEQ_TILE, CONV_WIDTH), BF16),
                        pltpu.VMEM((ATTN_WIDTH, SEQ_TILE), BF16),
                        pltpu.VMEM((2 * SEQ_TILE, ATTN_WIDTH), BF16),
                        pltpu.VMEM((ATTN_WIDTH, 2 * SEQ_TILE), BF16),
                        pltpu.VMEM((2 * SEQ_TILE, LANES), BF16),
                        pltpu.VMEM((ATTN_WIDTH, SEQ_TILE), F32),
                        pltpu.VMEM((2, KEY_SPAN, PAIR_COLS), F32),
                        pltpu.VMEM((SEQ_TILE, D_MODEL), F32),
                        pltpu.VMEM((SEQ_TILE, ATTN_WIDTH), BF16),
                        pltpu.VMEM((SEQ_TILE, D_MODEL), BF16),
                        pltpu.VMEM((SEQ_TILE, D_FF), BF16)],
        compiler_params=pltpu.CompilerParams(
            dimension_semantics=("arbitrary",), vmem_limit_bytes=VMEM_LIMIT),
        name="layer",
    )(x, g1, col_params, bias, g2, w_in, w_out, w1, w2)


def kernel(x, norm_mix_g, w_in, conv_w, conv_b, q_norm_g, k_norm_g, rel_bias, conv_out_g,
           attn_out_g, w_out, norm_mlp_g, w_mlp_in, w_mlp_out):
    depth = w_in.shape[0]
    for l in range(depth):
        col_params = jnp.stack(
            [conv_w[l, 0], conv_w[l, 1], conv_w[l, 2], conv_b[l], conv_out_g[l],
             jnp.tile(q_norm_g[l], N_HEADS), jnp.tile(k_norm_g[l], N_HEADS), attn_out_g[l]])
        col_params = jnp.broadcast_to(col_params[:, :, None], col_params.shape + (LANES,))
        x = _layer(x, norm_mix_g[l][None], col_params, _bias_table(rel_bias[l]),
                   norm_mlp_g[l][None], w_in[l], w_out[l], w_mlp_in[l], w_mlp_out[l])
    return x
```

```python
import functools
import math

import jax
import jax.numpy as jnp
from jax import lax
from jax.experimental import pallas as pl
from jax.experimental.pallas import tpu as pltpu

D_MODEL = 1024
CHUNK = 64
LEFT_CHUNKS = 8
BAND = LEFT_CHUNKS + 1
CONV_WIDTH = D_MODEL // 2
GROUP = 64
HEAD_DIM = 64
N_HEADS = (D_MODEL - CONV_WIDTH) // HEAD_DIM
ATTN_WIDTH = N_HEADS * HEAD_DIM
REL_CLIP = 128
REL_TABLE = (CHUNK - 1) + REL_CLIP + 1
D_FF = 4 * D_MODEL
EPS = 1e-6
NEG_INF = -1e30
LOG2_E = math.log2(math.e)

LANES = 128
SEQ_TILE = LEFT_CHUNKS * CHUNK
Q_BLOCK = 2 * CHUNK
Q_BLOCKS = SEQ_TILE // Q_BLOCK
KEY_SPAN = (BAND + 1) * CHUNK
HEAD_PAIRS = N_HEADS * HEAD_DIM // LANES
PAIR_COLS = 2 * Q_BLOCK
BIAS_FLAT_ROWS = KEY_SPAN - 2 * Q_BLOCK
BIAS_RING = 4 * LANES
MLP_COLS = 256
MLP_PIECES = (D_FF + D_MODEL) // MLP_COLS
WEIGHT_CHUNK = (512, 1024)
PIECES_BEFORE_ATTENTION = 2
PIECES_CLOSING = MLP_PIECES - PIECES_BEFORE_ATTENTION - SEQ_TILE // Q_BLOCK * HEAD_PAIRS
VMEM_LIMIT = 60000 * 1024

COL_CONV_W, COL_CONV_B, COL_CONV_G, COL_Q_G, COL_K_G, COL_ATTN_G, N_COL_PARAMS = 0, 3, 4, 5, 6, 7, 8

F32 = jnp.float32
BF16 = jnp.bfloat16
NT_DIMS = (((1,), (1,)), ((), ()))


def _resident(shape):
    return pl.BlockSpec(shape, lambda *_: (0,) * len(shape), pipeline_mode=pl.Buffered(1))


def _rms_norm_bf16(x, gain):
    ms = jnp.mean(x * x, axis=-1, keepdims=True)
    return (x * lax.rsqrt(ms + EPS) * gain).astype(BF16)


def _group_norm_channels(y):
    c, t = y.shape
    y3 = y.reshape(c // GROUP, GROUP, t)
    ms = jnp.mean(y3 * y3, axis=1, keepdims=True)
    return (y3 * lax.rsqrt(ms + EPS)).reshape(c, t)


def _bias_table_kernel(ring_ref, out_ref):
    def fields(rows, r0):
        r = lax.broadcasted_iota(jnp.int32, (rows, PAIR_COLS), 0) + r0
        c = lax.broadcasted_iota(jnp.int32, (rows, PAIR_COLS), 1)
        j = (c // CHUNK) % 2
        return c < Q_BLOCK, (r >= j * CHUNK) & (r < j * CHUNK + BAND * CHUNK)

    near_rows = KEY_SPAN - BIAS_FLAT_ROWS
    first_far, in_band_far = fields(BIAS_FLAT_ROWS, 0)
    _, in_band_near = fields(near_rows, BIAS_FLAT_ROWS)
    for p in range(HEAD_PAIRS):
        far = jnp.where(first_far, ring_ref[2 * p:2 * p + 1, 0:1], ring_ref[2 * p + 1:2 * p + 2, 0:1])
        out_ref[p, 0:BIAS_FLAT_ROWS, :] = jnp.where(in_band_far, far * LOG2_E, NEG_INF)
        halves = []
        for e in range(2):
            ring = jnp.broadcast_to(ring_ref[2 * p + e:2 * p + e + 1, :], (near_rows, BIAS_RING))
            halves.append(pltpu.roll(ring, 0, 1, stride=1, stride_axis=0)[:, 0:Q_BLOCK])
        near = jnp.concatenate(halves, axis=1)
        out_ref[p, BIAS_FLAT_ROWS:, :] = jnp.where(in_band_near, near * LOG2_E, NEG_INF)


def _bias_ring(rel_bias):
    far = jnp.broadcast_to(rel_bias[:, -1:], (N_HEADS, BIAS_RING - 2 * Q_BLOCK + 1))
    past = jnp.broadcast_to(rel_bias[:, :1], (N_HEADS, CHUNK))
    return jnp.concatenate([far, past, rel_bias[:, :REL_TABLE - 1]], axis=1)


def _bias_table(rel_bias):
    return pl.pallas_call(
        _bias_table_kernel,
        in_specs=[pl.BlockSpec(memory_space=pltpu.VMEM)],
        out_specs=pl.BlockSpec(memory_space=pltpu.VMEM),
        out_shape=jax.ShapeDtypeStruct((HEAD_PAIRS, KEY_SPAN, PAIR_COLS), F32),
        name="rel_bias_table",
    )(_bias_ring(rel_bias))


def _load_weights_bf16(w_in_hbm, w_out_hbm, w1_hbm, w2_hbm, wt_ref, wo_ref, w1_ref, w2_ref,
                       stage_ref, sem):
    rows, cols = stage_ref.shape[1:]
    chunks = []

    def add(src, dst, transposed=False):
        for r0 in range(0, src.shape[0], rows):
            for c0 in range(0, src.shape[1], cols):
                if transposed:
                    def store(v, r0=r0, c0=c0):
                        dst[c0:c0 + cols, r0:r0 + rows] = v.astype(BF16).T
                else:
                    def store(v, r0=r0, c0=c0):
                        dst[r0:r0 + rows, c0:c0 + cols] = v.astype(BF16)
                chunks.append((src.at[r0:r0 + rows, c0:c0 + cols], store))

    add(w_in_hbm, wt_ref, transposed=True)
    add(w_out_hbm, wo_ref)
    add(w1_hbm, w1_ref)
    add(w2_hbm, w2_ref)

    def copy(n):
        return pltpu.make_async_copy(chunks[n][0], stage_ref.at[n % 2], sem.at[n % 2])

    copy(0).start()
    for n in range(len(chunks)):
        if n + 1 < len(chunks):
            copy(n + 1).start()
        copy(n).wait()
        chunks[n][1](stage_ref[n % 2])


def _layer_kernel(tiles_per_seq, x_ref, g1_ref, colp_ref, bias_ref, g2_ref,
                  w_in_hbm, w_out_hbm, w1_hbm, w2_hbm, o_ref,
                  wt_ref, wo_ref, w1_ref, w2_ref, stage_ref, sem,
                  carry_ref, xn_ref, yc_ref, qt_ref, kw_ref, vtw_ref, pen_ref, yt_ref, s_ref,
                  hc_ref, yn_ref, xn2_ref, hid_ref):
    t = pl.program_id(0)
    n_tiles = pl.num_programs(0) - 1
    tile_in_seq = lax.rem(jnp.minimum(t, n_tiles - 1), tiles_per_seq)
    first_tile = tile_in_seq == 0
    second_tile = tile_in_seq == 1

    @pl.when(t == 0)
    def _():
        _load_weights_bf16(w_in_hbm, w_out_hbm, w1_hbm, w2_hbm, wt_ref, wo_ref, w1_ref, w2_ref,
                           stage_ref, sem)
        kw_ref[...] = jnp.zeros_like(kw_ref)
        vtw_ref[...] = jnp.zeros_like(vtw_ref)
        hc_ref[...] = jnp.zeros_like(hc_ref)
        yn_ref[...] = jnp.zeros_like(yn_ref)
        pen_ref[...] = jnp.zeros_like(pen_ref)

    pen_lane = lax.broadcasted_iota(jnp.int32, (SEQ_TILE, LANES), 1) == 0

    @pl.when(first_tile)
    def _():
        carry_ref[...] = jnp.zeros_like(carry_ref)
        pen_ref[0:SEQ_TILE, :] = jnp.where(pen_lane, NEG_INF, 0.0).astype(BF16)

    @pl.when(second_tile)
    def _():
        pen_ref[0:SEQ_TILE, :] = jnp.zeros((SEQ_TILE, LANES), BF16)

    o_ref[...] = hc_ref[...] + jnp.dot(yn_ref[...], wo_ref[CONV_WIDTH:, :],
                                       preferred_element_type=F32)
    xn_ref[...] = _rms_norm_bf16(x_ref[...], g1_ref[...])

    def mlp_prologue():
        xn2_ref[...] = _rms_norm_bf16(o_ref[...], g2_ref[...])

    def mlp_up(j):
        cols = slice(j * MLP_COLS, (j + 1) * MLP_COLS)
        hid = jnp.maximum(jnp.dot(xn2_ref[...], w1_ref[:, cols], preferred_element_type=F32), 0.0)
        hid_ref[:, cols] = (hid * hid).astype(BF16)

    def mlp_down(j):
        cols = slice(j * MLP_COLS, (j + 1) * MLP_COLS)
        o_ref[:, cols] += jnp.dot(hid_ref[...], w2_ref[:, cols], preferred_element_type=F32)

    mlp_pieces = [(mlp_up, j) for j in range(D_FF // MLP_COLS)]
    mlp_pieces += [(mlp_down, j) for j in range(D_MODEL // MLP_COLS)]

    def run_mlp_piece():
        fn, j = mlp_pieces.pop(0)
        fn(j)

    def proj_t(j):
        return lax.dot_general(wt_ref[j * CONV_WIDTH:(j + 1) * CONV_WIDTH, :], xn_ref[...],
                               NT_DIMS, preferred_element_type=F32)

    def col(j):
        return jnp.concatenate([colp_ref[j]] * (SEQ_TILE // LANES), axis=1)

    kw_ref[0:SEQ_TILE, :] = kw_ref[SEQ_TILE:, :]
    vtw_ref[:, 0:SEQ_TILE] = vtw_ref[:, SEQ_TILE:]

    gate_c = proj_t(1)
    gate_h = proj_t(2)
    gate_b = proj_t(0)
    mlp_prologue()
    k_t = proj_t(4)
    u = gate_c * gate_h
    ext = jnp.concatenate([carry_ref[...], u], axis=1)
    u1 = pltpu.roll(ext, 1, 1)[:, LANES:]
    u2 = pltpu.roll(ext, 2, 1)[:, LANES:]
    carry_ref[...] = u[:, SEQ_TILE - LANES:]
    y = col(COL_CONV_W) * u2 + col(COL_CONV_W + 1) * u1 + col(COL_CONV_W + 2) * u
    yc = gate_b * (y + col(COL_CONV_B))
    yc_ref[...] = (_group_norm_channels(yc) * col(COL_CONV_G)).astype(BF16).T
    q_t = proj_t(3)
    kw_ref[SEQ_TILE:, :] = (_group_norm_channels(k_t) * col(COL_K_G)).astype(BF16).T
    v_t = proj_t(5)
    for _ in range(PIECES_BEFORE_ATTENTION):
        run_mlp_piece()
    qt_ref[...] = (_group_norm_channels(q_t) * col(COL_Q_G)
                   * (HEAD_DIM ** -0.5 * LOG2_E)).astype(BF16)
    vtw_ref[:, SEQ_TILE:] = v_t.astype(BF16)

    first_head = lax.broadcasted_iota(jnp.int32, (LANES, Q_BLOCK), 0) < HEAD_DIM
    pen_rows = (lax.broadcasted_iota(jnp.int32, (LANES, PAIR_COLS), 0) == 0).astype(BF16)
    blocks = [(qb * Q_BLOCK, p) for qb in range(Q_BLOCKS) for p in range(HEAD_PAIRS)]

    def scores(n):
        k0, p = blocks[n]
        ch = slice(p * LANES, (p + 1) * LANES)
        qt = qt_ref[ch, k0:k0 + Q_BLOCK]
        zero = jnp.zeros_like(qt)
        rhs = jnp.concatenate([jnp.where(first_head, qt, zero),
                               jnp.where(first_head, zero, qt)], axis=1)
        rhs = jnp.concatenate([rhs, pen_rows], axis=0)
        lhs = jnp.concatenate([kw_ref[k0:k0 + KEY_SPAN, ch], pen_ref[k0:k0 + KEY_SPAN, :]],
                              axis=1)
        s_ref[n % 2] = jnp.dot(lhs, rhs, preferred_element_type=F32) + bias_ref[p]

    def attend(n):
        k0, p = blocks[n]
        ch = slice(p * LANES, (p + 1) * LANES)
        s = s_ref[n % 2]
        pr = jnp.exp2(s - jnp.max(s, axis=0, keepdims=True))
        denom = jnp.sum(pr, axis=0, keepdims=True)
        ot = jnp.dot(vtw_ref[ch, k0:k0 + KEY_SPAN], pr.astype(BF16),
                     preferred_element_type=F32) / denom
        yt_ref[p * LANES:p * LANES + HEAD_DIM, k0:k0 + Q_BLOCK] = ot[0:HEAD_DIM, 0:Q_BLOCK]
        yt_ref[p * LANES + HEAD_DIM:(p + 1) * LANES, k0:k0 + Q_BLOCK] = ot[HEAD_DIM:, Q_BLOCK:]

    def conv_half_out_proj(c):
        cols = slice(c * MLP_COLS, (c + 1) * MLP_COLS)
        hc_ref[:, cols] = x_ref[:, cols] + jnp.dot(yc_ref[...], wo_ref[0:CONV_WIDTH, cols],
                                                   preferred_element_type=F32)

    scores(0)
    for n in range(len(blocks)):
        if n + 1 < len(blocks):
            scores(n + 1)
        attend(n)
        if n < D_MODEL // MLP_COLS:
            conv_half_out_proj(n)
        run_mlp_piece()
    yn_ref[...] = (_group_norm_channels(yt_ref[...]) * col(COL_ATTN_G)).astype(BF16).T
    for _ in range(PIECES_CLOSING):
        run_mlp_piece()
    assert not mlp_pieces


def _layer(x, g1, col_params, bias, g2, w_in, w_out, w1, w2):
    bsz, seq, _ = x.shape
    in_hbm = pl.BlockSpec(memory_space=pl.ANY)
    tiles_per_seq = seq // SEQ_TILE
    n_tiles = bsz * tiles_per_seq

    def tile_of(t):
        t = jnp.minimum(t, n_tiles - 1)
        return t // tiles_per_seq, t % tiles_per_seq, 0

    def out_tile_of(t):
        return tile_of(jnp.maximum(t - 1, 0))

    return pl.pallas_call(
        functools.partial(_layer_kernel, tiles_per_seq),
        grid=(n_tiles + 1,),
        in_specs=[pl.BlockSpec((None, SEQ_TILE, D_MODEL), tile_of),
                  _resident((1, D_MODEL)), _resident((N_COL_PARAMS, CONV_WIDTH, LANES)),
                  _resident((HEAD_PAIRS, KEY_SPAN, PAIR_COLS)),
                  _resident((1, D_MODEL)), in_hbm, in_hbm, in_hbm, in_hbm],
        out_specs=pl.BlockSpec((None, SEQ_TILE, D_MODEL), out_tile_of),
        out_shape=jax.ShapeDtypeStruct(x.shape, F32),
        scratch_shapes=[pltpu.VMEM(w_in.shape[::-1], BF16),
                        pltpu.VMEM(w_out.shape, BF16),
                        pltpu.VMEM(w1.shape, BF16),
                        pltpu.VMEM(w2.shape, BF16),
                        pltpu.VMEM((2,) + WEIGHT_CHUNK, F32),
                        pltpu.SemaphoreType.DMA((2,)),
                        pltpu.VMEM((CONV_WIDTH, LANES), F32),
                        pltpu.VMEM((SEQ_TILE, D_MODEL), BF16),
                        pltpu.VMEM((SEQ_TILE, CONV_WIDTH), BF16),
                        pltpu.VMEM((ATTN_WIDTH, SEQ_TILE), BF16),
                        pltpu.VMEM((2 * SEQ_TILE, ATTN_WIDTH), BF16),
                        pltpu.VMEM((ATTN_WIDTH, 2 * SEQ_TILE), BF16),
                        pltpu.VMEM((2 * SEQ_TILE, LANES), BF16),
                        pltpu.VMEM((ATTN_WIDTH, SEQ_TILE), F32),
                        pltpu.VMEM((2, KEY_SPAN, PAIR_COLS), F32),
                        pltpu.VMEM((SEQ_TILE, D_MODEL), F32),
                        pltpu.VMEM((SEQ_TILE, ATTN_WIDTH), BF16),
                        pltpu.VMEM((SEQ_TILE, D_MODEL), BF16),
                        pltpu.VMEM((SEQ_TILE, D_FF), BF16)],
        compiler_params=pltpu.CompilerParams(
            dimension_semantics=("arbitrary",), vmem_limit_bytes=VMEM_LIMIT),
        name="layer",
    )(x, g1, col_params, bias, g2, w_in, w_out, w1, w2)


def kernel(x, norm_mix_g, w_in, conv_w, conv_b, q_norm_g, k_norm_g, rel_bias, conv_out_g,
           attn_out_g, w_out, norm_mlp_g, w_mlp_in, w_mlp_out):
    depth = w_in.shape[0]
    for l in range(depth):
        col_params = jnp.stack(
            [conv_w[l, 0], conv_w[l, 1], conv_w[l, 2], conv_b[l], conv_out_g[l],
             jnp.tile(q_norm_g[l], N_HEADS), jnp.tile(k_norm_g[l], N_HEADS), attn_out_g[l]])
        col_params = jnp.broadcast_to(col_params[:, :, None], col_params.shape + (LANES,))
        x = _layer(x, norm_mix_g[l][None], col_params, _bias_table(rel_bias[l]),
                   norm_mlp_g[l][None], w_in[l], w_out[l], w_mlp_in[l], w_mlp_out[l])
    return x
```

```python
import functools
import math

import jax
import jax.numpy as jnp
from jax import lax
from jax.experimental import pallas as pl
from jax.experimental.pallas import tpu as pltpu

D_MODEL = 1024
CHUNK = 64
LEFT_CHUNKS = 8
BAND = LEFT_CHUNKS + 1
CONV_WIDTH = D_MODEL // 2
GROUP = 64
HEAD_DIM = 64
N_HEADS = (D_MODEL - CONV_WIDTH) // HEAD_DIM
ATTN_WIDTH = N_HEADS * HEAD_DIM
REL_CLIP = 128
REL_TABLE = (CHUNK - 1) + REL_CLIP + 1
D_FF = 4 * D_MODEL
EPS = 1e-6
NEG_INF = -1e30
LOG2_E = math.log2(math.e)

LANES = 128
SEQ_TILE = LEFT_CHUNKS * CHUNK
Q_BLOCK = 2 * CHUNK
Q_BLOCKS = SEQ_TILE // Q_BLOCK
KEY_SPAN = (BAND + 1) * CHUNK
HEAD_PAIRS = N_HEADS * HEAD_DIM // LANES
PAIR_COLS = 2 * Q_BLOCK
BIAS_FLAT_ROWS = KEY_SPAN - 2 * Q_BLOCK
BIAS_RING = 4 * LANES
MLP_COLS = 256
MLP_PIECES = (D_FF + D_MODEL) // MLP_COLS
WEIGHT_CHUNK = (512, 1024)
PIECES_BEFORE_ATTENTION = 2
PIECES_CLOSING = MLP_PIECES - PIECES_BEFORE_ATTENTION - SEQ_TILE // Q_BLOCK * HEAD_PAIRS
VMEM_LIMIT = 60000 * 1024

COL_CONV_W, COL_CONV_B, COL_CONV_G, COL_Q_G, COL_K_G, COL_ATTN_G, N_COL_PARAMS = 0, 3, 4, 5, 6, 7, 8

F32 = jnp.float32
BF16 = jnp.bfloat16
NT_DIMS = (((1,), (1,)), ((), ()))


def _resident(shape):
    return pl.BlockSpec(shape, lambda *_: (0,) * len(shape), pipeline_mode=pl.Buffered(1))


def _rms_norm_bf16(x, gain):
    ms = jnp.mean(x * x, axis=-1, keepdims=True)
    return (x * lax.rsqrt(ms + EPS) * gain).astype(BF16)


def _group_norm_channels(y):
    c, t = y.shape
    y3 = y.reshape(c // GROUP, GROUP, t)
    ms = jnp.mean(y3 * y3, axis=1, keepdims=True)
    return (y3 * lax.rsqrt(ms + EPS)).reshape(c, t)


def _bias_table_kernel(ring_ref, out_ref):
    def fields(rows, r0):
        r = lax.broadcasted_iota(jnp.int32, (rows, PAIR_COLS), 0) + r0
        c = lax.broadcasted_iota(jnp.int32, (rows, PAIR_COLS), 1)
        j = (c // CHUNK) % 2
        return c < Q_BLOCK, (r >= j * CHUNK) & (r < j * CHUNK + BAND * CHUNK)

    near_rows = KEY_SPAN - BIAS_FLAT_ROWS
    first_far, in_band_far = fields(BIAS_FLAT_ROWS, 0)
    _, in_band_near = fields(near_rows, BIAS_FLAT_ROWS)
    for p in range(HEAD_PAIRS):
        far = jnp.where(first_far, ring_ref[2 * p:2 * p + 1, 0:1], ring_ref[2 * p + 1:2 * p + 2, 0:1])
        out_ref[p, 0:BIAS_FLAT_ROWS, :] = jnp.where(in_band_far, far * LOG2_E, NEG_INF)
        halves = []
        for e in range(2):
            ring = jnp.broadcast_to(ring_ref[2 * p + e:2 * p + e + 1, :], (near_rows, BIAS_RING))
            halves.append(pltpu.roll(ring, 0, 1, stride=1, stride_axis=0)[:, 0:Q_BLOCK])
        near = jnp.concatenate(halves, axis=1)
        out_ref[p, BIAS_FLAT_ROWS:, :] = jnp.where(in_band_near, near * LOG2_E, NEG_INF)


def _bias_ring(rel_bias):
    far = jnp.broadcast_to(rel_bias[:, -1:], (N_HEADS, BIAS_RING - 2 * Q_BLOCK + 1))
    past = jnp.broadcast_to(rel_bias[:, :1], (N_HEADS, CHUNK))
    return jnp.concatenate([far, past, rel_bias[:, :REL_TABLE - 1]], axis=1)


def _bias_table(rel_bias):
    return pl.pallas_call(
        _bias_table_kernel,
        in_specs=[pl.BlockSpec(memory_space=pltpu.VMEM)],
        out_specs=pl.BlockSpec(memory_space=pltpu.VMEM),
        out_shape=jax.ShapeDtypeStruct((HEAD_PAIRS, KEY_SPAN, PAIR_COLS), F32),
        name="rel_bias_table",
    )(_bias_ring(rel_bias))


def _load_weights_bf16(w_in_hbm, w_out_hbm, w1_hbm, w2_hbm, wt_ref, wo_ref, w1_ref, w2_ref,
                       stage_ref, sem):
    rows, cols = stage_ref.shape[1:]
    chunks = []

    def add(src, dst, transposed=False):
        for r0 in range(0, src.shape[0], rows):
            for c0 in range(0, src.shape[1], cols):
                if transposed:
                    def store(v, r0=r0, c0=c0):
                        dst[c0:c0 + cols, r0:r0 + rows] = v.T.astype(BF16)
                else:
                    def store(v, r0=r0, c0=c0):
                        dst[r0:r0 + rows, c0:c0 + cols] = v.astype(BF16)
                chunks.append((src.at[r0:r0 + rows, c0:c0 + cols], store))

    add(w_in_hbm, wt_ref, transposed=True)
    add(w_out_hbm, wo_ref)
    add(w1_hbm, w1_ref)
    add(w2_hbm, w2_ref)

    def copy(n):
        return pltpu.make_async_copy(chunks[n][0], stage_ref.at[n % 2], sem.at[n % 2])

    copy(0).start()
    for n in range(len(chunks)):
        if n + 1 < len(chunks):
            copy(n + 1).start()
        copy(n).wait()
        chunks[n][1](stage_ref[n % 2])


def _layer_kernel(tiles_per_seq, x_ref, g1_ref, colp_ref, bias_ref, g2_ref,
                  w_in_hbm, w_out_hbm, w1_hbm, w2_hbm, o_ref,
                  wt_ref, wo_ref, w1_ref, w2_ref, stage_ref, sem,
                  carry_ref, xn_ref, yc_ref, qt_ref, kw_ref, vtw_ref, pen_ref, yt_ref, s_ref,
                  hc_ref, yn_ref, xn2_ref, hid_ref):
    t = pl.program_id(0)
    n_tiles = pl.num_programs(0) - 1
    first_tile = lax.rem(jnp.minimum(t, n_tiles - 1), tiles_per_seq) == 0

    @pl.when(t == 0)
    def _():
        _load_weights_bf16(w_in_hbm, w_out_hbm, w1_hbm, w2_hbm, wt_ref, wo_ref, w1_ref, w2_ref,
                           stage_ref, sem)
        kw_ref[...] = jnp.zeros_like(kw_ref)
        vtw_ref[...] = jnp.zeros_like(vtw_ref)
        hc_ref[...] = jnp.zeros_like(hc_ref)
        yn_ref[...] = jnp.zeros_like(yn_ref)

    @pl.when(first_tile)
    def _():
        carry_ref[...] = jnp.zeros_like(carry_ref)

    o_ref[...] = hc_ref[...] + jnp.dot(yn_ref[...], wo_ref[CONV_WIDTH:, :],
                                       preferred_element_type=F32)
    xn_ref[...] = _rms_norm_bf16(x_ref[...], g1_ref[...])

    def mlp_prologue():
        xn2_ref[...] = _rms_norm_bf16(o_ref[...], g2_ref[...])

    def mlp_up(j):
        cols = slice(j * MLP_COLS, (j + 1) * MLP_COLS)
        hid = jnp.maximum(jnp.dot(xn2_ref[...], w1_ref[:, cols], preferred_element_type=F32), 0.0)
        hid_ref[:, cols] = (hid * hid).astype(BF16)

    def mlp_down(j):
        cols = slice(j * MLP_COLS, (j + 1) * MLP_COLS)
        o_ref[:, cols] += jnp.dot(hid_ref[...], w2_ref[:, cols], preferred_element_type=F32)

    mlp_pieces = [(mlp_up, j) for j in range(D_FF // MLP_COLS)]
    mlp_pieces += [(mlp_down, j) for j in range(D_MODEL // MLP_COLS)]

    def run_mlp_piece():
        fn, j = mlp_pieces.pop(0)
        fn(j)

    def proj_t(j):
        return lax.dot_general(wt_ref[j * CONV_WIDTH:(j + 1) * CONV_WIDTH, :], xn_ref[...],
                               NT_DIMS, preferred_element_type=F32)

    def col(j):
        return jnp.concatenate([colp_ref[j]] * (SEQ_TILE // LANES), axis=1)

    kw_ref[0:SEQ_TILE, :] = kw_ref[SEQ_TILE:, :]
    vtw_ref[:, 0:SEQ_TILE] = vtw_ref[:, SEQ_TILE:]
    pen_lane = lax.broadcasted_iota(jnp.int32, (SEQ_TILE, LANES), 1) == 0
    pen_ref[0:SEQ_TILE, :] = jnp.where(pen_lane & first_tile, NEG_INF, 0.0).astype(BF16)
    pen_ref[SEQ_TILE:, :] = jnp.zeros((SEQ_TILE, LANES), BF16)

    gate_c = proj_t(1)
    gate_h = proj_t(2)
    gate_b = proj_t(0)
    mlp_prologue()
    k_t = proj_t(4)
    u = gate_c * gate_h
    ext = jnp.concatenate([carry_ref[...], u], axis=1)
    u1 = pltpu.roll(ext, 1, 1)[:, LANES:]
    u2 = pltpu.roll(ext, 2, 1)[:, LANES:]
    carry_ref[...] = u[:, SEQ_TILE - LANES:]
    y = col(COL_CONV_W) * u2 + col(COL_CONV_W + 1) * u1 + col(COL_CONV_W + 2) * u
    yc = gate_b * (y + col(COL_CONV_B))
    yc_ref[...] = (_group_norm_channels(yc) * col(COL_CONV_G)).astype(BF16).T
    q_t = proj_t(3)
    kw_ref[SEQ_TILE:, :] = (_group_norm_channels(k_t) * col(COL_K_G)).astype(BF16).T
    v_t = proj_t(5)
    for _ in range(PIECES_BEFORE_ATTENTION):
        run_mlp_piece()
    qt_ref[...] = (_group_norm_channels(q_t) * col(COL_Q_G)
                   * (HEAD_DIM ** -0.5 * LOG2_E)).astype(BF16)
    vtw_ref[:, SEQ_TILE:] = v_t.astype(BF16)

    first_head = lax.broadcasted_iota(jnp.int32, (LANES, Q_BLOCK), 0) < HEAD_DIM
    pen_rows = (lax.broadcasted_iota(jnp.int32, (LANES, PAIR_COLS), 0) == 0).astype(BF16)
    blocks = [(qb * Q_BLOCK, p) for qb in range(Q_BLOCKS) for p in range(HEAD_PAIRS)]

    def scores(n):
        k0, p = blocks[n]
        ch = slice(p * LANES, (p + 1) * LANES)
        qt = qt_ref[ch, k0:k0 + Q_BLOCK]
        zero = jnp.zeros_like(qt)
        rhs = jnp.concatenate([jnp.where(first_head, qt, zero),
                               jnp.where(first_head, zero, qt)], axis=1)
        rhs = jnp.concatenate([rhs, pen_rows], axis=0)
        lhs = jnp.concatenate([kw_ref[k0:k0 + KEY_SPAN, ch], pen_ref[k0:k0 + KEY_SPAN, :]],
                              axis=1)
        s_ref[n % 2] = jnp.dot(lhs, rhs, preferred_element_type=F32) + bias_ref[p]

    def attend(n):
        k0, p = blocks[n]
        ch = slice(p * LANES, (p + 1) * LANES)
        s = s_ref[n % 2]
        pr = jnp.exp2(s - jnp.max(s, axis=0, keepdims=True))
        denom = jnp.sum(pr, axis=0, keepdims=True)
        ot = jnp.dot(vtw_ref[ch, k0:k0 + KEY_SPAN], pr.astype(BF16),
                     preferred_element_type=F32) / denom
        yt_ref[p * LANES:p * LANES + HEAD_DIM, k0:k0 + Q_BLOCK] = ot[0:HEAD_DIM, 0:Q_BLOCK]
        yt_ref[p * LANES + HEAD_DIM:(p + 1) * LANES, k0:k0 + Q_BLOCK] = ot[HEAD_DIM:, Q_BLOCK:]

    def conv_half_out_proj(c):
        cols = slice(c * MLP_COLS, (c + 1) * MLP_COLS)
        hc_ref[:, cols] = x_ref[:, cols] + jnp.dot(yc_ref[...], wo_ref[0:CONV_WIDTH, cols],
                                                   preferred_element_type=F32)

    scores(0)
    for n in range(len(blocks)):
        if n + 1 < len(blocks):
            scores(n + 1)
        attend(n)
        if n < D_MODEL // MLP_COLS:
            conv_half_out_proj(n)
        run_mlp_piece()
    yn_ref[...] = (_group_norm_channels(yt_ref[...]) * col(COL_ATTN_G)).astype(BF16).T
    for _ in range(PIECES_CLOSING):
        run_mlp_piece()
    assert not mlp_pieces


def _layer(x, g1, col_params, bias, g2, w_in, w_out, w1, w2):
    bsz, seq, d_model = x.shape
    assert d_model == D_MODEL and seq % SEQ_TILE == 0, x.shape
    assert w_in.shape == (D_MODEL, 3 * CONV_WIDTH + 3 * ATTN_WIDTH), w_in.shape
    assert w_out.shape == (D_MODEL, D_MODEL) and w1.shape == (D_MODEL, D_FF) == w2.shape[::-1]
    in_hbm = pl.BlockSpec(memory_space=pl.ANY)
    tiles_per_seq = seq // SEQ_TILE
    n_tiles = bsz * tiles_per_seq

    def tile_of(t):
        t = jnp.minimum(t, n_tiles - 1)
        return t // tiles_per_seq, t % tiles_per_seq, 0

    def out_tile_of(t):
        return tile_of(jnp.maximum(t - 1, 0))

    return pl.pallas_call(
        functools.partial(_layer_kernel, tiles_per_seq),
        grid=(n_tiles + 1,),
        in_specs=[pl.BlockSpec((None, SEQ_TILE, D_MODEL), tile_of),
                  _resident((1, D_MODEL)), _resident((N_COL_PARAMS, CONV_WIDTH, LANES)),
                  _resident((HEAD_PAIRS, KEY_SPAN, PAIR_COLS)),
                  _resident((1, D_MODEL)), in_hbm, in_hbm, in_hbm, in_hbm],
        out_specs=pl.BlockSpec((None, SEQ_TILE, D_MODEL), out_tile_of),
        out_shape=jax.ShapeDtypeStruct(x.shape, F32),
        scratch_shapes=[pltpu.VMEM(w_in.shape[::-1], BF16),
                        pltpu.VMEM(w_out.shape, BF16),
                        pltpu.VMEM(w1.shape, BF16),
                        pltpu.VMEM(w2.shape, BF16),
                        pltpu.VMEM((2,) + WEIGHT_CHUNK, F32),
                        pltpu.SemaphoreType.DMA((2,)),
                        pltpu.VMEM((CONV_WIDTH, LANES), F32),
                        pltpu.VMEM((SEQ_TILE, D_MODEL), BF16),
                        pltpu.VMEM((SEQ_TILE, CONV_WIDTH), BF16),
                        pltpu.VMEM((ATTN_WIDTH, SEQ_TILE), BF16),
                        pltpu.VMEM((2 * SEQ_TILE, ATTN_WIDTH), BF16),
                        pltpu.VMEM((ATTN_WIDTH, 2 * SEQ_TILE), BF16),
                        pltpu.VMEM((2 * SEQ_TILE, LANES), BF16),
                        pltpu.VMEM((ATTN_WIDTH, SEQ_TILE), F32),
                        pltpu.VMEM((2, KEY_SPAN, PAIR_COLS), F32),
                        pltpu.VMEM((SEQ_TILE, D_MODEL), F32),
                        pltpu.VMEM((SEQ_TILE, ATTN_WIDTH), BF16),
                        pltpu.VMEM((SEQ_TILE, D_MODEL), BF16),
                        pltpu.VMEM((SEQ_TILE, D_FF), BF16)],
        compiler_params=pltpu.CompilerParams(
            dimension_semantics=("arbitrary",), vmem_limit_bytes=VMEM_LIMIT),
        name="layer",
    )(x, g1, col_params, bias, g2, w_in, w_out, w1, w2)


def kernel(x, norm_mix_g, w_in, conv_w, conv_b, q_norm_g, k_norm_g, rel_bias, conv_out_g,
           attn_out_g, w_out, norm_mlp_g, w_mlp_in, w_mlp_out):
    depth = w_in.shape[0]
    for l in range(depth):
        col_params = jnp.stack(
            [conv_w[l, 0], conv_w[l, 1], conv_w[l, 2], conv_b[l], conv_out_g[l],
             jnp.tile(q_norm_g[l], N_HEADS), jnp.tile(k_norm_g[l], N_HEADS), attn_out_g[l]])
        col_params = jnp.broadcast_to(col_params[:, :, None], col_params.shape + (LANES,))
        x = _layer(x, norm_mix_g[l][None], col_params, _bias_table(rel_bias[l]),
                   norm_mlp_g[l][None], w_in[l], w_out[l], w_mlp_in[l], w_mlp_out[l])
    return x
```

```python
import functools
import math

import jax
import jax.numpy as jnp
from jax import lax
from jax.experimental import pallas as pl
from jax.experimental.pallas import tpu as pltpu

D_MODEL = 1024
CHUNK = 64
LEFT_CHUNKS = 8
BAND = LEFT_CHUNKS + 1
CONV_WIDTH = D_MODEL // 2
GROUP = 64
HEAD_DIM = 64
N_HEADS = (D_MODEL - CONV_WIDTH) // HEAD_DIM
ATTN_WIDTH = N_HEADS * HEAD_DIM
REL_CLIP = 128
REL_TABLE = (CHUNK - 1) + REL_CLIP + 1
D_FF = 4 * D_MODEL
EPS = 1e-6
NEG_INF = -1e30
LOG2_E = math.log2(math.e)

LANES = 128
SEQ_TILE = LEFT_CHUNKS * CHUNK
Q_BLOCK = 2 * CHUNK
Q_BLOCKS = SEQ_TILE // Q_BLOCK
KEY_SPAN = (BAND + 1) * CHUNK
HEAD_PAIRS = N_HEADS * HEAD_DIM // LANES
PAIR_COLS = 2 * Q_BLOCK
BIAS_FLAT_ROWS = KEY_SPAN - 2 * Q_BLOCK
BIAS_RING = 4 * LANES
MLP_COLS = 256
MLP_PIECES = (D_FF + D_MODEL) // MLP_COLS
WEIGHT_CHUNK = (512, 1024)
PIECES_BEFORE_ATTENTION = 2
PIECES_CLOSING = MLP_PIECES - PIECES_BEFORE_ATTENTION - SEQ_TILE // Q_BLOCK * HEAD_PAIRS
VMEM_LIMIT = 60000 * 1024

COL_CONV_W, COL_CONV_B, COL_CONV_G, COL_Q_G, COL_K_G, COL_ATTN_G, N_COL_PARAMS = 0, 3, 4, 5, 6, 7, 8

F32 = jnp.float32
BF16 = jnp.bfloat16
NT_DIMS = (((1,), (1,)), ((), ()))


def _resident(shape):
    return pl.BlockSpec(shape, lambda *_: (0,) * len(shape), pipeline_mode=pl.Buffered(1))


def _rms_norm_bf16(x, gain):
    ms = jnp.mean(x * x, axis=-1, keepdims=True)
    return (x * lax.rsqrt(ms + EPS) * gain).astype(BF16)


def _group_norm_channels(y):
    c, t = y.shape
    y3 = y.reshape(c // GROUP, GROUP, t)
    ms = jnp.mean(y3 * y3, axis=1, keepdims=True)
    return (y3 * lax.rsqrt(ms + EPS)).reshape(c, t)


def _bias_table_kernel(ring_ref, out_ref):
    def fields(rows, r0):
        r = lax.broadcasted_iota(jnp.int32, (rows, PAIR_COLS), 0) + r0
        c = lax.broadcasted_iota(jnp.int32, (rows, PAIR_COLS), 1)
        j = (c // CHUNK) % 2
        return c < Q_BLOCK, (r >= j * CHUNK) & (r < j * CHUNK + BAND * CHUNK)

    near_rows = KEY_SPAN - BIAS_FLAT_ROWS
    first_far, in_band_far = fields(BIAS_FLAT_ROWS, 0)
    _, in_band_near = fields(near_rows, BIAS_FLAT_ROWS)
    for p in range(HEAD_PAIRS):
        far = jnp.where(first_far, ring_ref[2 * p:2 * p + 1, 0:1], ring_ref[2 * p + 1:2 * p + 2, 0:1])
        out_ref[p, 0:BIAS_FLAT_ROWS, :] = jnp.where(in_band_far, far * LOG2_E, NEG_INF)
        halves = []
        for e in range(2):
            ring = jnp.broadcast_to(ring_ref[2 * p + e:2 * p + e + 1, :], (near_rows, BIAS_RING))
            halves.append(pltpu.roll(ring, 0, 1, stride=1, stride_axis=0)[:, 0:Q_BLOCK])
        near = jnp.concatenate(halves, axis=1)
        out_ref[p, BIAS_FLAT_ROWS:, :] = jnp.where(in_band_near, near * LOG2_E, NEG_INF)


def _bias_ring(rel_bias):
    far = jnp.broadcast_to(rel_bias[:, -1:], (N_HEADS, BIAS_RING - 2 * Q_BLOCK + 1))
    past = jnp.broadcast_to(rel_bias[:, :1], (N_HEADS, CHUNK))
    return jnp.concatenate([far, past, rel_bias[:, :REL_TABLE - 1]], axis=1)


def _bias_table(rel_bias):
    return pl.pallas_call(
        _bias_table_kernel,
        in_specs=[pl.BlockSpec(memory_space=pltpu.VMEM)],
        out_specs=pl.BlockSpec(memory_space=pltpu.VMEM),
        out_shape=jax.ShapeDtypeStruct((HEAD_PAIRS, KEY_SPAN, PAIR_COLS), F32),
        name="rel_bias_table",
    )(_bias_ring(rel_bias))


def _load_weights_bf16(w_in_hbm, w_out_hbm, w1_hbm, w2_hbm, wt_ref, wo_ref, w1_ref, w2_ref,
                       stage_ref, sem):
    rows, cols = stage_ref.shape[1:]
    chunks = []

    def add(src, dst, transposed=False):
        for r0 in range(0, src.shape[0], rows):
            for c0 in range(0, src.shape[1], cols):
                if transposed:
                    def store(v, r0=r0, c0=c0):
                        dst[c0:c0 + cols, r0:r0 + rows] = v.T.astype(BF16)
                else:
                    def store(v, r0=r0, c0=c0):
                        dst[r0:r0 + rows, c0:c0 + cols] = v.astype(BF16)
                chunks.append((src.at[r0:r0 + rows, c0:c0 + cols], store))

    add(w_in_hbm, wt_ref, transposed=True)
    add(w_out_hbm, wo_ref)
    add(w1_hbm, w1_ref)
    add(w2_hbm, w2_ref)

    def copy(n):
        return pltpu.make_async_copy(chunks[n][0], stage_ref.at[n % 2], sem.at[n % 2])

    copy(0).start()
    for n in range(len(chunks)):
        if n + 1 < len(chunks):
            copy(n + 1).start()
        copy(n).wait()
        chunks[n][1](stage_ref[n % 2])


def _layer_kernel(tiles_per_seq, n_tiles, x_hbm, g1_ref, colp_ref, bias_ref, g2_ref,
                  w_in_hbm, w_out_hbm, w1_hbm, w2_hbm, o_hbm,
                  wt_ref, wo_ref, w1_ref, w2_ref, stage_ref, sem,
                  carry_ref, xn_ref, yc_ref, qt_ref, kw_ref, vtw_ref, pen_ref, yt_ref, s_ref,
                  hc_ref, yn_ref, xn2_ref, hid_ref, xbuf_ref, obuf_ref, xsem, osem):
    _load_weights_bf16(w_in_hbm, w_out_hbm, w1_hbm, w2_hbm, wt_ref, wo_ref, w1_ref, w2_ref,
                       stage_ref, sem)
    kw_ref[...] = jnp.zeros_like(kw_ref)
    vtw_ref[...] = jnp.zeros_like(vtw_ref)
    hc_ref[...] = jnp.zeros_like(hc_ref)
    yn_ref[...] = jnp.zeros_like(yn_ref)

    def hbm_tile(ref, tile):
        tile = jnp.clip(tile, 0, n_tiles - 1)
        row0 = pl.multiple_of(lax.rem(tile, tiles_per_seq) * SEQ_TILE, SEQ_TILE)
        return ref.at[tile // tiles_per_seq, pl.ds(row0, SEQ_TILE), :]

    def x_copy(t):
        return pltpu.make_async_copy(hbm_tile(x_hbm, t), xbuf_ref.at[lax.rem(t, 2)],
                                     xsem.at[lax.rem(t, 2)])

    def o_copy(t):
        return pltpu.make_async_copy(obuf_ref.at[lax.rem(t, 2)], hbm_tile(o_hbm, t - 1),
                                     osem.at[lax.rem(t, 2)])

    x_copy(0).start()

    def step(t, carry):
        x_copy(t).wait()

        @pl.when(t < n_tiles)
        def _():
            x_copy(t + 1).start()

        @pl.when(t >= 3)
        def _():
            o_copy(t - 2).wait()

        _layer_step(t, n_tiles, tiles_per_seq, xbuf_ref.at[lax.rem(t, 2)],
                    obuf_ref.at[lax.rem(t, 2)], g1_ref, colp_ref, bias_ref, g2_ref,
                    wt_ref, wo_ref, w1_ref, w2_ref, carry_ref, xn_ref, yc_ref, qt_ref, kw_ref,
                    vtw_ref, pen_ref, yt_ref, s_ref, hc_ref, yn_ref, xn2_ref, hid_ref)

        @pl.when(t >= 1)
        def _():
            o_copy(t).start()
        return carry

    lax.fori_loop(0, n_tiles + 1, step, 0)
    o_copy(n_tiles - 1).wait()
    o_copy(n_tiles).wait()


def _layer_step(t, n_tiles, tiles_per_seq, x_ref, o_ref, g1_ref, colp_ref, bias_ref, g2_ref,
                wt_ref, wo_ref, w1_ref, w2_ref,
                carry_ref, xn_ref, yc_ref, qt_ref, kw_ref, vtw_ref, pen_ref, yt_ref, s_ref,
                hc_ref, yn_ref, xn2_ref, hid_ref):
    first_tile = lax.rem(jnp.minimum(t, n_tiles - 1), tiles_per_seq) == 0

    @pl.when(first_tile)
    def _():
        carry_ref[...] = jnp.zeros_like(carry_ref)

    o_ref[...] = hc_ref[...] + jnp.dot(yn_ref[...], wo_ref[CONV_WIDTH:, :],
                                       preferred_element_type=F32)
    xn_ref[...] = _rms_norm_bf16(x_ref[...], g1_ref[...])

    def mlp_prologue():
        xn2_ref[...] = _rms_norm_bf16(o_ref[...], g2_ref[...])

    def mlp_up(j):
        cols = slice(j * MLP_COLS, (j + 1) * MLP_COLS)
        hid = jnp.maximum(jnp.dot(xn2_ref[...], w1_ref[:, cols], preferred_element_type=F32), 0.0)
        hid_ref[:, cols] = (hid * hid).astype(BF16)

    def mlp_down(j):
        cols = slice(j * MLP_COLS, (j + 1) * MLP_COLS)
        o_ref[:, cols] += jnp.dot(hid_ref[...], w2_ref[:, cols], preferred_element_type=F32)

    mlp_pieces = [(mlp_up, j) for j in range(D_FF // MLP_COLS)]
    mlp_pieces += [(mlp_down, j) for j in range(D_MODEL // MLP_COLS)]

    def run_mlp_piece():
        fn, j = mlp_pieces.pop(0)
        fn(j)

    def proj_t(j):
        return lax.dot_general(wt_ref[j * CONV_WIDTH:(j + 1) * CONV_WIDTH, :], xn_ref[...],
                               NT_DIMS, preferred_element_type=F32)

    def col(j):
        return jnp.concatenate([colp_ref[j]] * (SEQ_TILE // LANES), axis=1)

    kw_ref[0:SEQ_TILE, :] = kw_ref[SEQ_TILE:, :]
    vtw_ref[:, 0:SEQ_TILE] = vtw_ref[:, SEQ_TILE:]
    pen_lane = lax.broadcasted_iota(jnp.int32, (SEQ_TILE, LANES), 1) == 0
    pen_ref[0:SEQ_TILE, :] = jnp.where(pen_lane & first_tile, NEG_INF, 0.0).astype(BF16)
    pen_ref[SEQ_TILE:, :] = jnp.zeros((SEQ_TILE, LANES), BF16)

    gate_c = proj_t(1)
    gate_h = proj_t(2)
    gate_b = proj_t(0)
    mlp_prologue()
    k_t = proj_t(4)
    u = gate_c * gate_h
    ext = jnp.concatenate([carry_ref[...], u], axis=1)
    u1 = pltpu.roll(ext, 1, 1)[:, LANES:]
    u2 = pltpu.roll(ext, 2, 1)[:, LANES:]
    carry_ref[...] = u[:, SEQ_TILE - LANES:]
    y = col(COL_CONV_W) * u2 + col(COL_CONV_W + 1) * u1 + col(COL_CONV_W + 2) * u
    yc = gate_b * (y + col(COL_CONV_B))
    yc_ref[...] = (_group_norm_channels(yc) * col(COL_CONV_G)).astype(BF16).T
    q_t = proj_t(3)
    kw_ref[SEQ_TILE:, :] = (_group_norm_channels(k_t) * col(COL_K_G)).astype(BF16).T
    v_t = proj_t(5)
    for _ in range(PIECES_BEFORE_ATTENTION):
        run_mlp_piece()
    qt_ref[...] = (_group_norm_channels(q_t) * col(COL_Q_G)
                   * (HEAD_DIM ** -0.5 * LOG2_E)).astype(BF16)
    vtw_ref[:, SEQ_TILE:] = v_t.astype(BF16)

    first_head = lax.broadcasted_iota(jnp.int32, (LANES, Q_BLOCK), 0) < HEAD_DIM
    pen_rows = (lax.broadcasted_iota(jnp.int32, (LANES, PAIR_COLS), 0) == 0).astype(BF16)
    blocks = [(qb * Q_BLOCK, p) for qb in range(Q_BLOCKS) for p in range(HEAD_PAIRS)]

    def scores(n):
        k0, p = blocks[n]
        ch = slice(p * LANES, (p + 1) * LANES)
        qt = qt_ref[ch, k0:k0 + Q_BLOCK]
        zero = jnp.zeros_like(qt)
        rhs = jnp.concatenate([jnp.where(first_head, qt, zero),
                               jnp.where(first_head, zero, qt)], axis=1)
        rhs = jnp.concatenate([rhs, pen_rows], axis=0)
        lhs = jnp.concatenate([kw_ref[k0:k0 + KEY_SPAN, ch], pen_ref[k0:k0 + KEY_SPAN, :]],
                              axis=1)
        s_ref[n % 2] = jnp.dot(lhs, rhs, preferred_element_type=F32) + bias_ref[p]

    def attend(n):
        k0, p = blocks[n]
        ch = slice(p * LANES, (p + 1) * LANES)
        s = s_ref[n % 2]
        pr = jnp.exp2(s - jnp.max(s, axis=0, keepdims=True))
        denom = jnp.sum(pr, axis=0, keepdims=True)
        ot = jnp.dot(vtw_ref[ch, k0:k0 + KEY_SPAN], pr.astype(BF16),
                     preferred_element_type=F32) / denom
        yt_ref[p * LANES:p * LANES + HEAD_DIM, k0:k0 + Q_BLOCK] = ot[0:HEAD_DIM, 0:Q_BLOCK]
        yt_ref[p * LANES + HEAD_DIM:(p + 1) * LANES, k0:k0 + Q_BLOCK] = ot[HEAD_DIM:, Q_BLOCK:]

    def conv_half_out_proj(c):
        cols = slice(c * MLP_COLS, (c + 1) * MLP_COLS)
        hc_ref[:, cols] = x_ref[:, cols] + jnp.dot(yc_ref[...], wo_ref[0:CONV_WIDTH, cols],
                                                   preferred_element_type=F32)

    scores(0)
    for n in range(len(blocks)):
        if n + 1 < len(blocks):
            scores(n + 1)
        attend(n)
        if n < D_MODEL // MLP_COLS:
            conv_half_out_proj(n)
        run_mlp_piece()
    yn_ref[...] = (_group_norm_channels(yt_ref[...]) * col(COL_ATTN_G)).astype(BF16).T
    for _ in range(PIECES_CLOSING):
        run_mlp_piece()
    assert not mlp_pieces


def _layer(x, g1, col_params, bias, g2, w_in, w_out, w1, w2):
    bsz, seq, d_model = x.shape
    assert d_model == D_MODEL and seq % SEQ_TILE == 0, x.shape
    assert w_in.shape == (D_MODEL, 3 * CONV_WIDTH + 3 * ATTN_WIDTH), w_in.shape
    assert w_out.shape == (D_MODEL, D_MODEL) and w1.shape == (D_MODEL, D_FF) == w2.shape[::-1]
    in_hbm = pl.BlockSpec(memory_space=pl.ANY)
    in_vmem = pl.BlockSpec(memory_space=pltpu.VMEM)
    tiles_per_seq = seq // SEQ_TILE
    n_tiles = bsz * tiles_per_seq
    assert n_tiles >= 2, x.shape

    return pl.pallas_call(
        functools.partial(_layer_kernel, tiles_per_seq, n_tiles),
        in_specs=[in_hbm, in_vmem, in_vmem, in_vmem, in_vmem, in_hbm, in_hbm, in_hbm, in_hbm],
        out_specs=in_hbm,
        out_shape=jax.ShapeDtypeStruct(x.shape, F32),
        scratch_shapes=[pltpu.VMEM(w_in.shape[::-1], BF16),
                        pltpu.VMEM(w_out.shape, BF16),
                        pltpu.VMEM(w1.shape, BF16),
                        pltpu.VMEM(w2.shape, BF16),
                        pltpu.VMEM((2,) + WEIGHT_CHUNK, F32),
                        pltpu.SemaphoreType.DMA((2,)),
                        pltpu.VMEM((CONV_WIDTH, LANES), F32),
                        pltpu.VMEM((SEQ_TILE, D_MODEL), BF16),
                        pltpu.VMEM((SEQ_TILE, CONV_WIDTH), BF16),
                        pltpu.VMEM((ATTN_WIDTH, SEQ_TILE), BF16),
                        pltpu.VMEM((2 * SEQ_TILE, ATTN_WIDTH), BF16),
                        pltpu.VMEM((ATTN_WIDTH, 2 * SEQ_TILE), BF16),
                        pltpu.VMEM((2 * SEQ_TILE, LANES), BF16),
                        pltpu.VMEM((ATTN_WIDTH, SEQ_TILE), F32),
                        pltpu.VMEM((2, KEY_SPAN, PAIR_COLS), F32),
                        pltpu.VMEM((SEQ_TILE, D_MODEL), F32),
                        pltpu.VMEM((SEQ_TILE, ATTN_WIDTH), BF16),
                        pltpu.VMEM((SEQ_TILE, D_MODEL), BF16),
                        pltpu.VMEM((SEQ_TILE, D_FF), BF16),
                        pltpu.VMEM((2, SEQ_TILE, D_MODEL), F32),
                        pltpu.VMEM((2, SEQ_TILE, D_MODEL), F32),
                        pltpu.SemaphoreType.DMA((2,)),
                        pltpu.SemaphoreType.DMA((2,))],
        compiler_params=pltpu.CompilerParams(vmem_limit_bytes=VMEM_LIMIT),
        name="layer",
    )(x, g1, col_params, bias, g2, w_in, w_out, w1, w2)


def kernel(x, norm_mix_g, w_in, conv_w, conv_b, q_norm_g, k_norm_g, rel_bias, conv_out_g,
           attn_out_g, w_out, norm_mlp_g, w_mlp_in, w_mlp_out):
    depth = w_in.shape[0]
    for l in range(depth):
        col_params = jnp.stack(
            [conv_w[l, 0], conv_w[l, 1], conv_w[l, 2], conv_b[l], conv_out_g[l],
             jnp.tile(q_norm_g[l], N_HEADS), jnp.tile(k_norm_g[l], N_HEADS), attn_out_g[l]])
        col_params = jnp.broadcast_to(col_params[:, :, None], col_params.shape + (LANES,))
        x = _layer(x, norm_mix_g[l][None], col_params, _bias_table(rel_bias[l]),
                   norm_mlp_g[l][None], w_in[l], w_out[l], w_mlp_in[l], w_mlp_out[l])
    return x
```
